```python
import math
import jax, jax.numpy as jnp
from jax import lax
import numpy as np

D_MODEL = 2048
BATCH = 4
SEQ = 2048
DEPTH = 2
DEC_BATCH = 32
DEC_SEQ = 4
PAST_LEN = 8192
PAGE_SIZE = 128

N_MIXERS = 2
N_GLA_LAYERS = (DEPTH + 1) // 2
N_MOBA_LAYERS = DEPTH // 2
MIX_WIDTH = D_MODEL
HEAD_DIM = 128
N_MEM = 256
XA_HEADS = 4
XA_WIDTH = XA_HEADS * HEAD_DIM
TOK_WIDTH = MIX_WIDTH - XA_WIDTH
GLA_HEADS = 6
GLA_DV = TOK_WIDTH // GLA_HEADS
GLA_DK = GLA_DV // 2
GLA_RANK = 16
GLA_TAU = 16.0
GLA_CHUNK = 64
MOBA_HEADS = TOK_WIDTH // HEAD_DIM
MOBA_BLOCK = 256
MOBA_TOPK = 3
MOBA_QBLOCK = 32
REL_BUCKETS = 32
REL_MAX_DIST = 128
PEER_HEADS = 8
PEER_NKEYS = 128
PEER_EXPERTS = PEER_NKEYS * PEER_NKEYS
PEER_QDIM = 256
PEER_TOPK = 16
PEER_TBLOCK = 128
EPS = 1e-6

kernel_name = "gla_moba_peer_hybrid_step"

F32 = jnp.float32


def rmsnorm(x, g):
    x32 = x.astype(F32)
    y = x32 * lax.rsqrt(jnp.mean(x32 * x32, axis=-1, keepdims=True) + EPS)
    return y.astype(x.dtype) * g


def gla_chunk(S, inp):
    q, k, v, g = inp
    C = q.shape[1]
    b = jnp.cumsum(g, axis=1)
    o_inter = jnp.einsum('bchk,bhkv->bchv', q * jnp.exp(b), S)
    causal = jnp.tril(jnp.ones((C, C), bool))
    diff = b[:, :, None] - b[:, None, :]
    decay = jnp.exp(jnp.where(causal[None, :, :, None, None], diff, -jnp.inf))
    att = jnp.einsum('bthk,bshk,btshk->bths', q, k, decay)
    o_intra = jnp.einsum('bths,bshv->bthv', att, v)
    b_last = b[:, -1]
    S_new = jnp.exp(b_last)[..., None] * S + jnp.einsum('bshk,bshv->bhkv', k * jnp.exp(b_last[:, None] - b), v)
    return S_new, o_inter + o_intra


def gla_scan(q, k, v, g, S0, chunk):
    B, T, H, _ = q.shape
    n = T // chunk
    to_chunks = lambda a: a.reshape(B, n, chunk, H, a.shape[-1]).swapaxes(0, 1)
    S, o = lax.scan(gla_chunk, S0, (to_chunks(q), to_chunks(k), to_chunks(v), to_chunks(g)))
    return o.swapaxes(0, 1).reshape(B, T, H, GLA_DV), S


def gla_block(h, w_in, w_alpha, b_alpha, g_head, S0, chunk):
    B, T, _ = h.shape
    dq, dv = GLA_HEADS * GLA_DK, GLA_HEADS * GLA_DV
    p = h @ w_in
    q = p[..., :dq]
    k = p[..., dq:2 * dq]
    v = p[..., 2 * dq:2 * dq + dv]
    r = p[..., 2 * dq + dv:2 * dq + 2 * dv]
    a = p[..., 2 * dq + 2 * dv:2 * dq + 2 * dv + GLA_RANK]
    qx = p[..., 2 * dq + 2 * dv + GLA_RANK:]
    g = jax.nn.log_sigmoid((a @ w_alpha + b_alpha).astype(F32)) / GLA_TAU
    heads = lambda t, d: t.reshape(B, T, GLA_HEADS, d).astype(F32)
    o, S = gla_scan(heads(q, GLA_DK) * (GLA_DK ** -0.5), heads(k, GLA_DK), heads(v, GLA_DV),
                    g.reshape(B, T, GLA_HEADS, GLA_DK), S0.astype(F32), chunk)
    o = rmsnorm(o, g_head) * jax.nn.silu(r.astype(F32)).reshape(B, T, GLA_HEADS, GLA_DV)
    return o.reshape(B, T, dv).astype(h.dtype), qx.reshape(B, T, XA_HEADS, HEAD_DIM), S


def rel_bucket(rel):
    n = jnp.maximum(rel, 0)
    max_exact = REL_BUCKETS // 2
    nf = jnp.maximum(n, 1).astype(F32)
    large = max_exact + (jnp.log(nf / max_exact) / math.log(REL_MAX_DIST / max_exact)
                         * (REL_BUCKETS - max_exact)).astype(jnp.int32)
    large = jnp.minimum(large, REL_BUCKETS - 1)
    return jnp.where(n < max_exact, n, large)


def moba_seq(q, kf, vf, q_pos, rel_bias):
    T, H, hd = q.shape
    nb = kf.shape[0] // MOBA_BLOCK
    kb = kf.reshape(nb, MOBA_BLOCK, H, hd).transpose(2, 0, 1, 3)
    vb = vf.reshape(nb, MOBA_BLOCK, H, hd).transpose(2, 0, 1, 3)
    kmean = jnp.mean(kb.astype(F32), axis=2)
    n_sel = max(1, min(MOBA_TOPK, nb - 1))
    qb = MOBA_QBLOCK if T % MOBA_QBLOCK == 0 else T
    bias_t = rel_bias.T
    offs = jnp.arange(MOBA_BLOCK, dtype=jnp.int32)
    scale = HEAD_DIM ** -0.5

    def step(args):
        qc, pc = args
        own = pc // MOBA_BLOCK
        gate = jnp.einsum('thd,hnd->htn', qc.astype(F32), kmean)
        fully_past = jnp.arange(nb)[None, None, :] < own[None, :, None]
        gate = jnp.where(fully_past, gate, -jnp.inf)
        _, sel = lax.top_k(gate, n_sel)
        valid = jnp.concatenate([sel < own[None, :, None], jnp.ones((H, qb, 1), bool)], axis=-1)
        blk = jnp.concatenate([sel, jnp.broadcast_to(own[None, :, None], (H, qb, 1))], axis=-1)
        ksel = jax.vmap(lambda a, i: a[i])(kb, blk)
        vsel = jax.vmap(lambda a, i: a[i])(vb, blk)
        kpos = blk[..., None] * MOBA_BLOCK + offs
        rel = pc[None, :, None, None] - kpos
        bias = jax.vmap(lambda t, i: t[i])(bias_t, rel_bucket(rel))
        logits = jnp.einsum('thd,htnsd->htns', qc, ksel).astype(F32) * scale + bias
        logits = jnp.where(valid[..., None] & (rel >= 0), logits, -jnp.inf)
        p = jax.nn.softmax(logits.reshape(H, qb, -1), axis=-1).astype(vsel.dtype)
        return jnp.einsum('htn,htnd->thd', p, vsel.reshape(H, qb, -1, hd))

    out = lax.map(step, (q.reshape(T // qb, qb, H, hd), q_pos.reshape(T // qb, qb)))
    return out.reshape(T, H, hd)


def moba_proj(h, w_in):
    B, T, _ = h.shape
    p = h @ w_in
    sh = lambda t: t.reshape(B, T, MOBA_HEADS, HEAD_DIM)
    return (sh(p[..., :TOK_WIDTH]), sh(p[..., TOK_WIDTH:2 * TOK_WIDTH]), sh(p[..., 2 * TOK_WIDTH:3 * TOK_WIDTH]),
            p[..., 3 * TOK_WIDTH:].reshape(B, T, XA_HEADS, HEAD_DIM))


def moba_prompt(q, k, v, rel_bias):
    B, T, H, hd = q.shape
    lpad = -(-T // MOBA_BLOCK) * MOBA_BLOCK
    pad = ((0, lpad - T), (0, 0), (0, 0))
    pos = jnp.arange(T, dtype=jnp.int32)

    def one(args):
        qs, ks, vs = args
        return moba_seq(qs, jnp.pad(ks, pad), jnp.pad(vs, pad), pos, rel_bias)

    return lax.map(one, (q, k, v)).reshape(B, T, H * hd)


def moba_sample(q, k, v, cache_k, cache_v, page_table, li, rel_bias):
    B, T, H, hd = q.shape
    past = page_table.shape[1] * cache_k.shape[2]
    total = past + T
    lpad = -(-total // MOBA_BLOCK) * MOBA_BLOCK
    pad = ((0, lpad - total), (0, 0), (0, 0))
    pos = past + jnp.arange(T, dtype=jnp.int32)

    def one(args):
        qs, ks, vs, pt = args
        kp = cache_k[pt, li].reshape(past, H, hd)
        vp = cache_v[pt, li].reshape(past, H, hd)
        kf = jnp.pad(jnp.concatenate([kp, ks.astype(kp.dtype)], axis=0), pad)
        vf = jnp.pad(jnp.concatenate([vp, vs.astype(vp.dtype)], axis=0), pad)
        return moba_seq(qs, kf, vf, pos, rel_bias)

    return lax.map(one, (q, k, v, page_table)).reshape(B, T, H * hd)


def mem_kv(mem, g, w):
    B = mem.shape[0]
    kv = rmsnorm(mem, g) @ w
    return (kv[..., :XA_WIDTH].reshape(B, N_MEM, XA_HEADS, HEAD_DIM),
            kv[..., XA_WIDTH:].reshape(B, N_MEM, XA_HEADS, HEAD_DIM))


def cross_attend(qx, mk, mv):
    B, T = qx.shape[:2]
    s = jnp.einsum('bthd,bmhd->bhtm', qx, mk).astype(F32) * (HEAD_DIM ** -0.5)
    p = jax.nn.softmax(s, axis=-1).astype(mv.dtype)
    return jnp.einsum('bhtm,bmhd->bthd', p, mv).reshape(B, T, XA_WIDTH)


def peer(xn, w_q, subkeys, u, v):
    B, T, D = xn.shape
    n = B * T
    tb = PEER_TBLOCK if n % PEER_TBLOCK == 0 else n
    kk = PEER_TOPK

    def block(xb):
        q = (xb @ w_q).reshape(tb, PEER_HEADS, 2, PEER_QDIM // 2).astype(F32)
        s = jnp.einsum('thcd,hcnd->thcn', q, subkeys.astype(F32))
        sv, si = lax.top_k(s, kk)
        cand = sv[..., 0, :, None] + sv[..., 1, None, :]
        cidx = si[..., 0, :, None] * PEER_NKEYS + si[..., 1, None, :]
        fv, fi = lax.top_k(cand.reshape(tb, PEER_HEADS, kk * kk), kk)
        eidx = jnp.take_along_axis(cidx.reshape(tb, PEER_HEADS, kk * kk), fi, axis=-1)
        gate = jax.nn.softmax(fv, axis=-1)
        ue = u[eidx]
        act = jax.nn.gelu(jnp.einsum('thkd,td->thk', ue, xb).astype(F32), approximate=False)
        ve = v[eidx]
        return jnp.einsum('thk,thkd->td', (gate * act).astype(ve.dtype), ve)

    y = lax.map(block, xn.reshape(n // tb, tb, D))
    return y.reshape(B, T, D).astype(xn.dtype)


def setup_inputs(seed: int = 0) -> dict:
    key = jax.random.key(seed)
    ks = jax.random.split(key, 32)
    n_pages = PAST_LEN // PAGE_SIZE
    n_used = DEC_BATCH * n_pages
    n_pool = n_used + (n_used + 3) // 4
    nrm = lambda k, shape, s: jax.random.normal(k, shape, F32) * s
    gain = lambda k, shape: 1.0 + 0.02 * jax.random.normal(k, shape, F32)
    gla_in = 2 * GLA_HEADS * GLA_DK + 2 * GLA_HEADS * GLA_DV + GLA_RANK + XA_WIDTH
    moba_in = 3 * TOK_WIDTH + XA_WIDTH
    perm = jax.random.permutation(ks[0], n_pool)
    page_table = perm[:n_used].reshape(DEC_BATCH, n_pages).astype(jnp.int32)
    return {
        "x_prompt": nrm(ks[1], (BATCH, SEQ, D_MODEL), 1.0),
        "x_sample": nrm(ks[2], (DEC_BATCH, DEC_SEQ, D_MODEL), 1.0),
        "state_gla": nrm(ks[3], (DEC_BATCH, N_GLA_LAYERS, GLA_HEADS, GLA_DK, GLA_DV), 0.3),
        "cache_moba_k": nrm(ks[4], (n_pool, N_MOBA_LAYERS, PAGE_SIZE, MOBA_HEADS, HEAD_DIM), 1.0),
        "cache_moba_v": nrm(ks[5], (n_pool, N_MOBA_LAYERS, PAGE_SIZE, MOBA_HEADS, HEAD_DIM), 1.0),
        "cache_mem_k": nrm(ks[6], (DEC_BATCH, DEPTH, N_MEM, XA_HEADS, HEAD_DIM), 1.0),
        "cache_mem_v": nrm(ks[7], (DEC_BATCH, DEPTH, N_MEM, XA_HEADS, HEAD_DIM), 1.0),
        "page_table": page_table,
        "mem_prompt": nrm(ks[8], (BATCH, N_MEM, D_MODEL), 1.0),
        "ln_mix": gain(ks[9], (DEPTH, D_MODEL)),
        "ln_mem": gain(ks[10], (DEPTH, D_MODEL)),
        "ln_ffn": gain(ks[11], (DEPTH, D_MODEL)),
        "ln_final": gain(ks[12], (D_MODEL,)),
        "w_in_gla": nrm(ks[13], (N_GLA_LAYERS, D_MODEL, gla_in), D_MODEL ** -0.5),
        "w_alpha_gla": nrm(ks[14], (N_GLA_LAYERS, GLA_RANK, GLA_HEADS * GLA_DK), GLA_RANK ** -0.5),
        "b_alpha_gla": nrm(ks[15], (N_GLA_LAYERS, GLA_HEADS * GLA_DK), 0.1),
        "g_head_gla": gain(ks[16], (N_GLA_LAYERS, GLA_DV)),
        "w_in_moba": nrm(ks[17], (N_MOBA_LAYERS, D_MODEL, moba_in), D_MODEL ** -0.5),
        "rel_bias": nrm(ks[18], (REL_BUCKETS, MOBA_HEADS), 0.5),
        "w_mem_kv": nrm(ks[19], (DEPTH, D_MODEL, 2 * XA_WIDTH), D_MODEL ** -0.5),
        "w_out": nrm(ks[20], (DEPTH, MIX_WIDTH, D_MODEL), 0.5 * MIX_WIDTH ** -0.5),
        "w_peer_q": nrm(ks[21], (DEPTH, D_MODEL, PEER_HEADS * PEER_QDIM), D_MODEL ** -0.5),
        "peer_subkeys": nrm(ks[22], (DEPTH, PEER_HEADS, 2, PEER_NKEYS, PEER_QDIM // 2), (PEER_QDIM // 2) ** -0.5),
        "peer_u": nrm(ks[23], (DEPTH, PEER_EXPERTS, D_MODEL), D_MODEL ** -0.5),
        "peer_v": nrm(ks[24], (DEPTH, PEER_EXPERTS, D_MODEL), 0.25),
    }


def reference(x_prompt, x_sample, state_gla, cache_moba_k, cache_moba_v, cache_mem_k, cache_mem_v, page_table,
              mem_prompt, ln_mix, ln_mem, ln_ffn, ln_final, w_in_gla, w_alpha_gla, b_alpha_gla, g_head_gla,
              w_in_moba, rel_bias, w_mem_kv, w_out, w_peer_q, peer_subkeys, peer_u, peer_v):
    xp, xs = x_prompt, x_sample
    B = xp.shape[0]
    ts = xs.shape[1]
    gla_p, gla_s = [], []
    kp_l, vp_l, ks_l, vs_l = [], [], [], []
    mkp_l, mvp_l = [], []
    for i in range(DEPTH):
        j = i // N_MIXERS
        hp = rmsnorm(xp, ln_mix[i])
        hs = rmsnorm(xs, ln_mix[i])
        mk_p, mv_p = mem_kv(mem_prompt, ln_mem[i], w_mem_kv[i])
        mkp_l.append(mk_p)
        mvp_l.append(mv_p)
        mk_s, mv_s = cache_mem_k[:, i], cache_mem_v[:, i]
        if i % N_MIXERS == 0:
            S0 = jnp.zeros((B, GLA_HEADS, GLA_DK, GLA_DV), F32)
            mix_p, qx_p, S_p = gla_block(hp, w_in_gla[j], w_alpha_gla[j], b_alpha_gla[j], g_head_gla[j], S0, GLA_CHUNK)
            mix_s, qx_s, S_s = gla_block(hs, w_in_gla[j], w_alpha_gla[j], b_alpha_gla[j], g_head_gla[j],
                                         state_gla[:, j], ts)
            gla_p.append(S_p.astype(xp.dtype))
            gla_s.append(S_s.astype(state_gla.dtype))
        else:
            q_p, k_p, v_p, qx_p = moba_proj(hp, w_in_moba[j])
            mix_p = moba_prompt(q_p, k_p, v_p, rel_bias)
            q_s, k_s, v_s, qx_s = moba_proj(hs, w_in_moba[j])
            mix_s = moba_sample(q_s, k_s, v_s, cache_moba_k, cache_moba_v, page_table, j, rel_bias)
            kp_l.append(k_p)
            vp_l.append(v_p)
            ks_l.append(k_s)
            vs_l.append(v_s)
        xp = xp + jnp.concatenate([mix_p, cross_attend(qx_p, mk_p, mv_p)], axis=-1) @ w_out[i]
        xs = xs + jnp.concatenate([mix_s, cross_attend(qx_s, mk_s, mv_s)], axis=-1) @ w_out[i]
        xp = xp + peer(rmsnorm(xp, ln_ffn[i]), w_peer_q[i], peer_subkeys[i], peer_u[i], peer_v[i])
        xs = xs + peer(rmsnorm(xs, ln_ffn[i]), w_peer_q[i], peer_subkeys[i], peer_u[i], peer_v[i])
    y_prompt = rmsnorm(xp, ln_final)
    y_sample = rmsnorm(xs, ln_final)
    return (y_prompt, y_sample,
            jnp.stack(gla_p, axis=1), jnp.stack(gla_s, axis=1),
            jnp.stack(kp_l, axis=1), jnp.stack(vp_l, axis=1),
            jnp.stack(ks_l, axis=1), jnp.stack(vs_l, axis=1),
            jnp.stack(mkp_l, axis=1), jnp.stack(mvp_l, axis=1))
```

```python
import functools
import math

import numpy as np
import jax
import jax.numpy as jnp
from jax import lax
from jax.experimental import pallas as pl
from jax.experimental.pallas import tpu as pltpu

D_MODEL = 2048
DEPTH = 2
N_MIXERS = 2
HEAD_DIM = 128
N_MEM = 256
XA_HEADS = 4
XA_WIDTH = XA_HEADS * HEAD_DIM
TOK_WIDTH = D_MODEL - XA_WIDTH
GLA_HEADS = 6
GLA_DV = TOK_WIDTH // GLA_HEADS
GLA_DK = GLA_DV // 2
GLA_RANK = 16
GLA_TAU = 16.0
GLA_CHUNK = 64
MOBA_HEADS = TOK_WIDTH // HEAD_DIM
MOBA_BLOCK = 256
MOBA_TOPK = 3
REL_BUCKETS = 32
REL_MAX_DIST = 128
PEER_HEADS = 8
PEER_NKEYS = 128
PEER_QDIM = 256
PEER_TOPK = 16
EPS = 1e-6

F32 = jnp.float32
BF16 = jnp.bfloat16

LANE = 128
SUBLANE = 8
VMEM_CAP_BYTES = 56 * 1024 * 1024

NEG = -1e30
_NT = (((1,), (1,)), ((), ()))
_TN = (((0,), (0,)), ((), ()))

GLA_SAMPLE_CHUNK = 16
SAMPLE_ROWS = 8


def _nt(a, b):
    return lax.dot_general(a, b, _NT, preferred_element_type=F32)


def _tn(a, b):
    return lax.dot_general(a, b, _TN, preferred_element_type=F32)


def _split3(x):
    x1 = x.astype(BF16)
    r1 = x - x1.astype(F32)
    x2 = r1.astype(BF16)
    x3 = (r1 - x2.astype(F32)).astype(BF16)
    return x1, x2, x3


def _nt_hi(a, b):
    a1 = a.astype(BF16)
    a2 = (a - a1.astype(F32)).astype(BF16)
    b1 = b.astype(BF16)
    b2 = (b - b1.astype(F32)).astype(BF16)
    return _nt(a1, b1) + _nt(a1, b2) + _nt(a2, b1)


def _vmem_limit(block_bytes, scratch_bytes=0, temp_bytes=0):
    return int(min(VMEM_CAP_BYTES, 2 * block_bytes + scratch_bytes + temp_bytes + (4 << 20)))


def _params(sem, vmem):
    return pltpu.CompilerParams(dimension_semantics=sem, vmem_limit_bytes=vmem)


def _norm_matmul_kernel(x_ref, g_ref, w_ref, o_ref, xn_ref):
    @pl.when(pl.program_id(1) == 0)
    def _():
        x = x_ref[...]
        y = x * lax.rsqrt(jnp.mean(x * x, axis=-1, keepdims=True) + EPS)
        xn_ref[...] = (y * g_ref[...]).astype(BF16)

    o_ref[...] = jnp.dot(xn_ref[...], w_ref[...], preferred_element_type=F32)


def norm_matmul(x, g, w, *, tm, tn, emit_xn=False):
    m, d = x.shape
    n = w.shape[1]
    assert m % tm == 0 and n % tn == 0 and w.shape[0] == d
    o_shape = jax.ShapeDtypeStruct((m, n), F32)
    o_spec = pl.BlockSpec((tm, tn), lambda i, j: (i, j))
    xn_bytes = tm * d * 2
    blocks = tm * d * 4 + d * 4 + d * tn * 2 + tm * tn * 4
    kwargs = dict(
        grid=(m // tm, n // tn),
        in_specs=[
            pl.BlockSpec((tm, d), lambda i, j: (i, 0)),
            pl.BlockSpec((1, d), lambda i, j: (0, 0)),
            pl.BlockSpec((d, tn), lambda i, j: (0, j)),
        ],
        name="norm_matmul",
    )
    if emit_xn:
        return pl.pallas_call(
            _norm_matmul_kernel,
            out_shape=(o_shape, jax.ShapeDtypeStruct((m, d), BF16)),
            out_specs=(o_spec, pl.BlockSpec((tm, d), lambda i, j: (i, 0))),
            compiler_params=_params(("parallel", "arbitrary"), _vmem_limit(blocks + xn_bytes, 0, tm * d * 8)),
            **kwargs,
        )(x, g.reshape(1, d), w)
    return pl.pallas_call(
        _norm_matmul_kernel,
        out_shape=o_shape,
        out_specs=o_spec,
        scratch_shapes=[pltpu.VMEM((tm, d), BF16)],
        compiler_params=_params(("parallel", "arbitrary"), _vmem_limit(blocks, xn_bytes, tm * d * 8)),
        **kwargs,
    )(x, g.reshape(1, d), w)


def _rmsnorm_kernel(x_ref, g_ref, o_ref):
    x = x_ref[...]
    o_ref[...] = x * lax.rsqrt(jnp.mean(x * x, axis=-1, keepdims=True) + EPS) * g_ref[...]


def rmsnorm_rows(x, g, *, tm):
    m, d = x.shape
    return pl.pallas_call(
        _rmsnorm_kernel,
        out_shape=jax.ShapeDtypeStruct((m, d), F32),
        grid=(m // tm,),
        in_specs=[pl.BlockSpec((tm, d), lambda i: (i, 0)), pl.BlockSpec((1, d), lambda i: (0, 0))],
        out_specs=pl.BlockSpec((tm, d), lambda i: (i, 0)),
        compiler_params=_params(("parallel",), _vmem_limit(2 * tm * d * 4, 0, tm * d * 4)),
        name="final_rmsnorm",
    )(x, g.reshape(1, d))


def _gla_constants(chunk):
    nlev = int(round(math.log2(chunk)))
    assert 1 << nlev == chunk
    idx = np.arange(chunk)
    col, row = idx[None, :], idx[:, None]
    mats = [col <= row, col > row]
    masks = []
    for lev in range(nlev):
        m = chunk >> (lev + 1)
        grp = idx // (2 * m)
        mid = grp * 2 * m + m
        up = (idx % (2 * m)) >= m
        mats.append((col >= mid[:, None]) & (col <= row) & up[:, None])
        mats.append((col > row) & (col < mid[:, None]) & (~up)[:, None])
        masks.append((grp[:, None] == grp[None, :]) & up[:, None] & (~up)[None, :])
    return (np.concatenate(mats, axis=0).astype(np.float32), np.stack(masks).astype(np.float32))


def _gla_kernel(q_ref, k_ref, v_ref, r_ref, a_ref, wa_ref, ba_ref, gh_ref, s0_ref, cmat_ref, lmask_ref,
                o_ref, sout_ref, st_ref, *, chunk, n_chunks, t_valid, t_total):
    c_rows = chunk
    nlev = int(round(math.log2(chunk)))
    st_ref[...] = jnp.transpose(s0_ref[0, 0, 0])

    def body(c, carry):
        r0 = pl.multiple_of(c * c_rows, c_rows)
        rows = pl.ds(r0, c_rows)
        q = q_ref[0, rows, :] * (GLA_DK ** -0.5)
        k = k_ref[0, rows, :]
        v = v_ref[0, rows, :]
        z = jnp.dot(a_ref[0, rows, :].astype(BF16), wa_ref[...], preferred_element_type=F32) + ba_ref[...]
        g = (jnp.minimum(z, 0.0) - jnp.log1p(jnp.exp(-jnp.abs(z)))) * (1.0 / GLA_TAU)
        if t_valid < t_total:
            t_idx = r0 + lax.broadcasted_iota(jnp.int32, (c_rows, 1), 0)
            g = jnp.where(t_idx < t_valid, g, 0.0)
        cm = cmat_ref[...]
        g1, g2, g3 = _split3(g)
        e = (jnp.dot(cm, g1, preferred_element_type=F32) + jnp.dot(cm, g2, preferred_element_type=F32)
             + jnp.dot(cm, g3, preferred_element_type=F32))
        qd = (q * jnp.exp(e[0:c_rows])).astype(BF16)
        kd = (k * jnp.exp(e[c_rows:2 * c_rows])).astype(BF16)
        att = jnp.zeros((c_rows, c_rows), F32)
        for lev in range(nlev):
            ea = e[(2 + 2 * lev) * c_rows:(3 + 2 * lev) * c_rows]
            eb = e[(3 + 2 * lev) * c_rows:(4 + 2 * lev) * c_rows]
            ql = (q * jnp.exp(ea)).astype(BF16)
            kl = (k * jnp.exp(eb)).astype(BF16)
            att = att + lmask_ref[lev] * _nt(ql, kl)
        vb = v.astype(BF16)
        st = st_ref[...]
        o = jnp.dot(att.astype(BF16), vb, preferred_element_type=F32)
        o = o + jnp.sum(q * k, axis=-1, keepdims=True) * v
        o = o + _nt(qd, st.astype(BF16))
        b_last = e[c_rows - 1:c_rows, :]
        st_ref[...] = st * jnp.exp(b_last) + _tn(vb, kd)
        on = o * lax.rsqrt(jnp.mean(o * o, axis=-1, keepdims=True) + EPS) * gh_ref[...]
        rr = r_ref[0, rows, :]
        o_ref[0, rows, :] = (on * (rr * jax.nn.sigmoid(rr))).astype(BF16)
        return carry

    lax.fori_loop(0, n_chunks, body, 0)
    sout_ref[0, 0, 0] = jnp.transpose(st_ref[...])


def gla_mix(p, w_alpha_pad, b_alpha, g_head, state0, state_layer, *, chunk, t_valid):
    bsz, t_total, _ = p.shape
    assert t_total % chunk == 0
    cmat, lmask = _gla_constants(chunk)
    n_e = cmat.shape[0]
    dk, dv = GLA_DK, GLA_DV
    kblk = lambda off: (lambda b, h: (b, 0, off + h))
    blocks = t_total * (3 * dk + 2 * dv) * 4 + t_total * dv * 2 + 2 * dk * dv * 4 + n_e * chunk * 2
    mix, s_out = pl.pallas_call(
        functools.partial(_gla_kernel, chunk=chunk, n_chunks=t_total // chunk, t_valid=t_valid, t_total=t_total),
        out_shape=(jax.ShapeDtypeStruct((bsz, t_total, GLA_HEADS * dv), BF16),
                   jax.ShapeDtypeStruct((bsz, 1, GLA_HEADS, dk, dv), F32)),
        grid=(bsz, GLA_HEADS),
        in_specs=[
            pl.BlockSpec((1, t_total, dk), kblk(0)),
            pl.BlockSpec((1, t_total, dk), kblk(GLA_HEADS)),
            pl.BlockSpec((1, t_total, dv), kblk(GLA_HEADS)),
            pl.BlockSpec((1, t_total, dv), kblk(2 * GLA_HEADS)),
            pl.BlockSpec((1, t_total, LANE), lambda b, h: (b, 0, GLA_A_COL // LANE)),
            pl.BlockSpec((LANE, dk), lambda b, h: (0, h)),
            pl.BlockSpec((1, dk), lambda b, h: (0, h)),
            pl.BlockSpec((1, dv), lambda b, h: (0, 0)),
            pl.BlockSpec((1, 1, 1, dk, dv), lambda b, h: (b, state_layer, h, 0, 0)),
            pl.BlockSpec((n_e, chunk), lambda b, h: (0, 0)),
            pl.BlockSpec(lmask.shape, lambda b, h: (0, 0, 0)),
        ],
        out_specs=(pl.BlockSpec((1, t_total, dv), lambda b, h: (b, 0, h)),
                   pl.BlockSpec((1, 1, 1, dk, dv), lambda b, h: (b, 0, h, 0, 0))),
        scratch_shapes=[pltpu.VMEM((dv, dk), F32)],
        compiler_params=_params(("parallel", "parallel"), _vmem_limit(blocks, dv * dk * 4, 8 << 20)),
        name="gla_mix",
    )(p, p, p, p, p, w_alpha_pad, b_alpha.reshape(1, -1), g_head.reshape(1, -1), state0,
      jnp.asarray(cmat, BF16), jnp.asarray(lmask, F32))
    return mix, s_out


GLA_A_COL = 2 * GLA_HEADS * GLA_DK + 2 * GLA_HEADS * GLA_DV
GLA_QX_COL = GLA_A_COL + XA_WIDTH
GLA_COLS = GLA_QX_COL + XA_WIDTH
MOBA_QX_COL = 3 * TOK_WIDTH


def _bucket_thresholds(max_dist):
    n = np.arange(max_dist + 1)
    max_exact = REL_BUCKETS // 2
    nf = np.maximum(n, 1).astype(np.float32)
    large = max_exact + (np.log(nf / np.float32(max_exact)) / np.float32(math.log(REL_MAX_DIST / max_exact))
                         * np.float32(REL_BUCKETS - max_exact)).astype(np.int32)
    large = np.minimum(large, REL_BUCKETS - 1)
    bucket = np.where(n < max_exact, n, large)
    assert np.all(np.diff(bucket) >= 0) and bucket[-1] == REL_BUCKETS - 1
    return [int(np.argmax(bucket >= b)) for b in range(REL_BUCKETS)]


def _moba_prompt_kernel(rb_ref, q_ref, k_ref, v_ref, o_ref, km_ref, vt_ref, bias_ref, m_ref, l_ref, acc_ref,
                        *, nb, thr):
    h = pl.program_id(1)
    qi = pl.program_id(2)
    blk = MOBA_BLOCK
    n_sel = max(1, min(MOBA_TOPK, nb - 1))
    tk = lax.broadcasted_iota(jnp.int32, (blk, blk), 0)
    tq = lax.broadcasted_iota(jnp.int32, (blk, blk), 1)

    @pl.when(qi == 0)
    def _():
        for n in range(nb):
            km_ref[n:n + 1, :] = jnp.mean(k_ref[0, n * blk:(n + 1) * blk, :], axis=0, keepdims=True)
            vt_ref[:, n * blk:(n + 1) * blk] = jnp.transpose(v_ref[0, n * blk:(n + 1) * blk, :]).astype(BF16)
        for which in range(2):
            d = tq - tk + which * blk
            bias = jnp.full((blk, blk), rb_ref[h, 0], F32)
            for bkt in range(1, REL_BUCKETS):
                bias = jnp.where(d >= thr[bkt], rb_ref[h, bkt], bias)
            bias_ref[which] = bias

    q = q_ref[0]
    gate = _nt_hi(km_ref[...], q)
    gs = [gate[n:n + 1, :] for n in range(nb)]
    sels = []
    for n in range(nb):
        rank = jnp.zeros((1, blk), jnp.int32)
        for m in range(nb):
            if m != n:
                beats = (gs[m] >= gs[n]) if m < n else (gs[m] > gs[n])
                rank = rank + jnp.where(beats, 1, 0) * (m < qi).astype(jnp.int32)
        sels.append(jnp.where(rank < n_sel, 1, 0) * (n < qi).astype(jnp.int32))
    causal = jnp.where(tq >= tk, 1, 0)

    m_ref[...] = jnp.full(m_ref.shape, NEG, F32)
    l_ref[...] = jnp.zeros(l_ref.shape, F32)
    acc_ref[...] = jnp.zeros(acc_ref.shape, F32)
    qb = q.astype(BF16)
    scale = HEAD_DIM ** -0.5
    for j in range(nb):
        @pl.when(j <= qi)
        def _(j=j):
            s = _nt(k_ref[0, j * blk:(j + 1) * blk, :].astype(BF16), qb) * scale
            dj = qi - j
            bias = jnp.where(dj == 0, bias_ref[0], jnp.where(dj == 1, bias_ref[1], rb_ref[h, REL_BUCKETS - 1]))
            valid = jnp.where(dj == 0, causal, jnp.broadcast_to(sels[j], (blk, blk)))
            s = jnp.where(valid > 0, s + bias, NEG)
            m_old = m_ref[...]
            m_new = jnp.maximum(m_old, jnp.max(s, axis=0, keepdims=True))
            alpha = jnp.exp(m_old - m_new)
            p = jnp.exp(s - m_new)
            l_ref[...] = alpha * l_ref[...] + jnp.sum(p, axis=0, keepdims=True)
            acc_ref[...] = alpha * acc_ref[...] + jnp.dot(vt_ref[:, j * blk:(j + 1) * blk], p.astype(BF16),
                                                          preferred_element_type=F32)
            m_ref[...] = m_new

    o_ref[0] = jnp.transpose(acc_ref[...] / l_ref[...]).astype(BF16)


def moba_prompt_mix(p, rel_bias):
    bsz, t_total, _ = p.shape
    blk = MOBA_BLOCK
    assert t_total % blk == 0
    nb = t_total // blk
    thr = _bucket_thresholds(2 * blk)
    assert thr[REL_BUCKETS - 1] <= blk + 1
    hd = HEAD_DIM
    blocks = blk * hd * 4 + 2 * t_total * hd * 4 + blk * hd * 2
    scratch = SUBLANE * hd * 4 + hd * t_total * 2 + 2 * blk * blk * 4 + 2 * blk * 4 + hd * blk * 4
    return pl.pallas_call(
        functools.partial(_moba_prompt_kernel, nb=nb, thr=thr),
        out_shape=jax.ShapeDtypeStruct((bsz, t_total, MOBA_HEADS * hd), BF16),
        grid=(bsz, MOBA_HEADS, nb),
        in_specs=[
            pl.BlockSpec(memory_space=pltpu.SMEM),
            pl.BlockSpec((1, blk, hd), lambda b, h, i: (b, i, h)),
            pl.BlockSpec((1, t_total, hd), lambda b, h, i: (b, 0, MOBA_HEADS + h)),
            pl.BlockSpec((1, t_total, hd), lambda b, h, i: (b, 0, 2 * MOBA_HEADS + h)),
        ],
        out_specs=pl.BlockSpec((1, blk, hd), lambda b, h, i: (b, i, h)),
        scratch_shapes=[
            pltpu.VMEM((nb, hd), F32),
            pltpu.VMEM((hd, t_total), BF16),
            pltpu.VMEM((2, blk, blk), F32),
            pltpu.VMEM((1, blk), F32),
            pltpu.VMEM((1, blk), F32),
            pltpu.VMEM((hd, blk), F32),
        ],
        compiler_params=_params(("parallel", "parallel", "arbitrary"), _vmem_limit(blocks, scratch, 8 << 20)),
        name="moba_prompt",
    )(jnp.transpose(rel_bias), p, p, p)


def _moba_sample_kernel(pt_ref, q_ref, kn_ref, vn_ref, ck_ref, cv_ref, rbx_ref, o_ref,
                        qbd_ref, ksum_ref, ma_ref, la_ref, mb_ref, lb_ref, oall_ref, b63_ref, bown_ref,
                        kown_ref, vown_ref, *, n_pages, page, thr, past_len):
    del pt_ref
    pg = pl.program_id(1)
    hd = HEAD_DIM
    nrow = LANE
    width = MOBA_HEADS * hd
    n_blocks = n_pages * page // MOBA_BLOCK
    scale = hd ** -0.5
    row = lax.broadcasted_iota(jnp.int32, (nrow, LANE), 0)
    lane = lax.broadcasted_iota(jnp.int32, (nrow, LANE), 1)
    t_row = row % SAMPLE_ROWS

    @pl.when(pg == 0)
    def _():
        q8 = q_ref[0]
        rowg = lax.broadcasted_iota(jnp.int32, (nrow, width), 0) // SAMPLE_ROWS
        colg = lax.broadcasted_iota(jnp.int32, (nrow, width), 1) // hd
        qbd_ref[...] = jnp.where(rowg == colg, jnp.concatenate([q8] * (nrow // SAMPLE_ROWS), axis=0), 0.0)
        ksum_ref[...] = jnp.zeros(ksum_ref.shape, F32)
        ma_ref[...] = jnp.full(ma_ref.shape, NEG, F32)
        mb_ref[...] = jnp.full(mb_ref.shape, NEG, F32)
        la_ref[...] = jnp.zeros(la_ref.shape, F32)
        lb_ref[...] = jnp.zeros(lb_ref.shape, F32)
        d63 = page + t_row - lane
        bias = jnp.broadcast_to(rbx_ref[:, 0:1], (nrow, LANE))
        for bkt in range(1, REL_BUCKETS):
            bias = jnp.where(d63 >= thr[bkt], rbx_ref[:, bkt:bkt + 1], bias)
        b63_ref[...] = bias
        down = t_row - lane
        bias = jnp.zeros((nrow, LANE), F32)
        for dd in range(SAMPLE_ROWS):
            bias = jnp.where(down == dd, rbx_ref[:, dd:dd + 1], bias)
        bown_ref[...] = bias
        kown_ref[...] = jnp.zeros(kown_ref.shape, F32)
        vown_ref[...] = jnp.zeros(vown_ref.shape, F32)
        kown_ref[0:SAMPLE_ROWS, :] = kn_ref[0]
        vown_ref[0:SAMPLE_ROWS, :] = vn_ref[0]

    qbd = qbd_ref[...].astype(BF16)
    rowg128 = lax.broadcasted_iota(jnp.int32, (nrow, hd), 0) // SAMPLE_ROWS

    def partial_softmax(kb, vb, bias, valid):
        s = _nt(qbd, kb) * scale + bias
        if valid is not None:
            s = jnp.where(valid, s, NEG)
        m = jnp.max(s, axis=-1, keepdims=True)
        p = jnp.exp(s - m)
        if valid is not None:
            p = jnp.where(valid, p, 0.0)
        l = jnp.sum(p, axis=-1, keepdims=True)
        o_full = jnp.dot(p.astype(BF16), vb, preferred_element_type=F32)
        o = jnp.zeros((nrow, hd), F32)
        for hh in range(MOBA_HEADS):
            o = o + jnp.where(rowg128 == hh, o_full[:, hh * hd:(hh + 1) * hd], 0.0)
        return m, l, o

    kp = ck_ref[0]
    blk_idx = pg // 2
    ksum_ref[pl.ds(blk_idx, 1), :] = ksum_ref[pl.ds(blk_idx, 1), :] + jnp.sum(kp, axis=0, keepdims=True)
    bias_pg = jnp.where(pg == n_pages - 1, b63_ref[...], jnp.broadcast_to(rbx_ref[:, REL_BUCKETS - 1:REL_BUCKETS],
                                                                       (nrow, LANE)))
    m, l, o = partial_softmax(kp.astype(BF16), cv_ref[0].astype(BF16), bias_pg, None)
    oall_ref[pg] = o
    hit_even = lane == jnp.where(pg % 2 == 0, blk_idx, -1)
    hit_odd = lane == jnp.where(pg % 2 == 1, blk_idx, -1)
    ma_ref[...] = jnp.where(hit_even, m, ma_ref[...])
    la_ref[...] = jnp.where(hit_even, l, la_ref[...])
    mb_ref[...] = jnp.where(hit_odd, m, mb_ref[...])
    lb_ref[...] = jnp.where(hit_odd, l, lb_ref[...])

    @pl.when(pg == n_pages - 1)
    def _():
        valid_own = jnp.logical_and(lane <= t_row, lane < SAMPLE_ROWS)
        m_own, l_own, o_own = partial_softmax(kown_ref[...].astype(BF16), vown_ref[...].astype(BF16),
                                              bown_ref[...], valid_own)
        kmean = ksum_ref[...] * (1.0 / MOBA_BLOCK)
        gate = _nt_hi(qbd_ref[...], kmean)
        g = jnp.where(lane < n_blocks, gate, -jnp.inf)
        sel_i = jnp.zeros((nrow, LANE), jnp.int32)
        for _ in range(max(1, min(MOBA_TOPK, n_blocks))):
            gmax = jnp.max(g, axis=-1, keepdims=True)
            first = jnp.min(jnp.where(g == gmax, lane, LANE), axis=-1, keepdims=True)
            pick = lane == first
            sel_i = jnp.where(pick, 1, sel_i)
            g = jnp.where(pick, -jnp.inf, g)
        sel = sel_i > 0
        ma = jnp.where(sel, ma_ref[...], NEG)
        mb = jnp.where(sel, mb_ref[...], NEG)
        m_all = jnp.maximum(jnp.maximum(jnp.max(ma, axis=-1, keepdims=True), jnp.max(mb, axis=-1, keepdims=True)),
                            m_own)
        wa = jnp.where(sel, jnp.exp(ma - m_all), 0.0)
        wb = jnp.where(sel, jnp.exp(mb - m_all), 0.0)
        w_own = jnp.exp(m_own - m_all)
        den = jnp.sum(wa * la_ref[...] + wb * lb_ref[...], axis=-1, keepdims=True) + w_own * l_own
        num = w_own * o_own
        for n in range(n_blocks):
            num = num + wa[:, n:n + 1] * oall_ref[2 * n] + wb[:, n:n + 1] * oall_ref[2 * n + 1]
        out = num / den
        o_ref[0] = jnp.concatenate([out[hh * SAMPLE_ROWS:(hh + 1) * SAMPLE_ROWS, :] for hh in range(MOBA_HEADS)],
                                   axis=1).astype(BF16)


def moba_sample_mix(p, cache_k, cache_v, page_table, layer, rel_bias):
    bs, rows, _ = p.shape
    n_pool, n_layers, page, heads, hd = cache_k.shape
    n_pages = page_table.shape[1]
    past_len = n_pages * page
    assert rows == SAMPLE_ROWS and heads == MOBA_HEADS and hd == HEAD_DIM and MOBA_BLOCK == 2 * page
    assert past_len % MOBA_BLOCK == 0 and MOBA_HEADS * SAMPLE_ROWS <= LANE and past_len // MOBA_BLOCK <= LANE
    width = heads * hd
    thr = _bucket_thresholds(2 * page + SAMPLE_ROWS)
    assert thr[REL_BUCKETS - 1] <= page + 1
    ck = cache_k.reshape(n_pool * n_layers, page, width)
    cv = cache_v.reshape(n_pool * n_layers, page, width)
    rbx = jnp.zeros((LANE, LANE), F32).at[:heads * SAMPLE_ROWS, :REL_BUCKETS].set(
        jnp.repeat(jnp.transpose(rel_bias), SAMPLE_ROWS, axis=0))
    pmap = lambda b, g, pt: (pt[b, g] * n_layers + layer, 0, 0)
    blocks = 3 * rows * width * 4 + 2 * page * width * 4 + LANE * LANE * 4 + rows * width * 2
    scratch = (LANE * width * 4 * 4 + 6 * LANE * LANE * 4 + n_pages * LANE * hd * 4)
    return pl.pallas_call(
        functools.partial(_moba_sample_kernel, n_pages=n_pages, page=page, thr=thr, past_len=past_len),
        out_shape=jax.ShapeDtypeStruct((bs, rows, width), BF16),
        grid_spec=pltpu.PrefetchScalarGridSpec(
            num_scalar_prefetch=1,
            grid=(bs, n_pages),
            in_specs=[
                pl.BlockSpec((1, rows, width), lambda b, g, pt: (b, 0, 0)),
                pl.BlockSpec((1, rows, width), lambda b, g, pt: (b, 0, 1)),
                pl.BlockSpec((1, rows, width), lambda b, g, pt: (b, 0, 2)),
                pl.BlockSpec((1, page, width), pmap),
                pl.BlockSpec((1, page, width), pmap),
                pl.BlockSpec((LANE, LANE), lambda b, g, pt: (0, 0)),
            ],
            out_specs=pl.BlockSpec((1, rows, width), lambda b, g, pt: (b, 0, 0)),
            scratch_shapes=[
                pltpu.VMEM((LANE, width), F32),
                pltpu.VMEM((LANE, width), F32),
                pltpu.VMEM((LANE, LANE), F32), pltpu.VMEM((LANE, LANE), F32),
                pltpu.VMEM((LANE, LANE), F32), pltpu.VMEM((LANE, LANE), F32),
                pltpu.VMEM((n_pages, LANE, hd), F32),
                pltpu.VMEM((LANE, LANE), F32), pltpu.VMEM((LANE, LANE), F32),
                pltpu.VMEM((LANE, width), F32), pltpu.VMEM((LANE, width), F32),
            ],
        ),
        compiler_params=_params(("parallel", "arbitrary"), _vmem_limit(blocks, scratch, 8 << 20)),
        name="moba_sample",
    )(page_table, p, p, p, ck, cv, rbx)


def _xattn_kernel(q_ref, mk_ref, mv_ref, o_ref):
    scale = HEAD_DIM ** -0.5
    for h in range(XA_HEADS):
        sl = slice(h * HEAD_DIM, (h + 1) * HEAD_DIM)
        s = _nt(q_ref[0, :, sl].astype(BF16), mk_ref[0, :, sl].astype(BF16)) * scale
        p = jnp.exp(s - jnp.max(s, axis=-1, keepdims=True))
        o = jnp.dot(p.astype(BF16), mv_ref[0, :, sl].astype(BF16), preferred_element_type=F32)
        o_ref[0, :, sl] = (o / jnp.sum(p, axis=-1, keepdims=True)).astype(BF16)


def cross_attend(p, qx_col, mk, mv, mk_idx, mv_idx, *, tq):
    bsz, t_total, _ = p.shape
    assert t_total % tq == 0 and qx_col % XA_WIDTH == 0
    blocks = tq * XA_WIDTH * 4 + 2 * N_MEM * XA_WIDTH * 4 + tq * XA_WIDTH * 2
    return pl.pallas_call(
        _xattn_kernel,
        out_shape=jax.ShapeDtypeStruct((bsz, t_total, XA_WIDTH), BF16),
        grid=(bsz, t_total // tq),
        in_specs=[
            pl.BlockSpec((1, tq, XA_WIDTH), lambda b, i: (b, i, qx_col // XA_WIDTH)),
            pl.BlockSpec((1, N_MEM, XA_WIDTH), lambda b, i: mk_idx(b)),
            pl.BlockSpec((1, N_MEM, XA_WIDTH), lambda b, i: mv_idx(b)),
        ],
        out_specs=pl.BlockSpec((1, tq, XA_WIDTH), lambda b, i: (b, i, 0)),
        compiler_params=_params(("parallel", "parallel"), _vmem_limit(blocks, 0, 8 * tq * N_MEM * 4)),
        name="cross_attend",
    )(p, mk, mv)


def _out_proj_kernel(mix_ref, xa_ref, w1_ref, w2_ref, res_ref, o_ref):
    o_ref[...] = (res_ref[...] + jnp.dot(mix_ref[...], w1_ref[...], preferred_element_type=F32)
                  + jnp.dot(xa_ref[...], w2_ref[...], preferred_element_type=F32))


def out_proj(mix, xa, w, res, *, tm, tn):
    m, d = res.shape
    assert m % tm == 0 and d % tn == 0 and TOK_WIDTH % XA_WIDTH == 0
    blocks = tm * D_MODEL * 2 + D_MODEL * tn * 2 + 2 * tm * tn * 4
    return pl.pallas_call(
        _out_proj_kernel,
        out_shape=jax.ShapeDtypeStruct((m, d), F32),
        grid=(m // tm, d // tn),
        in_specs=[
            pl.BlockSpec((tm, TOK_WIDTH), lambda i, j: (i, 0)),
            pl.BlockSpec((tm, XA_WIDTH), lambda i, j: (i, 0)),
            pl.BlockSpec((TOK_WIDTH, tn), lambda i, j: (0, j)),
            pl.BlockSpec((XA_WIDTH, tn), lambda i, j: (TOK_WIDTH // XA_WIDTH, j)),
            pl.BlockSpec((tm, tn), lambda i, j: (i, j)),
        ],
        out_specs=pl.BlockSpec((tm, tn), lambda i, j: (i, j)),
        compiler_params=_params(("parallel", "parallel"), _vmem_limit(blocks, 0, tm * tn * 4)),
        name="out_proj",
    )(mix, xa, w, w, res)


def _top_values(s, count):
    vals = []
    for _ in range(count):
        m = jnp.max(s, axis=0, keepdims=True)
        vals.append(m)
        s = jnp.where(s == m, -jnp.inf, s)
    return vals


def _peer_route_kernel(q_ref, sk_ref, e0_ref, cnt_ref, e1_ref, rnk_ref):
    kk = PEER_TOPK
    half = PEER_QDIM // 2
    for h in range(PEER_HEADS):
        s0 = _nt(sk_ref[2 * h], q_ref[:, (2 * h) * half:(2 * h + 1) * half].astype(BF16))
        s1 = _nt(sk_ref[2 * h + 1], q_ref[:, (2 * h + 1) * half:(2 * h + 2) * half].astype(BF16))
        top0 = _top_values(s0, kk)
        top1 = _top_values(s1, kk)
        cands = [top0[a] + top1[b] for a in range(kk) for b in range(kk) if (a + 1) * (b + 1) <= kk]
        n_pad = -len(cands) % SUBLANE
        cmat = jnp.concatenate(cands + [jnp.full_like(cands[0], -jnp.inf)] * n_pad, axis=0)
        rest = cmat
        for _ in range(kk):
            tau = jnp.max(rest, axis=0, keepdims=True)
            rest = jnp.where(rest == tau, -jnp.inf, rest)
        m_tot = top0[0] + top1[0]
        z = jnp.sum(jnp.where(cmat >= tau, jnp.exp(cmat - m_tot), 0.0), axis=0, keepdims=True)
        cnt = jnp.zeros(s0.shape, F32)
        rnk = jnp.zeros(s1.shape, F32)
        for b in range(kk):
            cnt = cnt + jnp.where(s0 + top1[b] >= tau, 1.0, 0.0)
            rnk = rnk + jnp.where(top1[b] > s1, 1.0, 0.0)
        e0_ref[h] = jnp.exp(s0 - top0[0]) / z
        cnt_ref[h] = cnt
        e1_ref[h] = jnp.exp(s1 - top1[0])
        rnk_ref[h] = rnk


def peer_route(q, subkeys, *, tb):
    m, _ = q.shape
    assert m % tb == 0
    sk = subkeys.reshape(PEER_HEADS * 2, PEER_NKEYS, PEER_QDIM // 2).astype(BF16)
    shp = jax.ShapeDtypeStruct((PEER_HEADS, PEER_NKEYS, m), F32)
    ospec = pl.BlockSpec((PEER_HEADS, PEER_NKEYS, tb), lambda i: (0, 0, i))
    blocks = tb * PEER_HEADS * PEER_QDIM * 4 + sk.size * 2 + 4 * PEER_HEADS * PEER_NKEYS * tb * 4
    return pl.pallas_call(
        _peer_route_kernel,
        out_shape=(shp, shp, shp, shp),
        grid=(m // tb,),
        in_specs=[pl.BlockSpec((tb, PEER_HEADS * PEER_QDIM), lambda i: (i, 0)),
                  pl.BlockSpec(sk.shape, lambda i: (0, 0, 0))],
        out_specs=(ospec, ospec, ospec, ospec),
        compiler_params=_params(("parallel",), _vmem_limit(blocks, 0, 16 * PEER_NKEYS * tb * 4)),
        name="peer_route",
    )(q, sk)


def _peer_expert_kernel(xn_ref, u_ref, v_ref, e0_ref, cnt_ref, e1_ref, rnk_ref, res_ref, o_ref, *, n_i):
    et = pl.program_id(1)

    @pl.when(et == 0)
    def _():
        o_ref[...] = res_ref[...]

    h_t = _nt(u_ref[...].astype(BF16), xn_ref[...])
    act = 0.5 * h_t * (1.0 + lax.erf(h_t * (1.0 / math.sqrt(2.0))))
    w_rows = []
    for ii in range(n_i):
        i_row = et * n_i + ii
        g = jnp.zeros((PEER_NKEYS, h_t.shape[1]), F32)
        for h in range(PEER_HEADS):
            cnt_i = cnt_ref[h, pl.ds(i_row, 1), :]
            e0_i = e0_ref[h, pl.ds(i_row, 1), :]
            g = g + jnp.where(rnk_ref[h] < cnt_i, e1_ref[h], 0.0) * e0_i
        w_rows.append((g * act[ii * PEER_NKEYS:(ii + 1) * PEER_NKEYS]).astype(BF16))
    w_t = jnp.concatenate(w_rows, axis=0)
    o_ref[...] += _tn(w_t, v_ref[...].astype(BF16))


def peer_experts(xn, u, v, route, res, *, tb, te):
    m, d = xn.shape
    n_exp = u.shape[0]
    assert m % tb == 0 and n_exp % te == 0 and te % PEER_NKEYS == 0 and n_exp == PEER_NKEYS * PEER_NKEYS
    rspec = pl.BlockSpec((PEER_HEADS, PEER_NKEYS, tb), lambda i, e: (0, 0, i))
    blocks = tb * d * 2 + 2 * te * d * 4 + 4 * PEER_HEADS * PEER_NKEYS * tb * 4 + 2 * tb * d * 4
    return pl.pallas_call(
        functools.partial(_peer_expert_kernel, n_i=te // PEER_NKEYS),
        out_shape=jax.ShapeDtypeStruct((m, d), F32),
        grid=(m // tb, n_exp // te),
        in_specs=[
            pl.BlockSpec((tb, d), lambda i, e: (i, 0)),
            pl.BlockSpec((te, d), lambda i, e: (e, 0)),
            pl.BlockSpec((te, d), lambda i, e: (e, 0)),
            rspec, rspec, rspec, rspec,
            pl.BlockSpec((tb, d), lambda i, e: (i, 0)),
        ],
        out_specs=pl.BlockSpec((tb, d), lambda i, e: (i, 0)),
        compiler_params=_params(("parallel", "arbitrary"), _vmem_limit(blocks, 0, 8 * te * tb * 4)),
        name="peer_experts",
    )(xn, u, v, *route, res)


def peer_layer(x, g, w_q, subkeys, u, v, *, tm, tb_route, tb_exp, te):
    q, xn = norm_matmul(x, g, w_q, tm=tm, tn=512, emit_xn=True)
    route = peer_route(q, subkeys, tb=tb_route)
    return peer_experts(xn, u, v, route, x, tb=tb_exp, te=te)


def _pad_gla_weight(w):
    d = w.shape[0]
    n_tok = GLA_A_COL + GLA_RANK
    return jnp.concatenate([w[:, :n_tok], jnp.zeros((d, GLA_QX_COL - n_tok), w.dtype), w[:, n_tok:]],
                           axis=1).astype(BF16)


def kernel(x_prompt, x_sample, state_gla, cache_moba_k, cache_moba_v, cache_mem_k, cache_mem_v, page_table, mem_prompt, ln_mix, ln_mem, ln_ffn, ln_final, w_in_gla, w_alpha_gla, b_alpha_gla, g_head_gla, w_in_moba, rel_bias, w_mem_kv, w_out, w_peer_q, peer_subkeys, peer_u, peer_v):
    bsz, seq, d = x_prompt.shape
    dec_b, dec_t, _ = x_sample.shape
    assert dec_t <= SAMPLE_ROWS
    xp = x_prompt.reshape(bsz * seq, d)
    xs = jnp.pad(x_sample, ((0, 0), (0, SAMPLE_ROWS - dec_t), (0, 0))).reshape(dec_b * SAMPLE_ROWS, d)
    mem2d = mem_prompt.reshape(bsz * N_MEM, d)
    n_dec = dec_b * SAMPLE_ROWS
    cmk = cache_mem_k.reshape(dec_b, DEPTH * N_MEM, XA_WIDTH)
    cmv = cache_mem_v.reshape(dec_b, DEPTH * N_MEM, XA_WIDTH)

    gla_p, gla_s, kp_l, vp_l, ks_l, vs_l, mkp_l, mvp_l = [], [], [], [], [], [], [], []
    for i in range(DEPTH):
        j = i // N_MIXERS
        kv = norm_matmul(mem2d, ln_mem[i], w_mem_kv[i].astype(BF16), tm=512, tn=512).reshape(bsz, N_MEM, 2 * XA_WIDTH)
        mkp_l.append(kv[..., :XA_WIDTH].reshape(bsz, N_MEM, XA_HEADS, HEAD_DIM))
        mvp_l.append(kv[..., XA_WIDTH:].reshape(bsz, N_MEM, XA_HEADS, HEAD_DIM))
        if i % N_MIXERS == 0:
            w_in = _pad_gla_weight(w_in_gla[j])
            wa = jnp.zeros((LANE, GLA_HEADS * GLA_DK), F32).at[:GLA_RANK].set(w_alpha_gla[j]).astype(BF16)
            pp = norm_matmul(xp, ln_mix[i], w_in, tm=512, tn=512).reshape(bsz, seq, GLA_COLS)
            ps = norm_matmul(xs, ln_mix[i], w_in, tm=n_dec, tn=512).reshape(dec_b, SAMPLE_ROWS, GLA_COLS)
            zeros0 = jnp.zeros((bsz, 1, GLA_HEADS, GLA_DK, GLA_DV), F32)
            mix_p, s_p = gla_mix(pp, wa, b_alpha_gla[j], g_head_gla[j], zeros0, 0, chunk=GLA_CHUNK, t_valid=seq)
            ps_pad = jnp.pad(ps, ((0, 0), (0, GLA_SAMPLE_CHUNK - SAMPLE_ROWS), (0, 0)))
            mix_s, s_s = gla_mix(ps_pad, wa, b_alpha_gla[j], g_head_gla[j], state_gla, j,
                                 chunk=GLA_SAMPLE_CHUNK, t_valid=dec_t)
            mix_s = mix_s[:, :SAMPLE_ROWS]
            gla_p.append(s_p[:, 0])
            gla_s.append(s_s[:, 0])
            qx_col = GLA_QX_COL
        else:
            w_in = w_in_moba[j].astype(BF16)
            pp = norm_matmul(xp, ln_mix[i], w_in, tm=512, tn=512).reshape(bsz, seq, -1)
            ps = norm_matmul(xs, ln_mix[i], w_in, tm=n_dec, tn=512).reshape(dec_b, SAMPLE_ROWS, -1)
            mix_p = moba_prompt_mix(pp, rel_bias)
            mix_s = moba_sample_mix(ps, cache_moba_k, cache_moba_v, page_table, j, rel_bias)
            sh = lambda t, n: t.reshape(t.shape[0], n, MOBA_HEADS, HEAD_DIM)
            kp_l.append(sh(pp[..., TOK_WIDTH:2 * TOK_WIDTH], seq))
            vp_l.append(sh(pp[..., 2 * TOK_WIDTH:3 * TOK_WIDTH], seq))
            ks_l.append(sh(ps[:, :dec_t, TOK_WIDTH:2 * TOK_WIDTH], dec_t))
            vs_l.append(sh(ps[:, :dec_t, 2 * TOK_WIDTH:3 * TOK_WIDTH], dec_t))
            qx_col = MOBA_QX_COL
        xa_p = cross_attend(pp, qx_col, kv, kv, lambda b: (b, 0, 0), lambda b: (b, 0, 1), tq=512)
        xa_s = cross_attend(ps, qx_col, cmk, cmv, lambda b, i=i: (b, i, 0), lambda b, i=i: (b, i, 0), tq=SAMPLE_ROWS)
        w_o = w_out[i].astype(BF16)
        xp = out_proj(mix_p.reshape(bsz * seq, TOK_WIDTH), xa_p.reshape(bsz * seq, XA_WIDTH), w_o, xp, tm=512, tn=1024)
        xs = out_proj(mix_s.reshape(n_dec, TOK_WIDTH), xa_s.reshape(n_dec, XA_WIDTH), w_o, xs, tm=n_dec, tn=1024)
        w_q = w_peer_q[i].astype(BF16)
        xp = peer_layer(xp, ln_ffn[i], w_q, peer_subkeys[i], peer_u[i], peer_v[i], tm=512, tb_route=256,
                        tb_exp=512, te=256)
        xs = peer_layer(xs, ln_ffn[i], w_q, peer_subkeys[i], peer_u[i], peer_v[i], tm=n_dec, tb_route=n_dec,
                        tb_exp=n_dec, te=256)
    y_prompt = rmsnorm_rows(xp, ln_final, tm=512).reshape(bsz, seq, d)
    y_sample = rmsnorm_rows(xs, ln_final, tm=n_dec).reshape(dec_b, SAMPLE_ROWS, d)[:, :dec_t]
    return (y_prompt, y_sample,
            jnp.stack(gla_p, axis=1), jnp.stack(gla_s, axis=1),
            jnp.stack(kp_l, axis=1), jnp.stack(vp_l, axis=1),
            jnp.stack(ks_l, axis=1), jnp.stack(vs_l, axis=1),
            jnp.stack(mkp_l, axis=1), jnp.stack(mvp_l, axis=1))
```

```python
import functools
import math

import numpy as np
import jax
import jax.numpy as jnp
from jax import lax
from jax.experimental import pallas as pl
from jax.experimental.pallas import tpu as pltpu

D_MODEL = 2048
DEPTH = 2
N_MIXERS = 2
HEAD_DIM = 128
N_MEM = 256
XA_HEADS = 4
XA_WIDTH = XA_HEADS * HEAD_DIM
TOK_WIDTH = D_MODEL - XA_WIDTH
GLA_HEADS = 6
GLA_DV = TOK_WIDTH // GLA_HEADS
GLA_DK = GLA_DV // 2
GLA_RANK = 16
GLA_TAU = 16.0
GLA_CHUNK = 64
MOBA_HEADS = TOK_WIDTH // HEAD_DIM
MOBA_BLOCK = 256
MOBA_TOPK = 3
REL_BUCKETS = 32
REL_MAX_DIST = 128
PEER_HEADS = 8
PEER_NKEYS = 128
PEER_QDIM = 256
PEER_TOPK = 16
EPS = 1e-6

F32 = jnp.float32
BF16 = jnp.bfloat16

LANE = 128
SUBLANE = 8
VMEM_CAP_BYTES = 56 * 1024 * 1024

NEG = -1e30
_NT = (((1,), (1,)), ((), ()))
_TN = (((0,), (0,)), ((), ()))

GLA_SAMPLE_CHUNK = 16
SAMPLE_ROWS = 8
MOBA_SAMPLE_BLOCKS_PER_STEP = 2


def _nt(a, b):
    return lax.dot_general(a, b, _NT, preferred_element_type=F32)


def _tn(a, b):
    return lax.dot_general(a, b, _TN, preferred_element_type=F32)


def _split3(x):
    x1 = x.astype(BF16)
    r1 = x - x1.astype(F32)
    x2 = r1.astype(BF16)
    x3 = (r1 - x2.astype(F32)).astype(BF16)
    return x1, x2, x3


def _nt_hi(a, b):
    a1 = a.astype(BF16)
    a2 = (a - a1.astype(F32)).astype(BF16)
    b1 = b.astype(BF16)
    b2 = (b - b1.astype(F32)).astype(BF16)
    return _nt(a1, b1) + _nt(a1, b2) + _nt(a2, b1)


def _vmem_limit(block_bytes, scratch_bytes=0, temp_bytes=0):
    return int(min(VMEM_CAP_BYTES, 2 * block_bytes + scratch_bytes + temp_bytes + (4 << 20)))


def _params(sem, vmem):
    return pltpu.CompilerParams(dimension_semantics=sem, vmem_limit_bytes=vmem)


def _norm_matmul_kernel(x_ref, g_ref, w_ref, o_ref, xn_ref):
    @pl.when(pl.program_id(1) == 0)
    def _():
        x = x_ref[...]
        y = x * lax.rsqrt(jnp.mean(x * x, axis=-1, keepdims=True) + EPS)
        xn_ref[...] = (y * g_ref[...]).astype(BF16)

    o_ref[...] = jnp.dot(xn_ref[...], w_ref[...], preferred_element_type=F32)


def norm_matmul(x, g, w, *, tm, tn, emit_xn=False):
    m, d = x.shape
    n = w.shape[1]
    assert m % tm == 0 and n % tn == 0 and w.shape[0] == d
    o_shape = jax.ShapeDtypeStruct((m, n), F32)
    o_spec = pl.BlockSpec((tm, tn), lambda i, j: (i, j))
    xn_bytes = tm * d * 2
    blocks = tm * d * 4 + d * 4 + d * tn * 2 + tm * tn * 4
    kwargs = dict(
        grid=(m // tm, n // tn),
        in_specs=[
            pl.BlockSpec((tm, d), lambda i, j: (i, 0)),
            pl.BlockSpec((1, d), lambda i, j: (0, 0)),
            pl.BlockSpec((d, tn), lambda i, j: (0, j)),
        ],
        name="norm_matmul",
    )
    if emit_xn:
        return pl.pallas_call(
            _norm_matmul_kernel,
            out_shape=(o_shape, jax.ShapeDtypeStruct((m, d), BF16)),
            out_specs=(o_spec, pl.BlockSpec((tm, d), lambda i, j: (i, 0))),
            compiler_params=_params(("parallel", "arbitrary"), _vmem_limit(blocks + xn_bytes, 0, tm * d * 8)),
            **kwargs,
        )(x, g.reshape(1, d), w)
    return pl.pallas_call(
        _norm_matmul_kernel,
        out_shape=o_shape,
        out_specs=o_spec,
        scratch_shapes=[pltpu.VMEM((tm, d), BF16)],
        compiler_params=_params(("parallel", "arbitrary"), _vmem_limit(blocks, xn_bytes, tm * d * 8)),
        **kwargs,
    )(x, g.reshape(1, d), w)


def _rmsnorm_kernel(x_ref, g_ref, o_ref):
    x = x_ref[...]
    o_ref[...] = x * lax.rsqrt(jnp.mean(x * x, axis=-1, keepdims=True) + EPS) * g_ref[...]


def rmsnorm_rows(x, g, *, tm):
    m, d = x.shape
    return pl.pallas_call(
        _rmsnorm_kernel,
        out_shape=jax.ShapeDtypeStruct((m, d), F32),
        grid=(m // tm,),
        in_specs=[pl.BlockSpec((tm, d), lambda i: (i, 0)), pl.BlockSpec((1, d), lambda i: (0, 0))],
        out_specs=pl.BlockSpec((tm, d), lambda i: (i, 0)),
        compiler_params=_params(("parallel",), _vmem_limit(2 * tm * d * 4, 0, tm * d * 4)),
        name="final_rmsnorm",
    )(x, g.reshape(1, d))


def _gla_constants(chunk):
    nlev = int(round(math.log2(chunk)))
    assert 1 << nlev == chunk
    idx = np.arange(chunk)
    col, row = idx[None, :], idx[:, None]
    mats = [col <= row, col > row]
    masks = []
    for lev in range(nlev):
        m = chunk >> (lev + 1)
        grp = idx // (2 * m)
        mid = grp * 2 * m + m
        up = (idx % (2 * m)) >= m
        mats.append((col >= mid[:, None]) & (col <= row) & up[:, None])
        mats.append((col > row) & (col < mid[:, None]) & (~up)[:, None])
        masks.append((grp[:, None] == grp[None, :]) & up[:, None] & (~up)[None, :])
    return (np.concatenate(mats, axis=0).astype(np.float32), np.stack(masks).astype(np.float32))


def _gla_kernel(q_ref, k_ref, v_ref, r_ref, a_ref, wa_ref, ba_ref, gh_ref, s0_ref, cmat_ref, lmask_ref,
                o_ref, sout_ref, st_ref, *, chunk, n_chunks, t_valid, t_total):
    c_rows = chunk
    nlev = int(round(math.log2(chunk)))
    st_ref[...] = jnp.transpose(s0_ref[0, 0, 0])

    def body(c, carry):
        r0 = pl.multiple_of(c * c_rows, c_rows)
        rows = pl.ds(r0, c_rows)
        q = q_ref[0, rows, :] * (GLA_DK ** -0.5)
        k = k_ref[0, rows, :]
        v = v_ref[0, rows, :]
        z = jnp.dot(a_ref[0, rows, :].astype(BF16), wa_ref[...], preferred_element_type=F32) + ba_ref[...]
        g = (jnp.minimum(z, 0.0) - jnp.log1p(jnp.exp(-jnp.abs(z)))) * (1.0 / GLA_TAU)
        if t_valid < t_total:
            t_idx = r0 + lax.broadcasted_iota(jnp.int32, (c_rows, 1), 0)
            g = jnp.where(t_idx < t_valid, g, 0.0)
        cm = cmat_ref[...]
        g1, g2, g3 = _split3(g)
        e = (jnp.dot(cm, g1, preferred_element_type=F32) + jnp.dot(cm, g2, preferred_element_type=F32)
             + jnp.dot(cm, g3, preferred_element_type=F32))
        qd = (q * jnp.exp(e[0:c_rows])).astype(BF16)
        kd = (k * jnp.exp(e[c_rows:2 * c_rows])).astype(BF16)
        att = jnp.zeros((c_rows, c_rows), F32)
        for lev in range(nlev):
            ea = e[(2 + 2 * lev) * c_rows:(3 + 2 * lev) * c_rows]
            eb = e[(3 + 2 * lev) * c_rows:(4 + 2 * lev) * c_rows]
            ql = (q * jnp.exp(ea)).astype(BF16)
            kl = (k * jnp.exp(eb)).astype(BF16)
            att = att + lmask_ref[lev] * _nt(ql, kl)
        vb = v.astype(BF16)
        st = st_ref[...]
        o = jnp.dot(att.astype(BF16), vb, preferred_element_type=F32)
        o = o + jnp.sum(q * k, axis=-1, keepdims=True) * v
        o = o + _nt(qd, st.astype(BF16))
        b_last = e[c_rows - 1:c_rows, :]
        st_ref[...] = st * jnp.exp(b_last) + _tn(vb, kd)
        on = o * lax.rsqrt(jnp.mean(o * o, axis=-1, keepdims=True) + EPS) * gh_ref[...]
        rr = r_ref[0, rows, :]
        o_ref[0, rows, :] = (on * (rr * jax.nn.sigmoid(rr))).astype(BF16)
        return carry

    lax.fori_loop(0, n_chunks, body, 0)
    sout_ref[0, 0, 0] = jnp.transpose(st_ref[...])


def gla_mix(p, w_alpha_pad, b_alpha, g_head, state0, state_layer, *, chunk, t_valid):
    bsz, t_total, _ = p.shape
    assert t_total % chunk == 0
    cmat, lmask = _gla_constants(chunk)
    n_e = cmat.shape[0]
    dk, dv = GLA_DK, GLA_DV
    kblk = lambda off: (lambda b, h: (b, 0, off + h))
    blocks = t_total * (3 * dk + 2 * dv) * 4 + t_total * dv * 2 + 2 * dk * dv * 4 + n_e * chunk * 2
    mix, s_out = pl.pallas_call(
        functools.partial(_gla_kernel, chunk=chunk, n_chunks=t_total // chunk, t_valid=t_valid, t_total=t_total),
        out_shape=(jax.ShapeDtypeStruct((bsz, t_total, GLA_HEADS * dv), BF16),
                   jax.ShapeDtypeStruct((bsz, 1, GLA_HEADS, dk, dv), F32)),
        grid=(bsz, GLA_HEADS),
        in_specs=[
            pl.BlockSpec((1, t_total, dk), kblk(0)),
            pl.BlockSpec((1, t_total, dk), kblk(GLA_HEADS)),
            pl.BlockSpec((1, t_total, dv), kblk(GLA_HEADS)),
            pl.BlockSpec((1, t_total, dv), kblk(2 * GLA_HEADS)),
            pl.BlockSpec((1, t_total, LANE), lambda b, h: (b, 0, GLA_A_COL // LANE)),
            pl.BlockSpec((LANE, dk), lambda b, h: (0, h)),
            pl.BlockSpec((1, dk), lambda b, h: (0, h)),
            pl.BlockSpec((1, dv), lambda b, h: (0, 0)),
            pl.BlockSpec((1, 1, 1, dk, dv), lambda b, h: (b, state_layer, h, 0, 0)),
            pl.BlockSpec((n_e, chunk), lambda b, h: (0, 0)),
            pl.BlockSpec(lmask.shape, lambda b, h: (0, 0, 0)),
        ],
        out_specs=(pl.BlockSpec((1, t_total, dv), lambda b, h: (b, 0, h)),
                   pl.BlockSpec((1, 1, 1, dk, dv), lambda b, h: (b, 0, h, 0, 0))),
        scratch_shapes=[pltpu.VMEM((dv, dk), F32)],
        compiler_params=_params(("parallel", "parallel"), _vmem_limit(blocks, dv * dk * 4, 8 << 20)),
        name="gla_mix",
    )(p, p, p, p, p, w_alpha_pad, b_alpha.reshape(1, -1), g_head.reshape(1, -1), state0,
      jnp.asarray(cmat, BF16), jnp.asarray(lmask, F32))
    return mix, s_out


GLA_A_COL = 2 * GLA_HEADS * GLA_DK + 2 * GLA_HEADS * GLA_DV
GLA_QX_COL = GLA_A_COL + XA_WIDTH
GLA_COLS = GLA_QX_COL + XA_WIDTH
MOBA_QX_COL = 3 * TOK_WIDTH


def _bucket_thresholds(max_dist):
    n = np.arange(max_dist + 1)
    max_exact = REL_BUCKETS // 2
    nf = np.maximum(n, 1).astype(np.float32)
    large = max_exact + (np.log(nf / np.float32(max_exact)) / np.float32(math.log(REL_MAX_DIST / max_exact))
                         * np.float32(REL_BUCKETS - max_exact)).astype(np.int32)
    large = np.minimum(large, REL_BUCKETS - 1)
    bucket = np.where(n < max_exact, n, large)
    assert np.all(np.diff(bucket) >= 0) and bucket[-1] == REL_BUCKETS - 1
    return [int(np.argmax(bucket >= b)) for b in range(REL_BUCKETS)]


def _moba_prompt_kernel(rb_ref, q_ref, k_ref, v_ref, o_ref, km_ref, vt_ref, bias_ref, m_ref, l_ref, acc_ref,
                        *, nb, thr):
    h = pl.program_id(1)
    qi = pl.program_id(2)
    blk = MOBA_BLOCK
    n_sel = max(1, min(MOBA_TOPK, nb - 1))
    tk = lax.broadcasted_iota(jnp.int32, (blk, blk), 0)
    tq = lax.broadcasted_iota(jnp.int32, (blk, blk), 1)

    @pl.when(qi == 0)
    def _():
        for n in range(nb):
            km_ref[n:n + 1, :] = jnp.mean(k_ref[0, n * blk:(n + 1) * blk, :], axis=0, keepdims=True)
            vt_ref[:, n * blk:(n + 1) * blk] = jnp.transpose(v_ref[0, n * blk:(n + 1) * blk, :]).astype(BF16)
        for which in range(2):
            d = tq - tk + which * blk
            bias = jnp.full((blk, blk), rb_ref[h, 0], F32)
            for bkt in range(1, REL_BUCKETS):
                bias = jnp.where(d >= thr[bkt], rb_ref[h, bkt], bias)
            bias_ref[which] = bias

    q = q_ref[0]
    gate = _nt_hi(km_ref[...], q)
    gs = [gate[n:n + 1, :] for n in range(nb)]
    sels = []
    for n in range(nb):
        rank = jnp.zeros((1, blk), jnp.int32)
        for m in range(nb):
            if m != n:
                beats = (gs[m] >= gs[n]) if m < n else (gs[m] > gs[n])
                rank = rank + jnp.where(beats, 1, 0) * (m < qi).astype(jnp.int32)
        sels.append(jnp.where(rank < n_sel, 1, 0) * (n < qi).astype(jnp.int32))
    causal = jnp.where(tq >= tk, 1, 0)

    m_ref[...] = jnp.full(m_ref.shape, NEG, F32)
    l_ref[...] = jnp.zeros(l_ref.shape, F32)
    acc_ref[...] = jnp.zeros(acc_ref.shape, F32)
    qb = q.astype(BF16)
    scale = HEAD_DIM ** -0.5
    for j in range(nb):
        @pl.when(j <= qi)
        def _(j=j):
            s = _nt(k_ref[0, j * blk:(j + 1) * blk, :].astype(BF16), qb) * scale
            dj = qi - j
            bias = jnp.where(dj == 0, bias_ref[0], jnp.where(dj == 1, bias_ref[1], rb_ref[h, REL_BUCKETS - 1]))
            valid = jnp.where(dj == 0, causal, jnp.broadcast_to(sels[j], (blk, blk)))
            s = jnp.where(valid > 0, s + bias, NEG)
            m_old = m_ref[...]
            m_new = jnp.maximum(m_old, jnp.max(s, axis=0, keepdims=True))
            alpha = jnp.exp(m_old - m_new)
            p = jnp.exp(s - m_new)
            l_ref[...] = alpha * l_ref[...] + jnp.sum(p, axis=0, keepdims=True)
            acc_ref[...] = alpha * acc_ref[...] + jnp.dot(vt_ref[:, j * blk:(j + 1) * blk], p.astype(BF16),
                                                          preferred_element_type=F32)
            m_ref[...] = m_new

    o_ref[0] = jnp.transpose(acc_ref[...] / l_ref[...]).astype(BF16)


def moba_prompt_mix(p, rel_bias):
    bsz, t_total, _ = p.shape
    blk = MOBA_BLOCK
    assert t_total % blk == 0
    nb = t_total // blk
    thr = _bucket_thresholds(2 * blk)
    assert thr[REL_BUCKETS - 1] <= blk + 1
    hd = HEAD_DIM
    blocks = blk * hd * 4 + 2 * t_total * hd * 4 + blk * hd * 2
    scratch = SUBLANE * hd * 4 + hd * t_total * 2 + 2 * blk * blk * 4 + 2 * blk * 4 + hd * blk * 4
    return pl.pallas_call(
        functools.partial(_moba_prompt_kernel, nb=nb, thr=thr),
        out_shape=jax.ShapeDtypeStruct((bsz, t_total, MOBA_HEADS * hd), BF16),
        grid=(bsz, MOBA_HEADS, nb),
        in_specs=[
            pl.BlockSpec(memory_space=pltpu.SMEM),
            pl.BlockSpec((1, blk, hd), lambda b, h, i: (b, i, h)),
            pl.BlockSpec((1, t_total, hd), lambda b, h, i: (b, 0, MOBA_HEADS + h)),
            pl.BlockSpec((1, t_total, hd), lambda b, h, i: (b, 0, 2 * MOBA_HEADS + h)),
        ],
        out_specs=pl.BlockSpec((1, blk, hd), lambda b, h, i: (b, i, h)),
        scratch_shapes=[
            pltpu.VMEM((nb, hd), F32),
            pltpu.VMEM((hd, t_total), BF16),
            pltpu.VMEM((2, blk, blk), F32),
            pltpu.VMEM((1, blk), F32),
            pltpu.VMEM((1, blk), F32),
            pltpu.VMEM((hd, blk), F32),
        ],
        compiler_params=_params(("parallel", "parallel", "arbitrary"), _vmem_limit(blocks, scratch, 8 << 20)),
        name="moba_prompt",
    )(jnp.transpose(rel_bias), p, p, p)


def _moba_sample_kernel(pt_ref, q_ref, kn_ref, vn_ref, *rest, n_blocks, bps):
    del pt_ref
    npg = 2 * bps
    ck_refs, cv_refs = rest[:npg], rest[npg:2 * npg]
    (rbx_ref, o_ref, qbd_ref, ksum_ref, ma_ref, la_ref, oall_ref, blast_ref, bown_ref, kown_ref,
     vown_ref) = rest[2 * npg:]
    step = pl.program_id(1)
    n_steps = n_blocks // bps
    hd = HEAD_DIM
    blk = MOBA_BLOCK
    nrow = LANE
    width = MOBA_HEADS * hd
    scale = hd ** -0.5
    row = lax.broadcasted_iota(jnp.int32, (nrow, LANE), 0)
    lane = lax.broadcasted_iota(jnp.int32, (nrow, LANE), 1)
    t_row = row % SAMPLE_ROWS
    thr = _bucket_thresholds(blk + SAMPLE_ROWS)
    assert thr[REL_BUCKETS - 1] <= blk + 1

    @pl.when(step == 0)
    def _():
        q8 = q_ref[0]
        rowg = lax.broadcasted_iota(jnp.int32, (nrow, width), 0) // SAMPLE_ROWS
        colg = lax.broadcasted_iota(jnp.int32, (nrow, width), 1) // hd
        qbd_ref[...] = jnp.where(rowg == colg, jnp.concatenate([q8] * (nrow // SAMPLE_ROWS), axis=0), 0.0)
        ksum_ref[...] = jnp.zeros(ksum_ref.shape, F32)
        ma_ref[...] = jnp.full(ma_ref.shape, NEG, F32)
        la_ref[...] = jnp.zeros(la_ref.shape, F32)
        dlast = (blk + lax.broadcasted_iota(jnp.int32, (nrow, blk), 0) % SAMPLE_ROWS
                 - lax.broadcasted_iota(jnp.int32, (nrow, blk), 1))
        bias = jnp.broadcast_to(rbx_ref[:, 0:1], (nrow, blk))
        for bkt in range(1, REL_BUCKETS):
            bias = jnp.where(dlast >= thr[bkt], rbx_ref[:, bkt:bkt + 1], bias)
        blast_ref[...] = bias
        down = t_row - lane
        bias = jnp.zeros((nrow, LANE), F32)
        for dd in range(SAMPLE_ROWS):
            bias = jnp.where(down == dd, rbx_ref[:, dd:dd + 1], bias)
        bown_ref[...] = bias
        kown_ref[...] = jnp.zeros(kown_ref.shape, F32)
        vown_ref[...] = jnp.zeros(vown_ref.shape, F32)
        kown_ref[0:SAMPLE_ROWS, :] = kn_ref[0]
        vown_ref[0:SAMPLE_ROWS, :] = vn_ref[0]

    qbd = qbd_ref[...].astype(BF16)
    rowg128 = lax.broadcasted_iota(jnp.int32, (nrow, hd), 0) // SAMPLE_ROWS

    def partial_softmax(kb, vb, bias, valid):
        s = _nt(qbd, kb) * scale + bias
        if valid is not None:
            s = jnp.where(valid, s, NEG)
        m = jnp.max(s, axis=-1, keepdims=True)
        p = jnp.exp(s - m)
        if valid is not None:
            p = jnp.where(valid, p, 0.0)
        l = jnp.sum(p, axis=-1, keepdims=True)
        o_full = jnp.dot(p.astype(BF16), vb, preferred_element_type=F32)
        o = jnp.zeros((nrow, hd), F32)
        for hh in range(MOBA_HEADS):
            o = o + jnp.where(rowg128 == hh, o_full[:, hh * hd:(hh + 1) * hd], 0.0)
        return m, l, o

    far_bias = rbx_ref[:, REL_BUCKETS - 1:REL_BUCKETS]
    ma_new = ma_ref[...]
    la_new = la_ref[...]
    for r in range(bps):
        b_idx = step * bps + r
        kb = jnp.concatenate([ck_refs[2 * r][0], ck_refs[2 * r + 1][0]], axis=0)
        vb = jnp.concatenate([cv_refs[2 * r][0], cv_refs[2 * r + 1][0]], axis=0)
        ksum_ref[pl.ds(b_idx, 1), :] = jnp.sum(kb, axis=0, keepdims=True)
        if r == bps - 1:
            bias = jnp.where(step == n_steps - 1, blast_ref[...], jnp.broadcast_to(far_bias, (nrow, blk)))
        else:
            bias = far_bias
        m, l, o = partial_softmax(kb.astype(BF16), vb.astype(BF16), bias, None)
        oall_ref[b_idx] = o
        ma_new = jnp.where(lane == b_idx, m, ma_new)
        la_new = jnp.where(lane == b_idx, l, la_new)
    ma_ref[...] = ma_new
    la_ref[...] = la_new

    @pl.when(step == n_steps - 1)
    def _():
        valid_own = jnp.logical_and(lane <= t_row, lane < SAMPLE_ROWS)
        m_own, l_own, o_own = partial_softmax(kown_ref[...].astype(BF16), vown_ref[...].astype(BF16),
                                              bown_ref[...], valid_own)
        kmean = ksum_ref[...] * (1.0 / MOBA_BLOCK)
        gate = _nt_hi(qbd_ref[...], kmean)
        g = jnp.where(lane < n_blocks, gate, -jnp.inf)
        sel_i = jnp.zeros((nrow, LANE), jnp.int32)
        for _ in range(max(1, min(MOBA_TOPK, n_blocks))):
            gmax = jnp.max(g, axis=-1, keepdims=True)
            first = jnp.min(jnp.where(g == gmax, lane, LANE), axis=-1, keepdims=True)
            pick = lane == first
            sel_i = jnp.where(pick, 1, sel_i)
            g = jnp.where(pick, -jnp.inf, g)
        sel = sel_i > 0
        ma = jnp.where(sel, ma_new, NEG)
        m_all = jnp.maximum(jnp.max(ma, axis=-1, keepdims=True), m_own)
        wa = jnp.where(sel, jnp.exp(ma - m_all), 0.0)
        w_own = jnp.exp(m_own - m_all)
        den = jnp.sum(wa * la_new, axis=-1, keepdims=True) + w_own * l_own
        num = w_own * o_own
        for n in range(n_blocks):
            num = num + wa[:, n:n + 1] * oall_ref[n]
        out = num / den
        o_ref[0] = jnp.concatenate([out[hh * SAMPLE_ROWS:(hh + 1) * SAMPLE_ROWS, :] for hh in range(MOBA_HEADS)],
                                   axis=1).astype(BF16)


def moba_sample_mix(p, cache_k, cache_v, page_table, layer, rel_bias):
    bs, rows, _ = p.shape
    n_pool, n_layers, page, heads, hd = cache_k.shape
    n_pages = page_table.shape[1]
    past_len = n_pages * page
    assert rows == SAMPLE_ROWS and heads == MOBA_HEADS and hd == HEAD_DIM and MOBA_BLOCK == 2 * page
    assert past_len % MOBA_BLOCK == 0 and MOBA_HEADS * SAMPLE_ROWS <= LANE and past_len // MOBA_BLOCK <= LANE
    n_blocks = past_len // MOBA_BLOCK
    bps = MOBA_SAMPLE_BLOCKS_PER_STEP if n_blocks % MOBA_SAMPLE_BLOCKS_PER_STEP == 0 else 1
    npg = 2 * bps
    width = heads * hd
    ck = cache_k.reshape(n_pool * n_layers, page, width)
    cv = cache_v.reshape(n_pool * n_layers, page, width)
    rbx = jnp.zeros((LANE, LANE), F32).at[:heads * SAMPLE_ROWS, :REL_BUCKETS].set(
        jnp.repeat(jnp.transpose(rel_bias), SAMPLE_ROWS, axis=0))

    def pmap(r):
        return lambda b, g, pt: (pt[b, g * npg + r] * n_layers + layer, 0, 0)

    page_specs = [pl.BlockSpec((1, page, width), pmap(r)) for r in range(npg)]
    blocks = 3 * rows * width * 4 + 2 * npg * page * width * 4 + LANE * LANE * 4 + rows * width * 2
    scratch = (LANE * width * 4 * 4 + 3 * LANE * LANE * 4 + LANE * MOBA_BLOCK * 4 + n_blocks * LANE * hd * 4)
    return pl.pallas_call(
        functools.partial(_moba_sample_kernel, n_blocks=n_blocks, bps=bps),
        out_shape=jax.ShapeDtypeStruct((bs, rows, width), BF16),
        grid_spec=pltpu.PrefetchScalarGridSpec(
            num_scalar_prefetch=1,
            grid=(bs, n_blocks // bps),
            in_specs=[
                pl.BlockSpec((1, rows, width), lambda b, g, pt: (b, 0, 0)),
                pl.BlockSpec((1, rows, width), lambda b, g, pt: (b, 0, 1)),
                pl.BlockSpec((1, rows, width), lambda b, g, pt: (b, 0, 2)),
                *page_specs, *page_specs,
                pl.BlockSpec((LANE, LANE), lambda b, g, pt: (0, 0)),
            ],
            out_specs=pl.BlockSpec((1, rows, width), lambda b, g, pt: (b, 0, 0)),
            scratch_shapes=[
                pltpu.VMEM((LANE, width), F32),
                pltpu.VMEM((LANE, width), F32),
                pltpu.VMEM((LANE, LANE), F32), pltpu.VMEM((LANE, LANE), F32),
                pltpu.VMEM((n_blocks, LANE, hd), F32),
                pltpu.VMEM((LANE, MOBA_BLOCK), F32), pltpu.VMEM((LANE, LANE), F32),
                pltpu.VMEM((LANE, width), F32), pltpu.VMEM((LANE, width), F32),
            ],
        ),
        compiler_params=_params(("parallel", "arbitrary"), _vmem_limit(blocks, scratch, 12 << 20)),
        name="moba_sample",
    )(page_table, p, p, p, *([ck] * npg), *([cv] * npg), rbx)


def _xattn_kernel(q_ref, mk_ref, mv_ref, o_ref):
    scale = HEAD_DIM ** -0.5
    for h in range(XA_HEADS):
        sl = slice(h * HEAD_DIM, (h + 1) * HEAD_DIM)
        s = _nt(q_ref[0, :, sl].astype(BF16), mk_ref[0, :, sl].astype(BF16)) * scale
        p = jnp.exp(s - jnp.max(s, axis=-1, keepdims=True))
        o = jnp.dot(p.astype(BF16), mv_ref[0, :, sl].astype(BF16), preferred_element_type=F32)
        o_ref[0, :, sl] = (o / jnp.sum(p, axis=-1, keepdims=True)).astype(BF16)


def cross_attend(p, qx_col, mk, mv, mk_idx, mv_idx, *, tq):
    bsz, t_total, _ = p.shape
    assert t_total % tq == 0 and qx_col % XA_WIDTH == 0
    blocks = tq * XA_WIDTH * 4 + 2 * N_MEM * XA_WIDTH * 4 + tq * XA_WIDTH * 2
    return pl.pallas_call(
        _xattn_kernel,
        out_shape=jax.ShapeDtypeStruct((bsz, t_total, XA_WIDTH), BF16),
        grid=(bsz, t_total // tq),
        in_specs=[
            pl.BlockSpec((1, tq, XA_WIDTH), lambda b, i: (b, i, qx_col // XA_WIDTH)),
            pl.BlockSpec((1, N_MEM, XA_WIDTH), lambda b, i: mk_idx(b)),
            pl.BlockSpec((1, N_MEM, XA_WIDTH), lambda b, i: mv_idx(b)),
        ],
        out_specs=pl.BlockSpec((1, tq, XA_WIDTH), lambda b, i: (b, i, 0)),
        compiler_params=_params(("parallel", "parallel"), _vmem_limit(blocks, 0, 8 * tq * N_MEM * 4)),
        name="cross_attend",
    )(p, mk, mv)


def _out_proj_kernel(mix_ref, xa_ref, w1_ref, w2_ref, res_ref, o_ref):
    o_ref[...] = (res_ref[...] + jnp.dot(mix_ref[...], w1_ref[...], preferred_element_type=F32)
                  + jnp.dot(xa_ref[...], w2_ref[...], preferred_element_type=F32))


def out_proj(mix, xa, w, res, *, tm, tn):
    m, d = res.shape
    assert m % tm == 0 and d % tn == 0 and TOK_WIDTH % XA_WIDTH == 0
    blocks = tm * D_MODEL * 2 + D_MODEL * tn * 2 + 2 * tm * tn * 4
    return pl.pallas_call(
        _out_proj_kernel,
        out_shape=jax.ShapeDtypeStruct((m, d), F32),
        grid=(m // tm, d // tn),
        in_specs=[
            pl.BlockSpec((tm, TOK_WIDTH), lambda i, j: (i, 0)),
            pl.BlockSpec((tm, XA_WIDTH), lambda i, j: (i, 0)),
            pl.BlockSpec((TOK_WIDTH, tn), lambda i, j: (0, j)),
            pl.BlockSpec((XA_WIDTH, tn), lambda i, j: (TOK_WIDTH // XA_WIDTH, j)),
            pl.BlockSpec((tm, tn), lambda i, j: (i, j)),
        ],
        out_specs=pl.BlockSpec((tm, tn), lambda i, j: (i, j)),
        compiler_params=_params(("parallel", "parallel"), _vmem_limit(blocks, 0, tm * tn * 4)),
        name="out_proj",
    )(mix, xa, w, w, res)


def _top_values(s, count):
    vals = []
    for _ in range(count):
        m = jnp.max(s, axis=0, keepdims=True)
        vals.append(m)
        s = jnp.where(s == m, -jnp.inf, s)
    return vals


def _peer_route_kernel(q_ref, sk_ref, e0_ref, cnt_ref, e1_ref, rnk_ref):
    kk = PEER_TOPK
    half = PEER_QDIM // 2
    for h in range(PEER_HEADS):
        s0 = _nt(sk_ref[2 * h], q_ref[:, (2 * h) * half:(2 * h + 1) * half].astype(BF16))
        s1 = _nt(sk_ref[2 * h + 1], q_ref[:, (2 * h + 1) * half:(2 * h + 2) * half].astype(BF16))
        top0 = _top_values(s0, kk)
        top1 = _top_values(s1, kk)
        cands = [top0[a] + top1[b] for a in range(kk) for b in range(kk) if (a + 1) * (b + 1) <= kk]
        n_pad = -len(cands) % SUBLANE
        cmat = jnp.concatenate(cands + [jnp.full_like(cands[0], -jnp.inf)] * n_pad, axis=0)
        rest = cmat
        for _ in range(kk):
            tau = jnp.max(rest, axis=0, keepdims=True)
            rest = jnp.where(rest == tau, -jnp.inf, rest)
        m_tot = top0[0] + top1[0]
        z = jnp.sum(jnp.where(cmat >= tau, jnp.exp(cmat - m_tot), 0.0), axis=0, keepdims=True)
        cnt = jnp.zeros(s0.shape, F32)
        rnk = jnp.zeros(s1.shape, F32)
        for b in range(kk):
            cnt = cnt + jnp.where(s0 + top1[b] >= tau, 1.0, 0.0)
            rnk = rnk + jnp.where(top1[b] > s1, 1.0, 0.0)
        e0_ref[h] = jnp.exp(s0 - top0[0]) / z
        cnt_ref[h] = cnt
        e1_ref[h] = jnp.exp(s1 - top1[0]).astype(BF16)
        rnk_ref[h] = rnk.astype(BF16)


def peer_route(q, subkeys, *, tb):
    m, _ = q.shape
    assert m % tb == 0
    sk = subkeys.reshape(PEER_HEADS * 2, PEER_NKEYS, PEER_QDIM // 2).astype(BF16)
    shp = jax.ShapeDtypeStruct((PEER_HEADS, PEER_NKEYS, m), F32)
    shp16 = jax.ShapeDtypeStruct((PEER_HEADS, PEER_NKEYS, m), BF16)
    ospec = pl.BlockSpec((PEER_HEADS, PEER_NKEYS, tb), lambda i: (0, 0, i))
    blocks = tb * PEER_HEADS * PEER_QDIM * 4 + sk.size * 2 + 4 * PEER_HEADS * PEER_NKEYS * tb * 4
    return pl.pallas_call(
        _peer_route_kernel,
        out_shape=(shp, shp, shp16, shp16),
        grid=(m // tb,),
        in_specs=[pl.BlockSpec((tb, PEER_HEADS * PEER_QDIM), lambda i: (i, 0)),
                  pl.BlockSpec(sk.shape, lambda i: (0, 0, 0))],
        out_specs=(ospec, ospec, ospec, ospec),
        compiler_params=_params(("parallel",), _vmem_limit(blocks, 0, 16 * PEER_NKEYS * tb * 4)),
        name="peer_route",
    )(q, sk)


def _peer_expert_kernel(xn_ref, u_ref, v_ref, e0_ref, cnt_ref, e1_ref, rnk_ref, res_ref, o_ref, *cast_refs, n_i):
    et = pl.program_id(1)

    @pl.when(et == 0)
    def _():
        o_ref[...] = res_ref[...]

    u = u_ref[...]
    v = v_ref[...]
    if cast_refs:
        u = u.astype(BF16)
        v = v.astype(BF16)
        cast_refs[0][...] = u
        cast_refs[1][...] = v
    h_t = _nt(u, xn_ref[...])
    act = (0.5 * h_t * (1.0 + lax.erf(h_t * (1.0 / math.sqrt(2.0))))).astype(BF16)
    w_rows = []
    for ii in range(n_i):
        i_row = et * n_i + ii
        g = jnp.zeros((PEER_NKEYS, h_t.shape[1]), BF16)
        for h in range(PEER_HEADS):
            cnt_i = cnt_ref[h, pl.ds(i_row, 1), :].astype(BF16)
            e0_i = e0_ref[h, pl.ds(i_row, 1), :].astype(BF16)
            g = g + jnp.where(rnk_ref[h] < cnt_i, e1_ref[h], jnp.zeros((), BF16)) * e0_i
        w_rows.append(g * act[ii * PEER_NKEYS:(ii + 1) * PEER_NKEYS])
    w_t = jnp.concatenate(w_rows, axis=0)
    o_ref[...] += _tn(w_t, v)


def peer_experts(xn, u, v, layer, route, res, *, tb, te, emit_cast):
    m, d = xn.shape
    n_exp = u.shape[-2]
    assert m % tb == 0 and n_exp % te == 0 and te % PEER_NKEYS == 0 and n_exp == PEER_NKEYS * PEER_NKEYS
    assert emit_cast == (u.ndim == 3) and (not emit_cast or m == tb)
    if u.ndim == 3:
        tspec = pl.BlockSpec((None, te, d), lambda i, e: (layer, e, 0))
    else:
        tspec = pl.BlockSpec((te, d), lambda i, e: (e, 0))
    rspec = pl.BlockSpec((PEER_HEADS, PEER_NKEYS, tb), lambda i, e: (0, 0, i))
    ospec = pl.BlockSpec((tb, d), lambda i, e: (i, 0))
    o_shape = jax.ShapeDtypeStruct((m, d), F32)
    tbytes = u.dtype.itemsize
    blocks = (tb * d * 2 + 2 * te * d * tbytes + PEER_HEADS * PEER_NKEYS * tb * (4 + 4 + 2 + 2) + 2 * tb * d * 4
              + (2 * te * d * 2 if emit_cast else 0))
    if emit_cast:
        cshape = jax.ShapeDtypeStruct((n_exp, d), BF16)
        cspec = pl.BlockSpec((te, d), lambda i, e: (e, 0))
        out_shape, out_specs = (o_shape, cshape, cshape), (ospec, cspec, cspec)
    else:
        out_shape, out_specs = o_shape, ospec
    return pl.pallas_call(
        functools.partial(_peer_expert_kernel, n_i=te // PEER_NKEYS),
        out_shape=out_shape,
        grid=(m // tb, n_exp // te),
        in_specs=[pl.BlockSpec((tb, d), lambda i, e: (i, 0)), tspec, tspec, rspec, rspec, rspec, rspec,
                  pl.BlockSpec((tb, d), lambda i, e: (i, 0))],
        out_specs=out_specs,
        compiler_params=_params(("parallel", "arbitrary"), _vmem_limit(blocks, 0, 6 * te * tb * 4)),
        name="peer_experts",
    )(xn, u, v, *route, res)


def peer_layer(x, g, w_q, subkeys, u, v, layer, *, tm, tb_route, tb_exp, te, emit_cast):
    q, xn = norm_matmul(x, g, w_q, tm=tm, tn=512, emit_xn=True)
    route = peer_route(q, subkeys, tb=tb_route)
    return peer_experts(xn, u, v, layer, route, x, tb=tb_exp, te=te, emit_cast=emit_cast)


def _pad_gla_weight(w):
    d = w.shape[0]
    n_tok = GLA_A_COL + GLA_RANK
    return jnp.concatenate([w[:, :n_tok], jnp.zeros((d, GLA_QX_COL - n_tok), w.dtype), w[:, n_tok:]],
                           axis=1).astype(BF16)


def kernel(x_prompt, x_sample, state_gla, cache_moba_k, cache_moba_v, cache_mem_k, cache_mem_v, page_table, mem_prompt, ln_mix, ln_mem, ln_ffn, ln_final, w_in_gla, w_alpha_gla, b_alpha_gla, g_head_gla, w_in_moba, rel_bias, w_mem_kv, w_out, w_peer_q, peer_subkeys, peer_u, peer_v):
    bsz, seq, d = x_prompt.shape
    dec_b, dec_t, _ = x_sample.shape
    assert dec_t <= SAMPLE_ROWS
    xp = x_prompt.reshape(bsz * seq, d)
    xs = jnp.pad(x_sample, ((0, 0), (0, SAMPLE_ROWS - dec_t), (0, 0))).reshape(dec_b * SAMPLE_ROWS, d)
    mem2d = mem_prompt.reshape(bsz * N_MEM, d)
    n_dec = dec_b * SAMPLE_ROWS
    cmk = cache_mem_k.reshape(dec_b, DEPTH * N_MEM, XA_WIDTH)
    cmv = cache_mem_v.reshape(dec_b, DEPTH * N_MEM, XA_WIDTH)

    gla_p, gla_s, kp_l, vp_l, ks_l, vs_l, mkp_l, mvp_l = [], [], [], [], [], [], [], []
    for i in range(DEPTH):
        j = i // N_MIXERS
        kv = norm_matmul(mem2d, ln_mem[i], w_mem_kv[i].astype(BF16), tm=512, tn=512).reshape(bsz, N_MEM, 2 * XA_WIDTH)
        mkp_l.append(kv[..., :XA_WIDTH].reshape(bsz, N_MEM, XA_HEADS, HEAD_DIM))
        mvp_l.append(kv[..., XA_WIDTH:].reshape(bsz, N_MEM, XA_HEADS, HEAD_DIM))
        if i % N_MIXERS == 0:
            w_in = _pad_gla_weight(w_in_gla[j])
            wa = jnp.zeros((LANE, GLA_HEADS * GLA_DK), F32).at[:GLA_RANK].set(w_alpha_gla[j]).astype(BF16)
            pp = norm_matmul(xp, ln_mix[i], w_in, tm=512, tn=512).reshape(bsz, seq, GLA_COLS)
            ps = norm_matmul(xs, ln_mix[i], w_in, tm=n_dec, tn=512).reshape(dec_b, SAMPLE_ROWS, GLA_COLS)
            zeros0 = jnp.zeros((bsz, 1, GLA_HEADS, GLA_DK, GLA_DV), F32)
            mix_p, s_p = gla_mix(pp, wa, b_alpha_gla[j], g_head_gla[j], zeros0, 0, chunk=GLA_CHUNK, t_valid=seq)
            ps_pad = jnp.pad(ps, ((0, 0), (0, GLA_SAMPLE_CHUNK - SAMPLE_ROWS), (0, 0)))
            mix_s, s_s = gla_mix(ps_pad, wa, b_alpha_gla[j], g_head_gla[j], state_gla, j,
                                 chunk=GLA_SAMPLE_CHUNK, t_valid=dec_t)
            mix_s = mix_s[:, :SAMPLE_ROWS]
            gla_p.append(s_p[:, 0])
            gla_s.append(s_s[:, 0])
            qx_col = GLA_QX_COL
        else:
            w_in = w_in_moba[j].astype(BF16)
            pp = norm_matmul(xp, ln_mix[i], w_in, tm=512, tn=512).reshape(bsz, seq, -1)
            ps = norm_matmul(xs, ln_mix[i], w_in, tm=n_dec, tn=512).reshape(dec_b, SAMPLE_ROWS, -1)
            mix_p = moba_prompt_mix(pp, rel_bias)
            mix_s = moba_sample_mix(ps, cache_moba_k, cache_moba_v, page_table, j, rel_bias)
            sh = lambda t, n: t.reshape(t.shape[0], n, MOBA_HEADS, HEAD_DIM)
            kp_l.append(sh(pp[..., TOK_WIDTH:2 * TOK_WIDTH], seq))
            vp_l.append(sh(pp[..., 2 * TOK_WIDTH:3 * TOK_WIDTH], seq))
            ks_l.append(sh(ps[:, :dec_t, TOK_WIDTH:2 * TOK_WIDTH], dec_t))
            vs_l.append(sh(ps[:, :dec_t, 2 * TOK_WIDTH:3 * TOK_WIDTH], dec_t))
            qx_col = MOBA_QX_COL
        xa_p = cross_attend(pp, qx_col, kv, kv, lambda b: (b, 0, 0), lambda b: (b, 0, 1), tq=512)
        xa_s = cross_attend(ps, qx_col, cmk, cmv, lambda b, i=i: (b, i, 0), lambda b, i=i: (b, i, 0), tq=SAMPLE_ROWS)
        w_o = w_out[i].astype(BF16)
        xp = out_proj(mix_p.reshape(bsz * seq, TOK_WIDTH), xa_p.reshape(bsz * seq, XA_WIDTH), w_o, xp, tm=512, tn=1024)
        xs = out_proj(mix_s.reshape(n_dec, TOK_WIDTH), xa_s.reshape(n_dec, XA_WIDTH), w_o, xs, tm=n_dec, tn=1024)
        w_q = w_peer_q[i].astype(BF16)
        xs, u16, v16 = peer_layer(xs, ln_ffn[i], w_q, peer_subkeys[i], peer_u, peer_v, i, tm=n_dec, tb_route=n_dec,
                                  tb_exp=n_dec, te=256, emit_cast=True)
        xp = peer_layer(xp, ln_ffn[i], w_q, peer_subkeys[i], u16, v16, None, tm=512, tb_route=256,
                        tb_exp=512, te=512, emit_cast=False)
    y_prompt = rmsnorm_rows(xp, ln_final, tm=512).reshape(bsz, seq, d)
    y_sample = rmsnorm_rows(xs, ln_final, tm=n_dec).reshape(dec_b, SAMPLE_ROWS, d)[:, :dec_t]
    return (y_prompt, y_sample,
            jnp.stack(gla_p, axis=1), jnp.stack(gla_s, axis=1),
            jnp.stack(kp_l, axis=1), jnp.stack(vp_l, axis=1),
            jnp.stack(ks_l, axis=1), jnp.stack(vs_l, axis=1),
            jnp.stack(mkp_l, axis=1), jnp.stack(mvp_l, axis=1))
```

```python
import functools
import math

import numpy as np
import jax
import jax.numpy as jnp
from jax import lax
from jax.experimental import pallas as pl
from jax.experimental.pallas import tpu as pltpu

D_MODEL = 2048
DEPTH = 2
N_MIXERS = 2
HEAD_DIM = 128
N_MEM = 256
XA_HEADS = 4
XA_WIDTH = XA_HEADS * HEAD_DIM
TOK_WIDTH = D_MODEL - XA_WIDTH
GLA_HEADS = 6
GLA_DV = TOK_WIDTH // GLA_HEADS
GLA_DK = GLA_DV // 2
GLA_RANK = 16
GLA_TAU = 16.0
GLA_CHUNK = 64
MOBA_HEADS = TOK_WIDTH // HEAD_DIM
MOBA_BLOCK = 256
MOBA_TOPK = 3
REL_BUCKETS = 32
REL_MAX_DIST = 128
PEER_HEADS = 8
PEER_NKEYS = 128
PEER_QDIM = 256
PEER_TOPK = 16
EPS = 1e-6

F32 = jnp.float32
BF16 = jnp.bfloat16

LANE = 128
SUBLANE = 8
VMEM_CAP_BYTES = 56 * 1024 * 1024

NEG = -1e30
_NT = (((1,), (1,)), ((), ()))
_TN = (((0,), (0,)), ((), ()))

GLA_SAMPLE_CHUNK = 16
SAMPLE_ROWS = 8
MOBA_SAMPLE_BLOCKS_PER_STEP = 2
GLA_HEADS_PER_STEP = 2


def _nt(a, b):
    return lax.dot_general(a, b, _NT, preferred_element_type=F32)


def _tn(a, b):
    return lax.dot_general(a, b, _TN, preferred_element_type=F32)


def _split3(x):
    x1 = x.astype(BF16)
    r1 = x - x1.astype(F32)
    x2 = r1.astype(BF16)
    x3 = (r1 - x2.astype(F32)).astype(BF16)
    return x1, x2, x3


def _nt_hi(a, b):
    a1 = a.astype(BF16)
    a2 = (a - a1.astype(F32)).astype(BF16)
    b1 = b.astype(BF16)
    b2 = (b - b1.astype(F32)).astype(BF16)
    return _nt(a1, b1) + _nt(a1, b2) + _nt(a2, b1)


def _vmem_limit(block_bytes, scratch_bytes=0, temp_bytes=0):
    return int(min(VMEM_CAP_BYTES, 2 * block_bytes + scratch_bytes + temp_bytes + (4 << 20)))


def _params(sem, vmem):
    return pltpu.CompilerParams(dimension_semantics=sem, vmem_limit_bytes=vmem)


def _norm_matmul_kernel(x_ref, g_ref, w_ref, o_ref, xn_ref):
    @pl.when(pl.program_id(1) == 0)
    def _():
        x = x_ref[...]
        y = x * lax.rsqrt(jnp.mean(x * x, axis=-1, keepdims=True) + EPS)
        xn_ref[...] = (y * g_ref[...]).astype(BF16)

    o_ref[...] = jnp.dot(xn_ref[...], w_ref[...], preferred_element_type=F32)


def norm_matmul(x, g, w, *, tm, tn, emit_xn=False):
    m, d = x.shape
    n = w.shape[1]
    assert m % tm == 0 and n % tn == 0 and w.shape[0] == d
    o_shape = jax.ShapeDtypeStruct((m, n), F32)
    o_spec = pl.BlockSpec((tm, tn), lambda i, j: (i, j))
    xn_bytes = tm * d * 2
    blocks = tm * d * 4 + d * 4 + d * tn * 2 + tm * tn * 4
    kwargs = dict(
        grid=(m // tm, n // tn),
        in_specs=[
            pl.BlockSpec((tm, d), lambda i, j: (i, 0)),
            pl.BlockSpec((1, d), lambda i, j: (0, 0)),
            pl.BlockSpec((d, tn), lambda i, j: (0, j)),
        ],
        name="norm_matmul",
    )
    if emit_xn:
        return pl.pallas_call(
            _norm_matmul_kernel,
            out_shape=(o_shape, jax.ShapeDtypeStruct((m, d), BF16)),
            out_specs=(o_spec, pl.BlockSpec((tm, d), lambda i, j: (i, 0))),
            compiler_params=_params(("parallel", "arbitrary"), _vmem_limit(blocks + xn_bytes, 0, tm * d * 8)),
            **kwargs,
        )(x, g.reshape(1, d), w)
    return pl.pallas_call(
        _norm_matmul_kernel,
        out_shape=o_shape,
        out_specs=o_spec,
        scratch_shapes=[pltpu.VMEM((tm, d), BF16)],
        compiler_params=_params(("parallel", "arbitrary"), _vmem_limit(blocks, xn_bytes, tm * d * 8)),
        **kwargs,
    )(x, g.reshape(1, d), w)


def _rmsnorm_kernel(x_ref, g_ref, o_ref):
    x = x_ref[...]
    o_ref[...] = x * lax.rsqrt(jnp.mean(x * x, axis=-1, keepdims=True) + EPS) * g_ref[...]


def rmsnorm_rows(x, g, *, tm):
    m, d = x.shape
    return pl.pallas_call(
        _rmsnorm_kernel,
        out_shape=jax.ShapeDtypeStruct((m, d), F32),
        grid=(m // tm,),
        in_specs=[pl.BlockSpec((tm, d), lambda i: (i, 0)), pl.BlockSpec((1, d), lambda i: (0, 0))],
        out_specs=pl.BlockSpec((tm, d), lambda i: (i, 0)),
        compiler_params=_params(("parallel",), _vmem_limit(2 * tm * d * 4, 0, tm * d * 4)),
        name="final_rmsnorm",
    )(x, g.reshape(1, d))


def _gla_constants(chunk):
    nlev = int(round(math.log2(chunk)))
    assert 1 << nlev == chunk
    idx = np.arange(chunk)
    col, row = idx[None, :], idx[:, None]
    mats = [col <= row, col > row]
    masks = []
    for lev in range(nlev):
        m = chunk >> (lev + 1)
        grp = idx // (2 * m)
        mid = grp * 2 * m + m
        up = (idx % (2 * m)) >= m
        mats.append((col >= mid[:, None]) & (col <= row) & up[:, None])
        mats.append((col > row) & (col < mid[:, None]) & (~up)[:, None])
        masks.append((grp[:, None] == grp[None, :]) & up[:, None] & (~up)[None, :])
    return (np.concatenate(mats, axis=0).astype(np.float32), np.stack(masks).astype(np.float32))


def _gla_kernel(q_ref, k_ref, v_ref, r_ref, a_ref, wa_ref, ba_ref, gh_ref, s0_ref, cmat_ref, lmask_ref,
                o_ref, sout_ref, st_ref, *, chunk, n_chunks, t_valid, t_total, hps):
    c_rows = chunk
    nlev = int(round(math.log2(chunk)))
    dk, dv = GLA_DK, GLA_DV
    for hh in range(hps):
        st_ref[hh] = jnp.transpose(s0_ref[0, 0, hh])

    def load_head(hh, rows):
        ksl = slice(hh * dk, (hh + 1) * dk)
        vsl = slice(hh * dv, (hh + 1) * dv)
        return (q_ref[0, rows, ksl], k_ref[0, rows, ksl], v_ref[0, rows, vsl], r_ref[0, rows, vsl], st_ref[hh])

    def one_head(hh, r0, a_blk, loaded):
        ksl = slice(hh * dk, (hh + 1) * dk)
        q, k, v, rr, st = loaded
        q = q * (dk ** -0.5)
        z = jnp.dot(a_blk, wa_ref[:, ksl], preferred_element_type=F32) + ba_ref[:, ksl]
        g = (jnp.minimum(z, 0.0) - jnp.log1p(jnp.exp(-jnp.abs(z)))) * (1.0 / GLA_TAU)
        if t_valid < t_total:
            t_idx = r0 + lax.broadcasted_iota(jnp.int32, (c_rows, 1), 0)
            g = jnp.where(t_idx < t_valid, g, 0.0)
        cm = cmat_ref[...]
        g1, g2, g3 = _split3(g)
        e = (jnp.dot(cm, g1, preferred_element_type=F32) + jnp.dot(cm, g2, preferred_element_type=F32)
             + jnp.dot(cm, g3, preferred_element_type=F32))
        qd = (q * jnp.exp(e[0:c_rows])).astype(BF16)
        kd = (k * jnp.exp(e[c_rows:2 * c_rows])).astype(BF16)
        att = jnp.zeros((c_rows, c_rows), F32)
        for lev in range(nlev):
            ea = e[(2 + 2 * lev) * c_rows:(3 + 2 * lev) * c_rows]
            eb = e[(3 + 2 * lev) * c_rows:(4 + 2 * lev) * c_rows]
            ql = (q * jnp.exp(ea)).astype(BF16)
            kl = (k * jnp.exp(eb)).astype(BF16)
            att = att + lmask_ref[lev] * _nt(ql, kl)
        vb = v.astype(BF16)
        o = jnp.dot(att.astype(BF16), vb, preferred_element_type=F32)
        o = o + jnp.sum(q * k, axis=-1, keepdims=True) * v
        o = o + _nt(qd, st.astype(BF16))
        b_last = e[c_rows - 1:c_rows, :]
        st_new = st * jnp.exp(b_last) + _tn(vb, kd)
        on = o * lax.rsqrt(jnp.mean(o * o, axis=-1, keepdims=True) + EPS) * gh_ref[...]
        return (on * (rr * jax.nn.sigmoid(rr))).astype(BF16), st_new

    def body(c, carry):
        r0 = pl.multiple_of(c * c_rows, c_rows)
        rows = pl.ds(r0, c_rows)
        a_blk = a_ref[0, rows, :].astype(BF16)
        loaded = [load_head(hh, rows) for hh in range(hps)]
        results = [one_head(hh, r0, a_blk, loaded[hh]) for hh in range(hps)]
        for hh, (out, st_new) in enumerate(results):
            o_ref[0, rows, hh * dv:(hh + 1) * dv] = out
            st_ref[hh] = st_new
        return carry

    lax.fori_loop(0, n_chunks, body, 0)
    for hh in range(hps):
        sout_ref[0, 0, hh] = jnp.transpose(st_ref[hh])


def gla_mix(p, w_alpha_pad, b_alpha, g_head, state0, state_layer, *, chunk, t_valid):
    bsz, t_total, _ = p.shape
    assert t_total % chunk == 0
    cmat, lmask = _gla_constants(chunk)
    n_e = cmat.shape[0]
    dk, dv = GLA_DK, GLA_DV
    hps = GLA_HEADS_PER_STEP
    n_hg = GLA_HEADS // hps
    kblk = lambda off: (lambda b, h: (b, 0, off + h))
    blocks = hps * (t_total * (2 * dk + 2 * dv) * 4 + t_total * dv * 2 + 2 * dk * dv * 4) + t_total * LANE * 4
    mix, s_out = pl.pallas_call(
        functools.partial(_gla_kernel, chunk=chunk, n_chunks=t_total // chunk, t_valid=t_valid, t_total=t_total,
                          hps=hps),
        out_shape=(jax.ShapeDtypeStruct((bsz, t_total, GLA_HEADS * dv), BF16),
                   jax.ShapeDtypeStruct((bsz, 1, GLA_HEADS, dk, dv), F32)),
        grid=(bsz, n_hg),
        in_specs=[
            pl.BlockSpec((1, t_total, hps * dk), kblk(0)),
            pl.BlockSpec((1, t_total, hps * dk), kblk(n_hg)),
            pl.BlockSpec((1, t_total, hps * dv), kblk(n_hg)),
            pl.BlockSpec((1, t_total, hps * dv), kblk(2 * n_hg)),
            pl.BlockSpec((1, t_total, LANE), lambda b, h: (b, 0, GLA_A_COL // LANE)),
            pl.BlockSpec((LANE, hps * dk), lambda b, h: (0, h)),
            pl.BlockSpec((1, hps * dk), lambda b, h: (0, h)),
            pl.BlockSpec((1, dv), lambda b, h: (0, 0)),
            pl.BlockSpec((1, 1, hps, dk, dv), lambda b, h: (b, state_layer, h, 0, 0)),
            pl.BlockSpec((n_e, chunk), lambda b, h: (0, 0)),
            pl.BlockSpec(lmask.shape, lambda b, h: (0, 0, 0)),
        ],
        out_specs=(pl.BlockSpec((1, t_total, hps * dv), lambda b, h: (b, 0, h)),
                   pl.BlockSpec((1, 1, hps, dk, dv), lambda b, h: (b, 0, h, 0, 0))),
        scratch_shapes=[pltpu.VMEM((hps, dv, dk), F32)],
        compiler_params=_params(("parallel", "parallel"), _vmem_limit(blocks, hps * dv * dk * 4, 8 << 20)),
        name="gla_mix",
    )(p, p, p, p, p, w_alpha_pad, b_alpha.reshape(1, -1), g_head.reshape(1, -1), state0,
      jnp.asarray(cmat, BF16), jnp.asarray(lmask, F32))
    return mix, s_out


GLA_A_COL = 2 * GLA_HEADS * GLA_DK + 2 * GLA_HEADS * GLA_DV
GLA_QX_COL = GLA_A_COL + XA_WIDTH
GLA_COLS = GLA_QX_COL + XA_WIDTH
MOBA_QX_COL = 3 * TOK_WIDTH


def _bucket_thresholds(max_dist):
    n = np.arange(max_dist + 1)
    max_exact = REL_BUCKETS // 2
    nf = np.maximum(n, 1).astype(np.float32)
    large = max_exact + (np.log(nf / np.float32(max_exact)) / np.float32(math.log(REL_MAX_DIST / max_exact))
                         * np.float32(REL_BUCKETS - max_exact)).astype(np.int32)
    large = np.minimum(large, REL_BUCKETS - 1)
    bucket = np.where(n < max_exact, n, large)
    assert np.all(np.diff(bucket) >= 0) and bucket[-1] == REL_BUCKETS - 1
    return [int(np.argmax(bucket >= b)) for b in range(REL_BUCKETS)]


def _moba_prompt_kernel(rb_ref, q_ref, k_ref, v_ref, o_ref, km_ref, vt_ref, bias_ref, *, nb, thr):
    h = pl.program_id(1)
    qi = pl.program_id(2)
    blk = MOBA_BLOCK
    n_sel = max(1, min(MOBA_TOPK, nb - 1))
    tk = lax.broadcasted_iota(jnp.int32, (blk, blk), 0)
    tq = lax.broadcasted_iota(jnp.int32, (blk, blk), 1)

    @pl.when(qi == 0)
    def _():
        for n in range(nb):
            km_ref[n:n + 1, :] = jnp.mean(k_ref[0, n * blk:(n + 1) * blk, :], axis=0, keepdims=True)
            vt_ref[:, n * blk:(n + 1) * blk] = jnp.transpose(v_ref[0, n * blk:(n + 1) * blk, :]).astype(BF16)
        for which in range(2):
            d = tq - tk + which * blk
            bias = jnp.full((blk, blk), rb_ref[h, 0], F32)
            for bkt in range(1, REL_BUCKETS):
                bias = jnp.where(d >= thr[bkt], rb_ref[h, bkt], bias)
            bias_ref[which] = bias

    q = q_ref[0]
    gate = _nt_hi(km_ref[...], q)
    gs = [gate[n:n + 1, :] for n in range(nb)]
    sels = []
    for n in range(nb):
        rank = jnp.zeros((1, blk), jnp.int32)
        for m in range(nb):
            if m != n:
                beats = (gs[m] >= gs[n]) if m < n else (gs[m] > gs[n])
                rank = rank + jnp.where(beats, 1, 0) * (m < qi).astype(jnp.int32)
        sels.append(jnp.where(rank < n_sel, 1, 0) * (n < qi).astype(jnp.int32))
    causal = jnp.where(tq >= tk, 1, 0)

    qb = q.astype(BF16)
    scale = HEAD_DIM ** -0.5

    def attend(n_keys):
        tiles = []
        for j in range(n_keys):
            s = _nt(k_ref[0, j * blk:(j + 1) * blk, :].astype(BF16), qb) * scale
            dj = qi - j
            bias = jnp.where(dj == 0, bias_ref[0], jnp.where(dj == 1, bias_ref[1], rb_ref[h, REL_BUCKETS - 1]))
            valid = jnp.where(dj == 0, causal, jnp.broadcast_to(sels[j], (blk, blk)))
            tiles.append(jnp.where(valid > 0, s + bias, NEG))
        m = jnp.max(tiles[0], axis=0, keepdims=True)
        for s in tiles[1:]:
            m = jnp.maximum(m, jnp.max(s, axis=0, keepdims=True))
        l = jnp.zeros((1, blk), F32)
        acc = jnp.zeros((HEAD_DIM, blk), F32)
        for j, s in enumerate(tiles):
            p = jnp.exp(s - m)
            l = l + jnp.sum(p, axis=0, keepdims=True)
            acc = acc + jnp.dot(vt_ref[:, j * blk:(j + 1) * blk], p.astype(BF16), preferred_element_type=F32)
        o_ref[0] = jnp.transpose(acc / l).astype(BF16)

    half = (nb + 1) // 2
    if half < nb:
        @pl.when(qi < half)
        def _():
            attend(half)

        @pl.when(qi >= half)
        def _():
            attend(nb)
    else:
        attend(nb)


def moba_prompt_mix(p, rel_bias):
    bsz, t_total, _ = p.shape
    blk = MOBA_BLOCK
    assert t_total % blk == 0
    nb = t_total // blk
    thr = _bucket_thresholds(2 * blk)
    assert thr[REL_BUCKETS - 1] <= blk + 1
    hd = HEAD_DIM
    blocks = blk * hd * 4 + 2 * t_total * hd * 4 + blk * hd * 2
    scratch = SUBLANE * hd * 4 + hd * t_total * 2 + 2 * blk * blk * 4
    return pl.pallas_call(
        functools.partial(_moba_prompt_kernel, nb=nb, thr=thr),
        out_shape=jax.ShapeDtypeStruct((bsz, t_total, MOBA_HEADS * hd), BF16),
        grid=(bsz, MOBA_HEADS, nb),
        in_specs=[
            pl.BlockSpec(memory_space=pltpu.SMEM),
            pl.BlockSpec((1, blk, hd), lambda b, h, i: (b, i, h)),
            pl.BlockSpec((1, t_total, hd), lambda b, h, i: (b, 0, MOBA_HEADS + h)),
            pl.BlockSpec((1, t_total, hd), lambda b, h, i: (b, 0, 2 * MOBA_HEADS + h)),
        ],
        out_specs=pl.BlockSpec((1, blk, hd), lambda b, h, i: (b, i, h)),
        scratch_shapes=[
            pltpu.VMEM((nb, hd), F32),
            pltpu.VMEM((hd, t_total), BF16),
            pltpu.VMEM((2, blk, blk), F32),
        ],
        compiler_params=_params(("parallel", "parallel", "arbitrary"), _vmem_limit(blocks, scratch, 2 * nb * blk * blk * 4)),
        name="moba_prompt",
    )(jnp.transpose(rel_bias), p, p, p)


def _moba_sample_kernel(pt_ref, q_ref, kn_ref, vn_ref, *rest, n_blocks, bps):
    del pt_ref
    npg = 2 * bps
    ck_refs, cv_refs = rest[:npg], rest[npg:2 * npg]
    (rbx_ref, o_ref, qbd_ref, ksum_ref, ma_ref, la_ref, oall_ref, blast_ref, bown_ref, kown_ref,
     vown_ref) = rest[2 * npg:]
    step = pl.program_id(1)
    n_steps = n_blocks // bps
    hd = HEAD_DIM
    blk = MOBA_BLOCK
    nrow = LANE
    width = MOBA_HEADS * hd
    scale = hd ** -0.5
    row = lax.broadcasted_iota(jnp.int32, (nrow, LANE), 0)
    lane = lax.broadcasted_iota(jnp.int32, (nrow, LANE), 1)
    t_row = row % SAMPLE_ROWS
    thr = _bucket_thresholds(blk + SAMPLE_ROWS)
    assert thr[REL_BUCKETS - 1] <= blk + 1

    @pl.when(step == 0)
    def _():
        q8 = q_ref[0]
        rowg = lax.broadcasted_iota(jnp.int32, (nrow, width), 0) // SAMPLE_ROWS
        colg = lax.broadcasted_iota(jnp.int32, (nrow, width), 1) // hd
        qbd_ref[...] = jnp.where(rowg == colg, jnp.concatenate([q8] * (nrow // SAMPLE_ROWS), axis=0), 0.0)
        ksum_ref[...] = jnp.zeros(ksum_ref.shape, F32)
        ma_ref[...] = jnp.full(ma_ref.shape, NEG, F32)
        la_ref[...] = jnp.zeros(la_ref.shape, F32)
        dlast = (blk + lax.broadcasted_iota(jnp.int32, (nrow, blk), 0) % SAMPLE_ROWS
                 - lax.broadcasted_iota(jnp.int32, (nrow, blk), 1))
        bias = jnp.broadcast_to(rbx_ref[:, 0:1], (nrow, blk))
        for bkt in range(1, REL_BUCKETS):
            bias = jnp.where(dlast >= thr[bkt], rbx_ref[:, bkt:bkt + 1], bias)
        blast_ref[...] = bias
        down = t_row - lane
        bias = jnp.zeros((nrow, LANE), F32)
        for dd in range(SAMPLE_ROWS):
            bias = jnp.where(down == dd, rbx_ref[:, dd:dd + 1], bias)
        bown_ref[...] = bias
        kown_ref[...] = jnp.zeros(kown_ref.shape, F32)
        vown_ref[...] = jnp.zeros(vown_ref.shape, F32)
        kown_ref[0:SAMPLE_ROWS, :] = kn_ref[0]
        vown_ref[0:SAMPLE_ROWS, :] = vn_ref[0]

    qbd = qbd_ref[...].astype(BF16)
    rowg128 = lax.broadcasted_iota(jnp.int32, (nrow, hd), 0) // SAMPLE_ROWS

    def partial_softmax(kb, vb, bias, valid):
        s = _nt(qbd, kb) * scale + bias
        if valid is not None:
            s = jnp.where(valid, s, NEG)
        m = jnp.max(s, axis=-1, keepdims=True)
        p = jnp.exp(s - m)
        if valid is not None:
            p = jnp.where(valid, p, 0.0)
        l = jnp.sum(p, axis=-1, keepdims=True)
        o_full = jnp.dot(p.astype(BF16), vb, preferred_element_type=F32)
        o = jnp.zeros((nrow, hd), F32)
        for hh in range(MOBA_HEADS):
            o = o + jnp.where(rowg128 == hh, o_full[:, hh * hd:(hh + 1) * hd], 0.0)
        return m, l, o

    def page_rows(ref):
        by_head = pltpu.einshape("phd->hpd", ref[...])
        return jnp.concatenate([by_head[hh] for hh in range(MOBA_HEADS)], axis=1)

    far_bias = rbx_ref[:, REL_BUCKETS - 1:REL_BUCKETS]
    ma_new = ma_ref[...]
    la_new = la_ref[...]
    for r in range(bps):
        b_idx = step * bps + r
        kb = jnp.concatenate([page_rows(ck_refs[2 * r]), page_rows(ck_refs[2 * r + 1])], axis=0)
        vb = jnp.concatenate([page_rows(cv_refs[2 * r]), page_rows(cv_refs[2 * r + 1])], axis=0)
        ksum_ref[pl.ds(b_idx, 1), :] = jnp.sum(kb, axis=0, keepdims=True)
        if r == bps - 1:
            bias = jnp.where(step == n_steps - 1, blast_ref[...], jnp.broadcast_to(far_bias, (nrow, blk)))
        else:
            bias = far_bias
        m, l, o = partial_softmax(kb.astype(BF16), vb.astype(BF16), bias, None)
        oall_ref[b_idx] = o
        ma_new = jnp.where(lane == b_idx, m, ma_new)
        la_new = jnp.where(lane == b_idx, l, la_new)
    ma_ref[...] = ma_new
    la_ref[...] = la_new

    @pl.when(step == n_steps - 1)
    def _():
        valid_own = jnp.logical_and(lane <= t_row, lane < SAMPLE_ROWS)
        m_own, l_own, o_own = partial_softmax(kown_ref[...].astype(BF16), vown_ref[...].astype(BF16),
                                              bown_ref[...], valid_own)
        kmean = ksum_ref[...] * (1.0 / MOBA_BLOCK)
        gate = _nt_hi(qbd_ref[...], kmean)
        g = jnp.where(lane < n_blocks, gate, -jnp.inf)
        sel_i = jnp.zeros((nrow, LANE), jnp.int32)
        for _ in range(max(1, min(MOBA_TOPK, n_blocks))):
            gmax = jnp.max(g, axis=-1, keepdims=True)
            first = jnp.min(jnp.where(g == gmax, lane, LANE), axis=-1, keepdims=True)
            pick = lane == first
            sel_i = jnp.where(pick, 1, sel_i)
            g = jnp.where(pick, -jnp.inf, g)
        sel = sel_i > 0
        ma = jnp.where(sel, ma_new, NEG)
        m_all = jnp.maximum(jnp.max(ma, axis=-1, keepdims=True), m_own)
        wa = jnp.where(sel, jnp.exp(ma - m_all), 0.0)
        w_own = jnp.exp(m_own - m_all)
        den = jnp.sum(wa * la_new, axis=-1, keepdims=True) + w_own * l_own
        num = w_own * o_own
        for n in range(n_blocks):
            num = num + wa[:, n:n + 1] * oall_ref[n]
        out = num / den
        o_ref[0] = jnp.concatenate([out[hh * SAMPLE_ROWS:(hh + 1) * SAMPLE_ROWS, :] for hh in range(MOBA_HEADS)],
                                   axis=1).astype(BF16)


def moba_sample_mix(p, cache_k, cache_v, page_table, layer, rel_bias):
    bs, rows, _ = p.shape
    n_pool, n_layers, page, heads, hd = cache_k.shape
    n_pages = page_table.shape[1]
    past_len = n_pages * page
    assert rows == SAMPLE_ROWS and heads == MOBA_HEADS and hd == HEAD_DIM and MOBA_BLOCK == 2 * page
    assert past_len % MOBA_BLOCK == 0 and MOBA_HEADS * SAMPLE_ROWS <= LANE and past_len // MOBA_BLOCK <= LANE
    n_blocks = past_len // MOBA_BLOCK
    bps = MOBA_SAMPLE_BLOCKS_PER_STEP if n_blocks % MOBA_SAMPLE_BLOCKS_PER_STEP == 0 else 1
    npg = 2 * bps
    width = heads * hd
    rbx = jnp.zeros((LANE, LANE), F32).at[:heads * SAMPLE_ROWS, :REL_BUCKETS].set(
        jnp.repeat(jnp.transpose(rel_bias), SAMPLE_ROWS, axis=0))

    def pmap(r):
        return lambda b, g, pt: (pt[b, g * npg + r], layer, 0, 0, 0)

    page_specs = [pl.BlockSpec((None, None, page, heads, hd), pmap(r)) for r in range(npg)]
    heads_pad = -(-heads // SUBLANE) * SUBLANE
    blocks = 3 * rows * width * 4 + 2 * npg * page * heads_pad * hd * 4 + LANE * LANE * 4 + rows * width * 2
    scratch = (LANE * width * 4 * 4 + 3 * LANE * LANE * 4 + LANE * MOBA_BLOCK * 4 + n_blocks * LANE * hd * 4)
    return pl.pallas_call(
        functools.partial(_moba_sample_kernel, n_blocks=n_blocks, bps=bps),
        out_shape=jax.ShapeDtypeStruct((bs, rows, width), BF16),
        grid_spec=pltpu.PrefetchScalarGridSpec(
            num_scalar_prefetch=1,
            grid=(bs, n_blocks // bps),
            in_specs=[
                pl.BlockSpec((1, rows, width), lambda b, g, pt: (b, 0, 0)),
                pl.BlockSpec((1, rows, width), lambda b, g, pt: (b, 0, 1)),
                pl.BlockSpec((1, rows, width), lambda b, g, pt: (b, 0, 2)),
                *page_specs, *page_specs,
                pl.BlockSpec((LANE, LANE), lambda b, g, pt: (0, 0)),
            ],
            out_specs=pl.BlockSpec((1, rows, width), lambda b, g, pt: (b, 0, 0)),
            scratch_shapes=[
                pltpu.VMEM((LANE, width), F32),
                pltpu.VMEM((LANE, width), F32),
                pltpu.VMEM((LANE, LANE), F32), pltpu.VMEM((LANE, LANE), F32),
                pltpu.VMEM((n_blocks, LANE, hd), F32),
                pltpu.VMEM((LANE, MOBA_BLOCK), F32), pltpu.VMEM((LANE, LANE), F32),
                pltpu.VMEM((LANE, width), F32), pltpu.VMEM((LANE, width), F32),
            ],
        ),
        compiler_params=_params(("parallel", "arbitrary"), _vmem_limit(blocks, scratch, 12 << 20)),
        name="moba_sample",
    )(page_table, p, p, p, *([cache_k] * npg), *([cache_v] * npg), rbx)


def _xattn_kernel(q_ref, mk_ref, mv_ref, o_ref):
    scale = HEAD_DIM ** -0.5
    for h in range(XA_HEADS):
        sl = slice(h * HEAD_DIM, (h + 1) * HEAD_DIM)
        s = _nt(q_ref[0, :, sl].astype(BF16), mk_ref[:, h, :].astype(BF16)) * scale
        p = jnp.exp(s - jnp.max(s, axis=-1, keepdims=True))
        o = jnp.dot(p.astype(BF16), mv_ref[:, h, :].astype(BF16), preferred_element_type=F32)
        o_ref[0, :, sl] = (o / jnp.sum(p, axis=-1, keepdims=True)).astype(BF16)


def cross_attend(p, qx_col, mk, mv, mk_spec, mv_spec, *, tq):
    bsz, t_total, _ = p.shape
    assert t_total % tq == 0 and qx_col % XA_WIDTH == 0
    blocks = tq * XA_WIDTH * 4 + 2 * N_MEM * SUBLANE * HEAD_DIM * 4 + tq * XA_WIDTH * 2
    return pl.pallas_call(
        _xattn_kernel,
        out_shape=jax.ShapeDtypeStruct((bsz, t_total, XA_WIDTH), BF16),
        grid=(bsz, t_total // tq),
        in_specs=[
            pl.BlockSpec((1, tq, XA_WIDTH), lambda b, i: (b, i, qx_col // XA_WIDTH)),
            mk_spec,
            mv_spec,
        ],
        out_specs=pl.BlockSpec((1, tq, XA_WIDTH), lambda b, i: (b, i, 0)),
        compiler_params=_params(("parallel", "parallel"), _vmem_limit(blocks, 0, 8 * tq * N_MEM * 4)),
        name="cross_attend",
    )(p, mk, mv)


def _out_proj_kernel(mix_ref, xa_ref, w1_ref, w2_ref, res_ref, o_ref):
    o_ref[...] = (res_ref[...] + jnp.dot(mix_ref[...], w1_ref[...], preferred_element_type=F32)
                  + jnp.dot(xa_ref[...], w2_ref[...], preferred_element_type=F32))


def out_proj(mix, xa, w, res, *, tm, tn):
    m, d = res.shape
    assert m % tm == 0 and d % tn == 0 and TOK_WIDTH % XA_WIDTH == 0
    blocks = tm * D_MODEL * 2 + D_MODEL * tn * 2 + 2 * tm * tn * 4
    return pl.pallas_call(
        _out_proj_kernel,
        out_shape=jax.ShapeDtypeStruct((m, d), F32),
        grid=(m // tm, d // tn),
        in_specs=[
            pl.BlockSpec((tm, TOK_WIDTH), lambda i, j: (i, 0)),
            pl.BlockSpec((tm, XA_WIDTH), lambda i, j: (i, 0)),
            pl.BlockSpec((TOK_WIDTH, tn), lambda i, j: (0, j)),
            pl.BlockSpec((XA_WIDTH, tn), lambda i, j: (TOK_WIDTH // XA_WIDTH, j)),
            pl.BlockSpec((tm, tn), lambda i, j: (i, j)),
        ],
        out_specs=pl.BlockSpec((tm, tn), lambda i, j: (i, j)),
        compiler_params=_params(("parallel", "parallel"), _vmem_limit(blocks, 0, tm * tn * 4)),
        name="out_proj",
    )(mix, xa, w, w, res)


def _top_values(s, count):
    vals = []
    for _ in range(count):
        m = jnp.max(s, axis=0, keepdims=True)
        vals.append(m)
        s = jnp.where(s == m, -jnp.inf, s)
    return vals


def _peer_route_kernel(q_ref, sk_ref, e0_ref, cnt_ref, e1_ref, rnk_ref):
    kk = PEER_TOPK
    half = PEER_QDIM // 2
    e0_heads, cnt_heads = [], []
    for h in range(PEER_HEADS):
        s0 = _nt(sk_ref[2 * h], q_ref[:, (2 * h) * half:(2 * h + 1) * half].astype(BF16))
        s1 = _nt(sk_ref[2 * h + 1], q_ref[:, (2 * h + 1) * half:(2 * h + 2) * half].astype(BF16))
        top0 = _top_values(s0, kk)
        top1 = _top_values(s1, kk)
        cands = [top0[a] + top1[b] for a in range(kk) for b in range(kk) if (a + 1) * (b + 1) <= kk]
        n_pad = -len(cands) % SUBLANE
        cmat = jnp.concatenate(cands + [jnp.full_like(cands[0], -jnp.inf)] * n_pad, axis=0)
        rest = cmat
        for _ in range(kk):
            tau = jnp.max(rest, axis=0, keepdims=True)
            rest = jnp.where(rest == tau, -jnp.inf, rest)
        m_tot = top0[0] + top1[0]
        z = jnp.sum(jnp.where(cmat >= tau, jnp.exp(cmat - m_tot), 0.0), axis=0, keepdims=True)
        cnt = jnp.zeros(s0.shape, F32)
        rnk = jnp.zeros(s1.shape, F32)
        for b in range(kk):
            cnt = cnt + jnp.where(s0 + top1[b] >= tau, 1.0, 0.0)
            rnk = rnk + jnp.where(top1[b] > s1, 1.0, 0.0)
        e0_heads.append(jnp.exp(s0 - top0[0]) / z)
        cnt_heads.append(cnt)
        e1_ref[h] = jnp.exp(s1 - top1[0]).astype(BF16)
        rnk_ref[h] = rnk.astype(BF16)
    e0_ref[...] = pltpu.einshape("hit->iht", jnp.stack(e0_heads, axis=0))
    cnt_ref[...] = pltpu.einshape("hit->iht", jnp.stack(cnt_heads, axis=0))


def peer_route(q, subkeys, *, tb):
    m, _ = q.shape
    assert m % tb == 0
    sk = subkeys.reshape(PEER_HEADS * 2, PEER_NKEYS, PEER_QDIM // 2).astype(BF16)
    shp = jax.ShapeDtypeStruct((PEER_NKEYS, PEER_HEADS, m), F32)
    shp16 = jax.ShapeDtypeStruct((PEER_HEADS, PEER_NKEYS, m), BF16)
    ospec = pl.BlockSpec((PEER_NKEYS, PEER_HEADS, tb), lambda i: (0, 0, i))
    ospec16 = pl.BlockSpec((PEER_HEADS, PEER_NKEYS, tb), lambda i: (0, 0, i))
    blocks = tb * PEER_HEADS * PEER_QDIM * 4 + sk.size * 2 + 4 * PEER_HEADS * PEER_NKEYS * tb * 4
    return pl.pallas_call(
        _peer_route_kernel,
        out_shape=(shp, shp, shp16, shp16),
        grid=(m // tb,),
        in_specs=[pl.BlockSpec((tb, PEER_HEADS * PEER_QDIM), lambda i: (i, 0)),
                  pl.BlockSpec(sk.shape, lambda i: (0, 0, 0))],
        out_specs=(ospec, ospec, ospec16, ospec16),
        compiler_params=_params(("parallel",), _vmem_limit(blocks, 0, 16 * PEER_NKEYS * tb * 4)),
        name="peer_route",
    )(q, sk)


def _peer_expert_kernel(xn_ref, u_ref, v_ref, e0_ref, cnt_ref, e1_ref, rnk_ref, res_ref, o_ref, *cast_refs, n_i):
    *cast_refs, act_ref = cast_refs
    et = pl.program_id(1)

    @pl.when(et == 0)
    def _():
        o_ref[...] = res_ref[...]
        act_ref[...] = jnp.zeros(act_ref.shape, BF16)

    act = act_ref[...]
    u = u_ref[...]
    v = v_ref[...]
    if cast_refs:
        u = u.astype(BF16)
        v = v.astype(BF16)
        cast_refs[0][...] = u
        cast_refs[1][...] = v
    h_t = _nt(u, xn_ref[...])
    act_ref[...] = (0.5 * h_t * (1.0 + lax.erf(h_t * (1.0 / math.sqrt(2.0))))).astype(BF16)
    w_rows = []
    for ii in range(n_i):
        g = jnp.zeros((PEER_NKEYS, h_t.shape[1]), BF16)
        for h in range(PEER_HEADS):
            cnt_i = jnp.broadcast_to(cnt_ref[ii, h:h + 1, :], g.shape).astype(BF16)
            e0_i = jnp.broadcast_to(e0_ref[ii, h:h + 1, :], g.shape).astype(BF16)
            g = g + jnp.where(rnk_ref[h] < cnt_i, e1_ref[h], jnp.zeros((), BF16)) * e0_i
        w_rows.append(g * act[ii * PEER_NKEYS:(ii + 1) * PEER_NKEYS])
    w_t = jnp.concatenate(w_rows, axis=0)
    o_ref[...] += _tn(w_t, v)


def peer_experts(xn, u, v, layer, route, res, *, tb, te, emit_cast):
    m, d = xn.shape
    n_exp = u.shape[-2]
    assert m % tb == 0 and n_exp % te == 0 and te % PEER_NKEYS == 0 and n_exp == PEER_NKEYS * PEER_NKEYS
    assert emit_cast == (u.ndim == 3) and (not emit_cast or m == tb)
    n_tiles = n_exp // te
    first = lambda e: jnp.minimum(e, n_tiles - 1)
    second = lambda e: jnp.maximum(e - 1, 0)
    if u.ndim == 3:
        uspec = pl.BlockSpec((None, te, d), lambda i, e: (layer, first(e), 0))
        vspec = pl.BlockSpec((None, te, d), lambda i, e: (layer, second(e), 0))
    else:
        uspec = pl.BlockSpec((te, d), lambda i, e: (first(e), 0))
        vspec = pl.BlockSpec((te, d), lambda i, e: (second(e), 0))
    n_i = te // PEER_NKEYS
    ispec = pl.BlockSpec((n_i, PEER_HEADS, tb), lambda i, e: (second(e), 0, i))
    rspec = pl.BlockSpec((PEER_HEADS, PEER_NKEYS, tb), lambda i, e: (0, 0, i))
    ospec = pl.BlockSpec((tb, d), lambda i, e: (i, 0))
    o_shape = jax.ShapeDtypeStruct((m, d), F32)
    tbytes = u.dtype.itemsize
    blocks = (tb * d * 2 + 2 * te * d * tbytes + 2 * n_i * PEER_HEADS * tb * 4 + 2 * PEER_HEADS * PEER_NKEYS * tb * 2
              + 2 * tb * d * 4 + (2 * te * d * 2 if emit_cast else 0))
    if emit_cast:
        cshape = jax.ShapeDtypeStruct((n_exp, d), BF16)
        out_shape = (o_shape, cshape, cshape)
        out_specs = (ospec, pl.BlockSpec((te, d), lambda i, e: (first(e), 0)),
                     pl.BlockSpec((te, d), lambda i, e: (second(e), 0)))
    else:
        out_shape, out_specs = o_shape, ospec
    return pl.pallas_call(
        functools.partial(_peer_expert_kernel, n_i=n_i),
        out_shape=out_shape,
        grid=(m // tb, n_tiles + 1),
        in_specs=[pl.BlockSpec((tb, d), lambda i, e: (i, 0)), uspec, vspec, ispec, ispec, rspec, rspec,
                  pl.BlockSpec((tb, d), lambda i, e: (i, 0))],
        out_specs=out_specs,
        scratch_shapes=[pltpu.VMEM((te, tb), BF16)],
        compiler_params=_params(("parallel", "arbitrary"), _vmem_limit(blocks, te * tb * 2, 6 * te * tb * 4)),
        name="peer_experts",
    )(xn, u, v, *route, res)


def peer_layer(x, g, w_q, subkeys, u, v, layer, *, tm, tb_route, tb_exp, te, emit_cast):
    q, xn = norm_matmul(x, g, w_q, tm=tm, tn=512, emit_xn=True)
    route = peer_route(q, subkeys, tb=tb_route)
    return peer_experts(xn, u, v, layer, route, x, tb=tb_exp, te=te, emit_cast=emit_cast)


def _pad_gla_weight(w):
    d = w.shape[0]
    n_tok = GLA_A_COL + GLA_RANK
    return jnp.concatenate([w[:, :n_tok], jnp.zeros((d, GLA_QX_COL - n_tok), w.dtype), w[:, n_tok:]],
                           axis=1).astype(BF16)


def kernel(x_prompt, x_sample, state_gla, cache_moba_k, cache_moba_v, cache_mem_k, cache_mem_v, page_table, mem_prompt, ln_mix, ln_mem, ln_ffn, ln_final, w_in_gla, w_alpha_gla, b_alpha_gla, g_head_gla, w_in_moba, rel_bias, w_mem_kv, w_out, w_peer_q, peer_subkeys, peer_u, peer_v):
    bsz, seq, d = x_prompt.shape
    dec_b, dec_t, _ = x_sample.shape
    assert dec_t <= SAMPLE_ROWS
    xp = x_prompt.reshape(bsz * seq, d)
    xs = jnp.pad(x_sample, ((0, 0), (0, SAMPLE_ROWS - dec_t), (0, 0))).reshape(dec_b * SAMPLE_ROWS, d)
    mem2d = mem_prompt.reshape(bsz * N_MEM, d)
    n_dec = dec_b * SAMPLE_ROWS

    gla_p, gla_s, kp_l, vp_l, ks_l, vs_l, mkp_l, mvp_l = [], [], [], [], [], [], [], []
    for i in range(DEPTH):
        j = i // N_MIXERS
        kv = norm_matmul(mem2d, ln_mem[i], w_mem_kv[i].astype(BF16), tm=512, tn=512).reshape(bsz, N_MEM, 2 * XA_WIDTH)
        mkp_l.append(kv[..., :XA_WIDTH].reshape(bsz, N_MEM, XA_HEADS, HEAD_DIM))
        mvp_l.append(kv[..., XA_WIDTH:].reshape(bsz, N_MEM, XA_HEADS, HEAD_DIM))
        if i % N_MIXERS == 0:
            w_in = _pad_gla_weight(w_in_gla[j])
            wa = jnp.zeros((LANE, GLA_HEADS * GLA_DK), F32).at[:GLA_RANK].set(w_alpha_gla[j]).astype(BF16)
            pp = norm_matmul(xp, ln_mix[i], w_in, tm=512, tn=512).reshape(bsz, seq, GLA_COLS)
            ps = norm_matmul(xs, ln_mix[i], w_in, tm=n_dec, tn=512).reshape(dec_b, SAMPLE_ROWS, GLA_COLS)
            zeros0 = jnp.zeros((bsz, 1, GLA_HEADS, GLA_DK, GLA_DV), F32)
            mix_p, s_p = gla_mix(pp, wa, b_alpha_gla[j], g_head_gla[j], zeros0, 0, chunk=GLA_CHUNK, t_valid=seq)
            ps_pad = jnp.pad(ps, ((0, 0), (0, GLA_SAMPLE_CHUNK - SAMPLE_ROWS), (0, 0)))
            mix_s, s_s = gla_mix(ps_pad, wa, b_alpha_gla[j], g_head_gla[j], state_gla, j,
                                 chunk=GLA_SAMPLE_CHUNK, t_valid=dec_t)
            mix_s = mix_s[:, :SAMPLE_ROWS]
            gla_p.append(s_p[:, 0])
            gla_s.append(s_s[:, 0])
            qx_col = GLA_QX_COL
        else:
            w_in = w_in_moba[j].astype(BF16)
            pp = norm_matmul(xp, ln_mix[i], w_in, tm=512, tn=512).reshape(bsz, seq, -1)
            ps = norm_matmul(xs, ln_mix[i], w_in, tm=n_dec, tn=512).reshape(dec_b, SAMPLE_ROWS, -1)
            mix_p = moba_prompt_mix(pp, rel_bias)
            mix_s = moba_sample_mix(ps, cache_moba_k, cache_moba_v, page_table, j, rel_bias)
            sh = lambda t, n: t.reshape(t.shape[0], n, MOBA_HEADS, HEAD_DIM)
            kp_l.append(sh(pp[..., TOK_WIDTH:2 * TOK_WIDTH], seq))
            vp_l.append(sh(pp[..., 2 * TOK_WIDTH:3 * TOK_WIDTH], seq))
            ks_l.append(sh(ps[:, :dec_t, TOK_WIDTH:2 * TOK_WIDTH], dec_t))
            vs_l.append(sh(ps[:, :dec_t, 2 * TOK_WIDTH:3 * TOK_WIDTH], dec_t))
            qx_col = MOBA_QX_COL
        kv5 = kv.reshape(bsz, N_MEM, 2, XA_HEADS, HEAD_DIM)
        mem_blk = (N_MEM, XA_HEADS, HEAD_DIM)
        xa_p = cross_attend(pp, qx_col, kv5, kv5,
                            pl.BlockSpec((None, N_MEM, None, XA_HEADS, HEAD_DIM), lambda b, t: (b, 0, 0, 0, 0)),
                            pl.BlockSpec((None, N_MEM, None, XA_HEADS, HEAD_DIM), lambda b, t: (b, 0, 1, 0, 0)), tq=512)
        cache_spec = pl.BlockSpec((None, None) + mem_blk, lambda b, t, i=i: (b, i, 0, 0, 0))
        xa_s = cross_attend(ps, qx_col, cache_mem_k, cache_mem_v, cache_spec, cache_spec, tq=SAMPLE_ROWS)
        w_o = w_out[i].astype(BF16)
        xp = out_proj(mix_p.reshape(bsz * seq, TOK_WIDTH), xa_p.reshape(bsz * seq, XA_WIDTH), w_o, xp, tm=512, tn=1024)
        xs = out_proj(mix_s.reshape(n_dec, TOK_WIDTH), xa_s.reshape(n_dec, XA_WIDTH), w_o, xs, tm=n_dec, tn=1024)
        w_q = w_peer_q[i].astype(BF16)
        xs, u16, v16 = peer_layer(xs, ln_ffn[i], w_q, peer_subkeys[i], peer_u, peer_v, i, tm=n_dec, tb_route=n_dec,
                                  tb_exp=n_dec, te=256, emit_cast=True)
        xp = peer_layer(xp, ln_ffn[i], w_q, peer_subkeys[i], u16, v16, None, tm=512, tb_route=256,
                        tb_exp=512, te=1024, emit_cast=False)
    y_prompt = rmsnorm_rows(xp, ln_final, tm=512).reshape(bsz, seq, d)
    y_sample = rmsnorm_rows(xs, ln_final, tm=n_dec).reshape(dec_b, SAMPLE_ROWS, d)[:, :dec_t]
    return (y_prompt, y_sample,
            jnp.stack(gla_p, axis=1), jnp.stack(gla_s, axis=1),
            jnp.stack(kp_l, axis=1), jnp.stack(vp_l, axis=1),
            jnp.stack(ks_l, axis=1), jnp.stack(vs_l, axis=1),
            jnp.stack(mkp_l, axis=1), jnp.stack(mvp_l, axis=1))
```

```python
import functools
import math

import numpy as np
import jax
import jax.numpy as jnp
from jax import lax
from jax.experimental import pallas as pl
from jax.experimental.pallas import tpu as pltpu

D_MODEL = 2048
DEPTH = 2
N_MIXERS = 2
HEAD_DIM = 128
N_MEM = 256
XA_HEADS = 4
XA_WIDTH = XA_HEADS * HEAD_DIM
TOK_WIDTH = D_MODEL - XA_WIDTH
GLA_HEADS = 6
GLA_DV = TOK_WIDTH // GLA_HEADS
GLA_DK = GLA_DV // 2
GLA_RANK = 16
GLA_TAU = 16.0
GLA_CHUNK = 64
MOBA_HEADS = TOK_WIDTH // HEAD_DIM
MOBA_BLOCK = 256
MOBA_TOPK = 3
REL_BUCKETS = 32
REL_MAX_DIST = 128
PEER_HEADS = 8
PEER_NKEYS = 128
PEER_QDIM = 256
PEER_TOPK = 16
EPS = 1e-6

F32 = jnp.float32
BF16 = jnp.bfloat16

LANE = 128
SUBLANE = 8
VMEM_CAP_BYTES = 56 * 1024 * 1024

NEG = -1e30
_NT = (((1,), (1,)), ((), ()))
_TN = (((0,), (0,)), ((), ()))

GLA_PROMPT_CHUNK = 128
GLA_SAMPLE_CHUNK = 16
SAMPLE_ROWS = 8
MOBA_SAMPLE_BLOCKS_PER_STEP = 4
GLA_HEADS_PER_STEP = 2


def _nt(a, b):
    return lax.dot_general(a, b, _NT, preferred_element_type=F32)


def _tn(a, b):
    return lax.dot_general(a, b, _TN, preferred_element_type=F32)


def _split3(x):
    x1 = x.astype(BF16)
    r1 = x - x1.astype(F32)
    x2 = r1.astype(BF16)
    x3 = (r1 - x2.astype(F32)).astype(BF16)
    return x1, x2, x3


def _nt_hi(a, b):
    a1 = a.astype(BF16)
    a2 = (a - a1.astype(F32)).astype(BF16)
    b1 = b.astype(BF16)
    b2 = (b - b1.astype(F32)).astype(BF16)
    return _nt(a1, b1) + _nt(a1, b2) + _nt(a2, b1)


def _vmem_limit(block_bytes, scratch_bytes=0, temp_bytes=0):
    return int(min(VMEM_CAP_BYTES, 2 * block_bytes + scratch_bytes + temp_bytes + (4 << 20)))


def _params(sem, vmem):
    return pltpu.CompilerParams(dimension_semantics=sem, vmem_limit_bytes=vmem)


def _norm_matmul_kernel(x_ref, g_ref, w_ref, o_ref, xn_ref):
    @pl.when(pl.program_id(1) == 0)
    def _():
        x = x_ref[...]
        y = x * lax.rsqrt(jnp.mean(x * x, axis=-1, keepdims=True) + EPS)
        xn_ref[...] = (y * g_ref[...]).astype(BF16)

    o_ref[...] = jnp.dot(xn_ref[...], w_ref[...], preferred_element_type=F32)


def norm_matmul(x, g, w, *, tm, tn, emit_xn=False):
    m, d = x.shape
    n = w.shape[1]
    assert m % tm == 0 and n % tn == 0 and w.shape[0] == d
    o_shape = jax.ShapeDtypeStruct((m, n), F32)
    o_spec = pl.BlockSpec((tm, tn), lambda i, j: (i, j))
    xn_bytes = tm * d * 2
    blocks = tm * d * 4 + d * 4 + d * tn * 2 + tm * tn * 4
    kwargs = dict(
        grid=(m // tm, n // tn),
        in_specs=[
            pl.BlockSpec((tm, d), lambda i, j: (i, 0)),
            pl.BlockSpec((1, d), lambda i, j: (0, 0)),
            pl.BlockSpec((d, tn), lambda i, j: (0, j)),
        ],
        name="norm_matmul",
    )
    if emit_xn:
        return pl.pallas_call(
            _norm_matmul_kernel,
            out_shape=(o_shape, jax.ShapeDtypeStruct((m, d), BF16)),
            out_specs=(o_spec, pl.BlockSpec((tm, d), lambda i, j: (i, 0))),
            compiler_params=_params(("parallel", "arbitrary"), _vmem_limit(blocks + xn_bytes, 0, tm * d * 8)),
            **kwargs,
        )(x, g.reshape(1, d), w)
    return pl.pallas_call(
        _norm_matmul_kernel,
        out_shape=o_shape,
        out_specs=o_spec,
        scratch_shapes=[pltpu.VMEM((tm, d), BF16)],
        compiler_params=_params(("parallel", "arbitrary"), _vmem_limit(blocks, xn_bytes, tm * d * 8)),
        **kwargs,
    )(x, g.reshape(1, d), w)


def _rmsnorm_kernel(x_ref, g_ref, o_ref):
    x = x_ref[...]
    o_ref[...] = x * lax.rsqrt(jnp.mean(x * x, axis=-1, keepdims=True) + EPS) * g_ref[...]


def rmsnorm_rows(x, g, *, tm):
    m, d = x.shape
    return pl.pallas_call(
        _rmsnorm_kernel,
        out_shape=jax.ShapeDtypeStruct((m, d), F32),
        grid=(m // tm,),
        in_specs=[pl.BlockSpec((tm, d), lambda i: (i, 0)), pl.BlockSpec((1, d), lambda i: (0, 0))],
        out_specs=pl.BlockSpec((tm, d), lambda i: (i, 0)),
        compiler_params=_params(("parallel",), _vmem_limit(2 * tm * d * 4, 0, tm * d * 4)),
        name="final_rmsnorm",
    )(x, g.reshape(1, d))


def _gla_constants(chunk):
    nlev = int(round(math.log2(chunk)))
    assert 1 << nlev == chunk
    idx = np.arange(chunk)
    col, row = idx[None, :], idx[:, None]
    mats = [col <= row, col > row]
    masks = []
    for lev in range(nlev):
        m = chunk >> (lev + 1)
        grp = idx // (2 * m)
        mid = grp * 2 * m + m
        up = (idx % (2 * m)) >= m
        mats.append((col >= mid[:, None]) & (col <= row) & up[:, None])
        mats.append((col > row) & (col < mid[:, None]) & (~up)[:, None])
        masks.append((grp[:, None] == grp[None, :]) & up[:, None] & (~up)[None, :])
    return (np.concatenate(mats, axis=0).astype(np.float32), np.stack(masks).astype(np.float32))


def _gla_kernel(q_ref, k_ref, v_ref, r_ref, a_ref, wa_ref, ba_ref, gh_ref, s0_ref, cmat_ref, lmask_ref,
                o_ref, sout_ref, st_ref, *, chunk, n_chunks, t_valid, t_total, hps):
    c_rows = chunk
    nlev = int(round(math.log2(chunk)))
    dk, dv = GLA_DK, GLA_DV
    for hh in range(hps):
        st_ref[hh] = jnp.transpose(s0_ref[0, 0, hh])

    def load_head(hh, rows):
        ksl = slice(hh * dk, (hh + 1) * dk)
        vsl = slice(hh * dv, (hh + 1) * dv)
        return (q_ref[0, rows, ksl], k_ref[0, rows, ksl], v_ref[0, rows, vsl], r_ref[0, rows, vsl], st_ref[hh])

    def range_sums(r0, a_blk):
        z = jnp.dot(a_blk, wa_ref[...], preferred_element_type=F32) + ba_ref[...]
        g = (jnp.minimum(z, 0.0) - jnp.log1p(jnp.exp(-jnp.abs(z)))) * (1.0 / GLA_TAU)
        if t_valid < t_total:
            t_idx = r0 + lax.broadcasted_iota(jnp.int32, (c_rows, 1), 0)
            g = jnp.where(t_idx < t_valid, g, 0.0)
        e3 = jnp.dot(cmat_ref[...], jnp.concatenate(_split3(g), axis=1), preferred_element_type=F32)
        w = hps * dk
        return e3[:, 0:w] + e3[:, w:2 * w] + e3[:, 2 * w:3 * w]

    def one_head(hh, e_all, loaded):
        q, k, v, rr, st = loaded
        q = q * (dk ** -0.5)
        e = e_all[:, hh * dk:(hh + 1) * dk]
        qd = (q * jnp.exp(e[0:c_rows])).astype(BF16)
        kd = (k * jnp.exp(e[c_rows:2 * c_rows])).astype(BF16)
        att = jnp.zeros((c_rows, c_rows), F32)
        for lev in range(nlev):
            ea = e[(2 + 2 * lev) * c_rows:(3 + 2 * lev) * c_rows]
            eb = e[(3 + 2 * lev) * c_rows:(4 + 2 * lev) * c_rows]
            ql = (q * jnp.exp(ea)).astype(BF16)
            kl = (k * jnp.exp(eb)).astype(BF16)
            att = att + lmask_ref[lev] * _nt(ql, kl)
        vb = v.astype(BF16)
        o = jnp.dot(att.astype(BF16), vb, preferred_element_type=F32)
        o = o + jnp.sum(q * k, axis=-1, keepdims=True) * v
        o = o + _nt(qd, st.astype(BF16))
        b_last = e[c_rows - 1:c_rows, :]
        st_new = st * jnp.exp(b_last) + _tn(vb, kd)
        on = o * lax.rsqrt(jnp.mean(o * o, axis=-1, keepdims=True) + EPS) * gh_ref[...]
        return (on * (rr * jax.nn.sigmoid(rr))).astype(BF16), st_new

    def body(c, carry):
        r0 = pl.multiple_of(c * c_rows, c_rows)
        rows = pl.ds(r0, c_rows)
        a_blk = a_ref[0, rows, :].astype(BF16)
        loaded = [load_head(hh, rows) for hh in range(hps)]
        e_all = range_sums(r0, a_blk)
        results = [one_head(hh, e_all, loaded[hh]) for hh in range(hps)]
        for hh, (out, st_new) in enumerate(results):
            o_ref[0, rows, hh * dv:(hh + 1) * dv] = out
            st_ref[hh] = st_new
        return carry

    lax.fori_loop(0, n_chunks, body, 0)
    for hh in range(hps):
        sout_ref[0, 0, hh] = jnp.transpose(st_ref[hh])


def gla_mix(p, w_alpha_pad, b_alpha, g_head, state0, state_layer, *, chunk, t_valid):
    bsz, t_total, _ = p.shape
    assert t_total % chunk == 0
    cmat, lmask = _gla_constants(chunk)
    n_e = cmat.shape[0]
    dk, dv = GLA_DK, GLA_DV
    hps = GLA_HEADS_PER_STEP
    n_hg = GLA_HEADS // hps
    kblk = lambda off: (lambda b, h: (b, 0, off + h))
    blocks = hps * (t_total * (2 * dk + 2 * dv) * 4 + t_total * dv * 2 + 2 * dk * dv * 4) + t_total * LANE * 4
    mix, s_out = pl.pallas_call(
        functools.partial(_gla_kernel, chunk=chunk, n_chunks=t_total // chunk, t_valid=t_valid, t_total=t_total,
                          hps=hps),
        out_shape=(jax.ShapeDtypeStruct((bsz, t_total, GLA_HEADS * dv), BF16),
                   jax.ShapeDtypeStruct((bsz, 1, GLA_HEADS, dk, dv), F32)),
        grid=(bsz, n_hg),
        in_specs=[
            pl.BlockSpec((1, t_total, hps * dk), kblk(0)),
            pl.BlockSpec((1, t_total, hps * dk), kblk(n_hg)),
            pl.BlockSpec((1, t_total, hps * dv), kblk(n_hg)),
            pl.BlockSpec((1, t_total, hps * dv), kblk(2 * n_hg)),
            pl.BlockSpec((1, t_total, LANE), lambda b, h: (b, 0, GLA_A_COL // LANE)),
            pl.BlockSpec((LANE, hps * dk), lambda b, h: (0, h)),
            pl.BlockSpec((1, hps * dk), lambda b, h: (0, h)),
            pl.BlockSpec((1, dv), lambda b, h: (0, 0)),
            pl.BlockSpec((1, 1, hps, dk, dv), lambda b, h: (b, state_layer, h, 0, 0)),
            pl.BlockSpec((n_e, chunk), lambda b, h: (0, 0)),
            pl.BlockSpec(lmask.shape, lambda b, h: (0, 0, 0)),
        ],
        out_specs=(pl.BlockSpec((1, t_total, hps * dv), lambda b, h: (b, 0, h)),
                   pl.BlockSpec((1, 1, hps, dk, dv), lambda b, h: (b, 0, h, 0, 0))),
        scratch_shapes=[pltpu.VMEM((hps, dv, dk), F32)],
        compiler_params=_params(("parallel", "parallel"),
                                _vmem_limit(blocks, hps * dv * dk * 4, 2 * n_e * 3 * hps * dk * 4 + (4 << 20))),
        name="gla_mix",
    )(p, p, p, p, p, w_alpha_pad, b_alpha.reshape(1, -1), g_head.reshape(1, -1), state0,
      jnp.asarray(cmat, BF16), jnp.asarray(lmask, F32))
    return mix, s_out


GLA_A_COL = 2 * GLA_HEADS * GLA_DK + 2 * GLA_HEADS * GLA_DV
GLA_QX_COL = GLA_A_COL + XA_WIDTH
GLA_COLS = GLA_QX_COL + XA_WIDTH
MOBA_QX_COL = 3 * TOK_WIDTH


def _bucket_thresholds(max_dist):
    n = np.arange(max_dist + 1)
    max_exact = REL_BUCKETS // 2
    nf = np.maximum(n, 1).astype(np.float32)
    large = max_exact + (np.log(nf / np.float32(max_exact)) / np.float32(math.log(REL_MAX_DIST / max_exact))
                         * np.float32(REL_BUCKETS - max_exact)).astype(np.int32)
    large = np.minimum(large, REL_BUCKETS - 1)
    bucket = np.where(n < max_exact, n, large)
    assert np.all(np.diff(bucket) >= 0) and bucket[-1] == REL_BUCKETS - 1
    return [int(np.argmax(bucket >= b)) for b in range(REL_BUCKETS)]


def _moba_prompt_kernel(rb_ref, q_ref, k_ref, v_ref, o_ref, km_ref, vt_ref, bias_ref, *, nb, thr):
    h = pl.program_id(1)
    qi = pl.program_id(2)
    blk = MOBA_BLOCK
    n_sel = max(1, min(MOBA_TOPK, nb - 1))
    tk = lax.broadcasted_iota(jnp.int32, (blk, blk), 0)
    tq = lax.broadcasted_iota(jnp.int32, (blk, blk), 1)

    @pl.when(qi == 0)
    def _():
        for n in range(nb):
            km_ref[n:n + 1, :] = jnp.mean(k_ref[0, n * blk:(n + 1) * blk, :], axis=0, keepdims=True)
            vt_ref[:, n * blk:(n + 1) * blk] = jnp.transpose(v_ref[0, n * blk:(n + 1) * blk, :]).astype(BF16)
        for which in range(2):
            d = tq - tk + which * blk
            bias = jnp.full((blk, blk), rb_ref[h, 0], F32)
            for bkt in range(1, REL_BUCKETS):
                bias = jnp.where(d >= thr[bkt], rb_ref[h, bkt], bias)
            bias_ref[which] = bias

    q = q_ref[0]
    gate = _nt_hi(km_ref[...], q)
    gs = [gate[n:n + 1, :] for n in range(nb)]
    sels = []
    for n in range(nb):
        rank = jnp.zeros((1, blk), jnp.int32)
        for m in range(nb):
            if m != n:
                beats = (gs[m] >= gs[n]) if m < n else (gs[m] > gs[n])
                rank = rank + jnp.where(beats, 1, 0) * (m < qi).astype(jnp.int32)
        sels.append(jnp.where(rank < n_sel, 1, 0) * (n < qi).astype(jnp.int32))
    causal = jnp.where(tq >= tk, 1, 0)

    qb = q.astype(BF16)
    scale = HEAD_DIM ** -0.5

    def attend(n_keys):
        tiles = []
        for j in range(n_keys):
            s = _nt(k_ref[0, j * blk:(j + 1) * blk, :].astype(BF16), qb) * scale
            dj = qi - j
            bias = jnp.where(dj == 0, bias_ref[0], jnp.where(dj == 1, bias_ref[1], rb_ref[h, REL_BUCKETS - 1]))
            valid = jnp.where(dj == 0, causal, jnp.broadcast_to(sels[j], (blk, blk)))
            tiles.append(jnp.where(valid > 0, s + bias, NEG))
        m = jnp.max(tiles[0], axis=0, keepdims=True)
        for s in tiles[1:]:
            m = jnp.maximum(m, jnp.max(s, axis=0, keepdims=True))
        l = jnp.zeros((1, blk), F32)
        acc = jnp.zeros((HEAD_DIM, blk), F32)
        for j, s in enumerate(tiles):
            p = jnp.exp(s - m)
            l = l + jnp.sum(p, axis=0, keepdims=True)
            acc = acc + jnp.dot(vt_ref[:, j * blk:(j + 1) * blk], p.astype(BF16), preferred_element_type=F32)
        o_ref[0] = jnp.transpose(acc / l).astype(BF16)

    half = (nb + 1) // 2
    if half < nb:
        @pl.when(qi < half)
        def _():
            attend(half)

        @pl.when(qi >= half)
        def _():
            attend(nb)
    else:
        attend(nb)


def moba_prompt_mix(p, rel_bias):
    bsz, t_total, _ = p.shape
    blk = MOBA_BLOCK
    assert t_total % blk == 0
    nb = t_total // blk
    thr = _bucket_thresholds(2 * blk)
    assert thr[REL_BUCKETS - 1] <= blk + 1
    hd = HEAD_DIM
    blocks = blk * hd * 4 + 2 * t_total * hd * 4 + blk * hd * 2
    scratch = SUBLANE * hd * 4 + hd * t_total * 2 + 2 * blk * blk * 4
    return pl.pallas_call(
        functools.partial(_moba_prompt_kernel, nb=nb, thr=thr),
        out_shape=jax.ShapeDtypeStruct((bsz, t_total, MOBA_HEADS * hd), BF16),
        grid=(bsz, MOBA_HEADS, nb),
        in_specs=[
            pl.BlockSpec(memory_space=pltpu.SMEM),
            pl.BlockSpec((1, blk, hd), lambda b, h, i: (b, i, h)),
            pl.BlockSpec((1, t_total, hd), lambda b, h, i: (b, 0, MOBA_HEADS + h)),
            pl.BlockSpec((1, t_total, hd), lambda b, h, i: (b, 0, 2 * MOBA_HEADS + h)),
        ],
        out_specs=pl.BlockSpec((1, blk, hd), lambda b, h, i: (b, i, h)),
        scratch_shapes=[
            pltpu.VMEM((nb, hd), F32),
            pltpu.VMEM((hd, t_total), BF16),
            pltpu.VMEM((2, blk, blk), F32),
        ],
        compiler_params=_params(("parallel", "parallel", "arbitrary"), _vmem_limit(blocks, scratch, 2 * nb * blk * blk * 4)),
        name="moba_prompt",
    )(jnp.transpose(rel_bias), p, p, p)


def _moba_sample_kernel(pt_ref, q_ref, kn_ref, vn_ref, *rest, n_blocks, bps):
    del pt_ref
    npg = 2 * bps
    ck_refs, cv_refs = rest[:npg], rest[npg:2 * npg]
    (rbx_ref, o_ref, qbd_ref, ksum_ref, ma_ref, la_ref, oall_ref, blast_ref, bown_ref, kown_ref,
     vown_ref) = rest[2 * npg:]
    step = pl.program_id(1)
    n_steps = n_blocks // bps
    hd = HEAD_DIM
    blk = MOBA_BLOCK
    nrow = LANE
    width = MOBA_HEADS * hd
    scale = hd ** -0.5
    row = lax.broadcasted_iota(jnp.int32, (nrow, LANE), 0)
    lane = lax.broadcasted_iota(jnp.int32, (nrow, LANE), 1)
    t_row = row % SAMPLE_ROWS
    thr = _bucket_thresholds(blk + SAMPLE_ROWS)
    assert thr[REL_BUCKETS - 1] <= blk + 1

    @pl.when(step == 0)
    def _():
        q8 = q_ref[0]
        rowg = lax.broadcasted_iota(jnp.int32, (nrow, width), 0) // SAMPLE_ROWS
        colg = lax.broadcasted_iota(jnp.int32, (nrow, width), 1) // hd
        qbd_ref[...] = jnp.where(rowg == colg, jnp.concatenate([q8] * (nrow // SAMPLE_ROWS), axis=0), 0.0)
        ksum_ref[...] = jnp.zeros(ksum_ref.shape, F32)
        ma_ref[...] = jnp.full(ma_ref.shape, NEG, F32)
        la_ref[...] = jnp.zeros(la_ref.shape, F32)
        dlast = (blk + lax.broadcasted_iota(jnp.int32, (nrow, blk), 0) % SAMPLE_ROWS
                 - lax.broadcasted_iota(jnp.int32, (nrow, blk), 1))
        bias = jnp.broadcast_to(rbx_ref[:, 0:1], (nrow, blk))
        for bkt in range(1, REL_BUCKETS):
            bias = jnp.where(dlast >= thr[bkt], rbx_ref[:, bkt:bkt + 1], bias)
        blast_ref[...] = bias
        down = t_row - lane
        bias = jnp.zeros((nrow, LANE), F32)
        for dd in range(SAMPLE_ROWS):
            bias = jnp.where(down == dd, rbx_ref[:, dd:dd + 1], bias)
        bown_ref[...] = bias
        kown_ref[...] = jnp.zeros(kown_ref.shape, F32)
        vown_ref[...] = jnp.zeros(vown_ref.shape, F32)
        kown_ref[0:SAMPLE_ROWS, :] = kn_ref[0]
        vown_ref[0:SAMPLE_ROWS, :] = vn_ref[0]

    qbd = qbd_ref[...].astype(BF16)
    rowg128 = lax.broadcasted_iota(jnp.int32, (nrow, hd), 0) // SAMPLE_ROWS

    def partial_softmax(kb, vb, bias, valid):
        s = _nt(qbd, kb) * scale + bias
        if valid is not None:
            s = jnp.where(valid, s, NEG)
        m = jnp.max(s, axis=-1, keepdims=True)
        p = jnp.exp(s - m)
        if valid is not None:
            p = jnp.where(valid, p, 0.0)
        l = jnp.sum(p, axis=-1, keepdims=True)
        o_full = jnp.dot(p.astype(BF16), vb, preferred_element_type=F32)
        o = jnp.zeros((nrow, hd), F32)
        for hh in range(MOBA_HEADS):
            o = o + jnp.where(rowg128 == hh, o_full[:, hh * hd:(hh + 1) * hd], 0.0)
        return m, l, o

    def page_rows(ref):
        return jnp.concatenate([ref[hh] for hh in range(MOBA_HEADS)], axis=1)

    far_bias = rbx_ref[:, REL_BUCKETS - 1:REL_BUCKETS]
    ma_new = ma_ref[...]
    la_new = la_ref[...]
    for r in range(bps):
        b_idx = step * bps + r
        kb = jnp.concatenate([page_rows(ck_refs[2 * r]), page_rows(ck_refs[2 * r + 1])], axis=0)
        vb = jnp.concatenate([page_rows(cv_refs[2 * r]), page_rows(cv_refs[2 * r + 1])], axis=0)
        ksum_ref[pl.ds(b_idx, 1), :] = jnp.sum(kb, axis=0, keepdims=True)
        if r == bps - 1:
            bias = jnp.where(step == n_steps - 1, blast_ref[...], jnp.broadcast_to(far_bias, (nrow, blk)))
        else:
            bias = far_bias
        m, l, o = partial_softmax(kb.astype(BF16), vb.astype(BF16), bias, None)
        oall_ref[b_idx] = o
        ma_new = jnp.where(lane == b_idx, m, ma_new)
        la_new = jnp.where(lane == b_idx, l, la_new)
    ma_ref[...] = ma_new
    la_ref[...] = la_new

    @pl.when(step == n_steps - 1)
    def _():
        valid_own = jnp.logical_and(lane <= t_row, lane < SAMPLE_ROWS)
        m_own, l_own, o_own = partial_softmax(kown_ref[...].astype(BF16), vown_ref[...].astype(BF16),
                                              bown_ref[...], valid_own)
        kmean = ksum_ref[...] * (1.0 / MOBA_BLOCK)
        gate = _nt_hi(qbd_ref[...], kmean)
        g = jnp.where(lane < n_blocks, gate, -jnp.inf)
        sel_i = jnp.zeros((nrow, LANE), jnp.int32)
        for _ in range(max(1, min(MOBA_TOPK, n_blocks))):
            gmax = jnp.max(g, axis=-1, keepdims=True)
            first = jnp.min(jnp.where(g == gmax, lane, LANE), axis=-1, keepdims=True)
            pick = lane == first
            sel_i = jnp.where(pick, 1, sel_i)
            g = jnp.where(pick, -jnp.inf, g)
        sel = sel_i > 0
        ma = jnp.where(sel, ma_new, NEG)
        m_all = jnp.maximum(jnp.max(ma, axis=-1, keepdims=True), m_own)
        wa = jnp.where(sel, jnp.exp(ma - m_all), 0.0)
        w_own = jnp.exp(m_own - m_all)
        den = jnp.sum(wa * la_new, axis=-1, keepdims=True) + w_own * l_own
        num = w_own * o_own
        for n in range(n_blocks):
            num = num + wa[:, n:n + 1] * oall_ref[n]
        out = num / den
        o_ref[0] = jnp.concatenate([out[hh * SAMPLE_ROWS:(hh + 1) * SAMPLE_ROWS, :] for hh in range(MOBA_HEADS)],
                                   axis=1).astype(BF16)


def moba_sample_mix(p, cache_k, cache_v, page_table, layer, rel_bias):
    bs, rows, _ = p.shape
    n_pool, n_layers, page, heads, hd = cache_k.shape
    n_pages = page_table.shape[1]
    past_len = n_pages * page
    assert rows == SAMPLE_ROWS and heads == MOBA_HEADS and hd == HEAD_DIM and MOBA_BLOCK == 2 * page
    assert past_len % MOBA_BLOCK == 0 and MOBA_HEADS * SAMPLE_ROWS <= LANE and past_len // MOBA_BLOCK <= LANE
    n_blocks = past_len // MOBA_BLOCK
    bps = MOBA_SAMPLE_BLOCKS_PER_STEP if n_blocks % MOBA_SAMPLE_BLOCKS_PER_STEP == 0 else 1
    npg = 2 * bps
    width = heads * hd
    rbx = jnp.zeros((LANE, LANE), F32).at[:heads * SAMPLE_ROWS, :REL_BUCKETS].set(
        jnp.repeat(jnp.transpose(rel_bias), SAMPLE_ROWS, axis=0))

    def pmap(r):
        return lambda b, g, pt: (pt[b, g * npg + r], layer, 0, 0, 0)

    ck = jnp.transpose(cache_k, (0, 1, 3, 2, 4))
    cv = jnp.transpose(cache_v, (0, 1, 3, 2, 4))
    page_specs = [pl.BlockSpec((None, None, heads, page, hd), pmap(r)) for r in range(npg)]
    blocks = 3 * rows * width * 4 + 2 * npg * page * width * 4 + LANE * LANE * 4 + rows * width * 2
    scratch = (LANE * width * 4 * 4 + 3 * LANE * LANE * 4 + LANE * MOBA_BLOCK * 4 + n_blocks * LANE * hd * 4)
    return pl.pallas_call(
        functools.partial(_moba_sample_kernel, n_blocks=n_blocks, bps=bps),
        out_shape=jax.ShapeDtypeStruct((bs, rows, width), BF16),
        grid_spec=pltpu.PrefetchScalarGridSpec(
            num_scalar_prefetch=1,
            grid=(bs, n_blocks // bps),
            in_specs=[
                pl.BlockSpec((1, rows, width), lambda b, g, pt: (b, 0, 0)),
                pl.BlockSpec((1, rows, width), lambda b, g, pt: (b, 0, 1)),
                pl.BlockSpec((1, rows, width), lambda b, g, pt: (b, 0, 2)),
                *page_specs, *page_specs,
                pl.BlockSpec((LANE, LANE), lambda b, g, pt: (0, 0)),
            ],
            out_specs=pl.BlockSpec((1, rows, width), lambda b, g, pt: (b, 0, 0)),
            scratch_shapes=[
                pltpu.VMEM((LANE, width), F32),
                pltpu.VMEM((LANE, width), F32),
                pltpu.VMEM((LANE, LANE), F32), pltpu.VMEM((LANE, LANE), F32),
                pltpu.VMEM((n_blocks, LANE, hd), F32),
                pltpu.VMEM((LANE, MOBA_BLOCK), F32), pltpu.VMEM((LANE, LANE), F32),
                pltpu.VMEM((LANE, width), F32), pltpu.VMEM((LANE, width), F32),
            ],
        ),
        compiler_params=_params(("parallel", "arbitrary"), _vmem_limit(blocks, scratch, 12 << 20)),
        name="moba_sample",
    )(page_table, p, p, p, *([ck] * npg), *([cv] * npg), rbx)


def _xattn_kernel(q_ref, mk_ref, mv_ref, o_ref, *, stride, k_off, v_off):
    scale = HEAD_DIM ** -0.5
    for h in range(XA_HEADS):
        sl = slice(h * HEAD_DIM, (h + 1) * HEAD_DIM)
        mk = mk_ref[pl.ds(k_off + h, N_MEM, stride=stride), :].astype(BF16)
        mv = mv_ref[pl.ds(v_off + h, N_MEM, stride=stride), :].astype(BF16)
        s = _nt(q_ref[0, :, sl].astype(BF16), mk) * scale
        p = jnp.exp(s - jnp.max(s, axis=-1, keepdims=True))
        o = jnp.dot(p.astype(BF16), mv, preferred_element_type=F32)
        o_ref[0, :, sl] = (o / jnp.sum(p, axis=-1, keepdims=True)).astype(BF16)


def cross_attend(p, qx_col, mk, mv, mem_idx, *, stride, k_off, v_off, tq):
    bsz, t_total, _ = p.shape
    assert t_total % tq == 0 and qx_col % XA_WIDTH == 0
    mem_spec = pl.BlockSpec((None, N_MEM * stride, HEAD_DIM), lambda b, i: (mem_idx(b), 0, 0))
    mk_spec = mv_spec = mem_spec
    blocks = tq * XA_WIDTH * 4 + 2 * N_MEM * stride * HEAD_DIM * 4 + tq * XA_WIDTH * 2
    return pl.pallas_call(
        functools.partial(_xattn_kernel, stride=stride, k_off=k_off, v_off=v_off),
        out_shape=jax.ShapeDtypeStruct((bsz, t_total, XA_WIDTH), BF16),
        grid=(bsz, t_total // tq),
        in_specs=[
            pl.BlockSpec((1, tq, XA_WIDTH), lambda b, i: (b, i, qx_col // XA_WIDTH)),
            mk_spec,
            mv_spec,
        ],
        out_specs=pl.BlockSpec((1, tq, XA_WIDTH), lambda b, i: (b, i, 0)),
        compiler_params=_params(("parallel", "parallel"), _vmem_limit(blocks, 0, 8 * tq * N_MEM * 4)),
        name="cross_attend",
    )(p, mk, mv)


def _out_proj_kernel(mix_ref, xa_ref, w1_ref, w2_ref, res_ref, o_ref):
    o_ref[...] = (res_ref[...] + jnp.dot(mix_ref[...], w1_ref[...], preferred_element_type=F32)
                  + jnp.dot(xa_ref[...], w2_ref[...], preferred_element_type=F32))


def out_proj(mix, xa, w, res, *, tm, tn):
    m, d = res.shape
    assert m % tm == 0 and d % tn == 0 and TOK_WIDTH % XA_WIDTH == 0
    blocks = tm * D_MODEL * 2 + D_MODEL * tn * 2 + 2 * tm * tn * 4
    return pl.pallas_call(
        _out_proj_kernel,
        out_shape=jax.ShapeDtypeStruct((m, d), F32),
        grid=(m // tm, d // tn),
        in_specs=[
            pl.BlockSpec((tm, TOK_WIDTH), lambda i, j: (i, 0)),
            pl.BlockSpec((tm, XA_WIDTH), lambda i, j: (i, 0)),
            pl.BlockSpec((TOK_WIDTH, tn), lambda i, j: (0, j)),
            pl.BlockSpec((XA_WIDTH, tn), lambda i, j: (TOK_WIDTH // XA_WIDTH, j)),
            pl.BlockSpec((tm, tn), lambda i, j: (i, j)),
        ],
        out_specs=pl.BlockSpec((tm, tn), lambda i, j: (i, j)),
        compiler_params=_params(("parallel", "parallel"), _vmem_limit(blocks, 0, tm * tn * 4)),
        name="out_proj",
    )(mix, xa, w, w, res)


def _top_values(s, count):
    vals = []
    for _ in range(count):
        m = jnp.max(s, axis=0, keepdims=True)
        vals.append(m)
        s = jnp.where(s == m, -jnp.inf, s)
    return vals


def _peer_route_kernel(q_ref, sk_ref, e0_ref, cnt_ref, e1_ref, rnk_ref):
    kk = PEER_TOPK
    half = PEER_QDIM // 2
    e0_heads, cnt_heads = [], []
    for h in range(PEER_HEADS):
        s0 = _nt(sk_ref[2 * h], q_ref[:, (2 * h) * half:(2 * h + 1) * half].astype(BF16))
        s1 = _nt(sk_ref[2 * h + 1], q_ref[:, (2 * h + 1) * half:(2 * h + 2) * half].astype(BF16))
        top0 = _top_values(s0, kk)
        top1 = _top_values(s1, kk)
        cands = [top0[a] + top1[b] for a in range(kk) for b in range(kk) if (a + 1) * (b + 1) <= kk]
        n_pad = -len(cands) % SUBLANE
        cmat = jnp.concatenate(cands + [jnp.full_like(cands[0], -jnp.inf)] * n_pad, axis=0)
        rest = cmat
        for _ in range(kk):
            tau = jnp.max(rest, axis=0, keepdims=True)
            rest = jnp.where(rest == tau, -jnp.inf, rest)
        m_tot = top0[0] + top1[0]
        z = jnp.sum(jnp.where(cmat >= tau, jnp.exp(cmat - m_tot), 0.0), axis=0, keepdims=True)
        cnt = jnp.zeros(s0.shape, F32)
        rnk = jnp.zeros(s1.shape, F32)
        for b in range(kk):
            cnt = cnt + jnp.where(s0 + top1[b] >= tau, 1.0, 0.0)
            rnk = rnk + jnp.where(top1[b] > s1, 1.0, 0.0)
        e0_heads.append(jnp.exp(s0 - top0[0]) / z)
        cnt_heads.append(cnt)
        e1_ref[h] = jnp.exp(s1 - top1[0]).astype(BF16)
        rnk_ref[h] = rnk.astype(BF16)
    e0_ref[...] = pltpu.einshape("hit->iht", jnp.stack(e0_heads, axis=0))
    cnt_ref[...] = pltpu.einshape("hit->iht", jnp.stack(cnt_heads, axis=0))


def peer_route(q, subkeys, *, tb):
    m, _ = q.shape
    assert m % tb == 0
    sk = subkeys.reshape(PEER_HEADS * 2, PEER_NKEYS, PEER_QDIM // 2).astype(BF16)
    shp = jax.ShapeDtypeStruct((PEER_NKEYS, PEER_HEADS, m), F32)
    shp16 = jax.ShapeDtypeStruct((PEER_HEADS, PEER_NKEYS, m), BF16)
    ospec = pl.BlockSpec((PEER_NKEYS, PEER_HEADS, tb), lambda i: (0, 0, i))
    ospec16 = pl.BlockSpec((PEER_HEADS, PEER_NKEYS, tb), lambda i: (0, 0, i))
    blocks = tb * PEER_HEADS * PEER_QDIM * 4 + sk.size * 2 + 4 * PEER_HEADS * PEER_NKEYS * tb * 4
    return pl.pallas_call(
        _peer_route_kernel,
        out_shape=(shp, shp, shp16, shp16),
        grid=(m // tb,),
        in_specs=[pl.BlockSpec((tb, PEER_HEADS * PEER_QDIM), lambda i: (i, 0)),
                  pl.BlockSpec(sk.shape, lambda i: (0, 0, 0))],
        out_specs=(ospec, ospec, ospec16, ospec16),
        compiler_params=_params(("parallel",), _vmem_limit(blocks, 0, 16 * PEER_NKEYS * tb * 4)),
        name="peer_route",
    )(q, sk)


def _peer_expert_kernel(xn_ref, u_ref, v_ref, e0_ref, cnt_ref, e1_ref, rnk_ref, res_ref, o_ref, *cast_refs, n_i):
    *cast_refs, act_ref = cast_refs
    et = pl.program_id(1)

    @pl.when(et == 0)
    def _():
        o_ref[...] = res_ref[...]
        act_ref[...] = jnp.zeros(act_ref.shape, BF16)

    act = act_ref[...]
    u = u_ref[...]
    v = v_ref[...]
    if cast_refs:
        u = u.astype(BF16)
        v = v.astype(BF16)
        cast_refs[0][...] = u
        cast_refs[1][...] = v
    h_t = _nt(u, xn_ref[...])
    act_ref[...] = (0.5 * h_t * (1.0 + lax.erf(h_t * (1.0 / math.sqrt(2.0))))).astype(BF16)
    w_rows = []
    for ii in range(n_i):
        g = jnp.zeros((PEER_NKEYS, h_t.shape[1]), BF16)
        for h in range(PEER_HEADS):
            cnt_i = jnp.broadcast_to(cnt_ref[ii, h:h + 1, :], g.shape).astype(BF16)
            e0_i = jnp.broadcast_to(e0_ref[ii, h:h + 1, :], g.shape).astype(BF16)
            g = g + jnp.where(rnk_ref[h] < cnt_i, e1_ref[h], jnp.zeros((), BF16)) * e0_i
        w_rows.append(g * act[ii * PEER_NKEYS:(ii + 1) * PEER_NKEYS])
    w_t = jnp.concatenate(w_rows, axis=0)
    o_ref[...] += _tn(w_t, v)


def peer_experts(xn, u, v, layer, route, res, *, tb, te, emit_cast):
    m, d = xn.shape
    n_exp = u.shape[-2]
    assert m % tb == 0 and n_exp % te == 0 and te % PEER_NKEYS == 0 and n_exp == PEER_NKEYS * PEER_NKEYS
    assert emit_cast == (u.ndim == 3) and (not emit_cast or m == tb)
    n_tiles = n_exp // te
    first = lambda e: jnp.minimum(e, n_tiles - 1)
    second = lambda e: jnp.maximum(e - 1, 0)
    if u.ndim == 3:
        uspec = pl.BlockSpec((None, te, d), lambda i, e: (layer, first(e), 0))
        vspec = pl.BlockSpec((None, te, d), lambda i, e: (layer, second(e), 0))
    else:
        uspec = pl.BlockSpec((te, d), lambda i, e: (first(e), 0))
        vspec = pl.BlockSpec((te, d), lambda i, e: (second(e), 0))
    n_i = te // PEER_NKEYS
    ispec = pl.BlockSpec((n_i, PEER_HEADS, tb), lambda i, e: (second(e), 0, i))
    rspec = pl.BlockSpec((PEER_HEADS, PEER_NKEYS, tb), lambda i, e: (0, 0, i))
    ospec = pl.BlockSpec((tb, d), lambda i, e: (i, 0))
    o_shape = jax.ShapeDtypeStruct((m, d), F32)
    tbytes = u.dtype.itemsize
    blocks = (tb * d * 2 + 2 * te * d * tbytes + 2 * n_i * PEER_HEADS * tb * 4 + 2 * PEER_HEADS * PEER_NKEYS * tb * 2
              + 2 * tb * d * 4 + (2 * te * d * 2 if emit_cast else 0))
    if emit_cast:
        cshape = jax.ShapeDtypeStruct((n_exp, d), BF16)
        out_shape = (o_shape, cshape, cshape)
        out_specs = (ospec, pl.BlockSpec((te, d), lambda i, e: (first(e), 0)),
                     pl.BlockSpec((te, d), lambda i, e: (second(e), 0)))
    else:
        out_shape, out_specs = o_shape, ospec
    return pl.pallas_call(
        functools.partial(_peer_expert_kernel, n_i=n_i),
        out_shape=out_shape,
        grid=(m // tb, n_tiles + 1),
        in_specs=[pl.BlockSpec((tb, d), lambda i, e: (i, 0)), uspec, vspec, ispec, ispec, rspec, rspec,
                  pl.BlockSpec((tb, d), lambda i, e: (i, 0))],
        out_specs=out_specs,
        scratch_shapes=[pltpu.VMEM((te, tb), BF16)],
        compiler_params=_params(("parallel", "arbitrary"), _vmem_limit(blocks, te * tb * 2, 6 * te * tb * 4)),
        name="peer_experts",
    )(xn, u, v, *route, res)


def peer_layer(x, g, w_q, subkeys, u, v, layer, *, tm, tb_route, tb_exp, te, emit_cast):
    q, xn = norm_matmul(x, g, w_q, tm=tm, tn=512, emit_xn=True)
    route = peer_route(q, subkeys, tb=tb_route)
    return peer_experts(xn, u, v, layer, route, x, tb=tb_exp, te=te, emit_cast=emit_cast)


def _pad_gla_weight(w):
    d = w.shape[0]
    n_tok = GLA_A_COL + GLA_RANK
    return jnp.concatenate([w[:, :n_tok], jnp.zeros((d, GLA_QX_COL - n_tok), w.dtype), w[:, n_tok:]],
                           axis=1).astype(BF16)


def kernel(x_prompt, x_sample, state_gla, cache_moba_k, cache_moba_v, cache_mem_k, cache_mem_v, page_table, mem_prompt, ln_mix, ln_mem, ln_ffn, ln_final, w_in_gla, w_alpha_gla, b_alpha_gla, g_head_gla, w_in_moba, rel_bias, w_mem_kv, w_out, w_peer_q, peer_subkeys, peer_u, peer_v):
    bsz, seq, d = x_prompt.shape
    dec_b, dec_t, _ = x_sample.shape
    assert dec_t <= SAMPLE_ROWS
    xp = x_prompt.reshape(bsz * seq, d)
    xs = jnp.pad(x_sample, ((0, 0), (0, SAMPLE_ROWS - dec_t), (0, 0))).reshape(dec_b * SAMPLE_ROWS, d)
    mem2d = mem_prompt.reshape(bsz * N_MEM, d)
    n_dec = dec_b * SAMPLE_ROWS
    cmk = cache_mem_k.reshape(dec_b * DEPTH, N_MEM * XA_HEADS, HEAD_DIM)
    cmv = cache_mem_v.reshape(dec_b * DEPTH, N_MEM * XA_HEADS, HEAD_DIM)

    gla_p, gla_s, kp_l, vp_l, ks_l, vs_l, mkp_l, mvp_l = [], [], [], [], [], [], [], []
    for i in range(DEPTH):
        j = i // N_MIXERS
        kv = norm_matmul(mem2d, ln_mem[i], w_mem_kv[i].astype(BF16), tm=512, tn=512).reshape(bsz, N_MEM, 2 * XA_WIDTH)
        mkp_l.append(kv[..., :XA_WIDTH].reshape(bsz, N_MEM, XA_HEADS, HEAD_DIM))
        mvp_l.append(kv[..., XA_WIDTH:].reshape(bsz, N_MEM, XA_HEADS, HEAD_DIM))
        if i % N_MIXERS == 0:
            w_in = _pad_gla_weight(w_in_gla[j])
            wa = jnp.zeros((LANE, GLA_HEADS * GLA_DK), F32).at[:GLA_RANK].set(w_alpha_gla[j]).astype(BF16)
            pp = norm_matmul(xp, ln_mix[i], w_in, tm=1024, tn=512).reshape(bsz, seq, GLA_COLS)
            ps = norm_matmul(xs, ln_mix[i], w_in, tm=n_dec, tn=512).reshape(dec_b, SAMPLE_ROWS, GLA_COLS)
            zeros0 = jnp.zeros((bsz, 1, GLA_HEADS, GLA_DK, GLA_DV), F32)
            mix_p, s_p = gla_mix(pp, wa, b_alpha_gla[j], g_head_gla[j], zeros0, 0, chunk=GLA_PROMPT_CHUNK, t_valid=seq)
            ps_pad = jnp.pad(ps, ((0, 0), (0, GLA_SAMPLE_CHUNK - SAMPLE_ROWS), (0, 0)))
            mix_s, s_s = gla_mix(ps_pad, wa, b_alpha_gla[j], g_head_gla[j], state_gla, j,
                                 chunk=GLA_SAMPLE_CHUNK, t_valid=dec_t)
            mix_s = mix_s[:, :SAMPLE_ROWS]
            gla_p.append(s_p[:, 0])
            gla_s.append(s_s[:, 0])
            qx_col = GLA_QX_COL
        else:
            w_in = w_in_moba[j].astype(BF16)
            pp = norm_matmul(xp, ln_mix[i], w_in, tm=1024, tn=512).reshape(bsz, seq, -1)
            ps = norm_matmul(xs, ln_mix[i], w_in, tm=n_dec, tn=512).reshape(dec_b, SAMPLE_ROWS, -1)
            mix_p = moba_prompt_mix(pp, rel_bias)
            mix_s = moba_sample_mix(ps, cache_moba_k, cache_moba_v, page_table, j, rel_bias)
            sh = lambda t, n: t.reshape(t.shape[0], n, MOBA_HEADS, HEAD_DIM)
            kp_l.append(sh(pp[..., TOK_WIDTH:2 * TOK_WIDTH], seq))
            vp_l.append(sh(pp[..., 2 * TOK_WIDTH:3 * TOK_WIDTH], seq))
            ks_l.append(sh(ps[:, :dec_t, TOK_WIDTH:2 * TOK_WIDTH], dec_t))
            vs_l.append(sh(ps[:, :dec_t, 2 * TOK_WIDTH:3 * TOK_WIDTH], dec_t))
            qx_col = MOBA_QX_COL
        kv_rows = kv.reshape(bsz, N_MEM * 2 * XA_HEADS, HEAD_DIM)
        xa_p = cross_attend(pp, qx_col, kv_rows, kv_rows, lambda b: b, stride=2 * XA_HEADS, k_off=0, v_off=XA_HEADS,
                            tq=512)
        xa_s = cross_attend(ps, qx_col, cmk, cmv, lambda b, i=i: b * DEPTH + i, stride=XA_HEADS, k_off=0, v_off=0,
                            tq=SAMPLE_ROWS)
        w_o = w_out[i].astype(BF16)
        xp = out_proj(mix_p.reshape(bsz * seq, TOK_WIDTH), xa_p.reshape(bsz * seq, XA_WIDTH), w_o, xp, tm=512, tn=1024)
        xs = out_proj(mix_s.reshape(n_dec, TOK_WIDTH), xa_s.reshape(n_dec, XA_WIDTH), w_o, xs, tm=n_dec, tn=1024)
        w_q = w_peer_q[i].astype(BF16)
        xs, u16, v16 = peer_layer(xs, ln_ffn[i], w_q, peer_subkeys[i], peer_u, peer_v, i, tm=n_dec, tb_route=n_dec,
                                  tb_exp=n_dec, te=256, emit_cast=True)
        xp = peer_layer(xp, ln_ffn[i], w_q, peer_subkeys[i], u16, v16, None, tm=512, tb_route=256,
                        tb_exp=512, te=1024, emit_cast=False)
    y_prompt = rmsnorm_rows(xp, ln_final, tm=512).reshape(bsz, seq, d)
    y_sample = rmsnorm_rows(xs, ln_final, tm=n_dec).reshape(dec_b, SAMPLE_ROWS, d)[:, :dec_t]
    return (y_prompt, y_sample,
            jnp.stack(gla_p, axis=1), jnp.stack(gla_s, axis=1),
            jnp.stack(kp_l, axis=1), jnp.stack(vp_l, axis=1),
            jnp.stack(ks_l, axis=1), jnp.stack(vs_l, axis=1),
            jnp.stack(mkp_l, axis=1), jnp.stack(mvp_l, axis=1))
```

```python
import functools
import math

import numpy as np
import jax
import jax.numpy as jnp
from jax import lax
from jax.experimental import pallas as pl
from jax.experimental.pallas import tpu as pltpu

D_MODEL = 2048
DEPTH = 2
N_MIXERS = 2
HEAD_DIM = 128
N_MEM = 256
XA_HEADS = 4
XA_WIDTH = XA_HEADS * HEAD_DIM
TOK_WIDTH = D_MODEL - XA_WIDTH
GLA_HEADS = 6
GLA_DV = TOK_WIDTH // GLA_HEADS
GLA_DK = GLA_DV // 2
GLA_RANK = 16
GLA_TAU = 16.0
GLA_CHUNK = 64
MOBA_HEADS = TOK_WIDTH // HEAD_DIM
MOBA_BLOCK = 256
MOBA_TOPK = 3
REL_BUCKETS = 32
REL_MAX_DIST = 128
PEER_HEADS = 8
PEER_NKEYS = 128
PEER_QDIM = 256
PEER_TOPK = 16
EPS = 1e-6

F32 = jnp.float32
BF16 = jnp.bfloat16

LANE = 128
SUBLANE = 8
VMEM_CAP_BYTES = 56 * 1024 * 1024

NEG = -1e30
_NT = (((1,), (1,)), ((), ()))
_TN = (((0,), (0,)), ((), ()))

GLA_PROMPT_CHUNK = 128
GLA_SAMPLE_CHUNK = 16
SAMPLE_ROWS = 8
MOBA_SAMPLE_BLOCKS_PER_STEP = 4
GLA_HEADS_PER_STEP = 2


def _nt(a, b):
    return lax.dot_general(a, b, _NT, preferred_element_type=F32)


def _tn(a, b):
    return lax.dot_general(a, b, _TN, preferred_element_type=F32)


def _split3(x):
    x1 = x.astype(BF16)
    r1 = x - x1.astype(F32)
    x2 = r1.astype(BF16)
    x3 = (r1 - x2.astype(F32)).astype(BF16)
    return x1, x2, x3


def _nt_hi(a, b):
    a1 = a.astype(BF16)
    a2 = (a - a1.astype(F32)).astype(BF16)
    b1 = b.astype(BF16)
    b2 = (b - b1.astype(F32)).astype(BF16)
    return _nt(a1, b1) + _nt(a1, b2) + _nt(a2, b1)


def _vmem_limit(block_bytes, scratch_bytes=0, temp_bytes=0):
    return int(min(VMEM_CAP_BYTES, 2 * block_bytes + scratch_bytes + temp_bytes + (4 << 20)))


def _params(sem, vmem):
    return pltpu.CompilerParams(dimension_semantics=sem, vmem_limit_bytes=vmem)


def _norm_matmul_kernel(x_ref, g_ref, w_ref, o_ref, xn_ref):
    @pl.when(pl.program_id(1) == 0)
    def _():
        x = x_ref[...]
        y = x * lax.rsqrt(jnp.mean(x * x, axis=-1, keepdims=True) + EPS)
        xn_ref[...] = (y * g_ref[...]).astype(BF16)

    o_ref[...] = jnp.dot(xn_ref[...], w_ref[...], preferred_element_type=F32)


def norm_matmul(x, g, w, *, tm, tn, emit_xn=False):
    m, d = x.shape
    n = w.shape[1]
    assert m % tm == 0 and n % tn == 0 and w.shape[0] == d
    o_shape = jax.ShapeDtypeStruct((m, n), F32)
    o_spec = pl.BlockSpec((tm, tn), lambda i, j: (i, j))
    xn_bytes = tm * d * 2
    blocks = tm * d * 4 + d * 4 + d * tn * 2 + tm * tn * 4
    kwargs = dict(
        grid=(m // tm, n // tn),
        in_specs=[
            pl.BlockSpec((tm, d), lambda i, j: (i, 0)),
            pl.BlockSpec((1, d), lambda i, j: (0, 0)),
            pl.BlockSpec((d, tn), lambda i, j: (0, j)),
        ],
        name="norm_matmul",
    )
    if emit_xn:
        return pl.pallas_call(
            _norm_matmul_kernel,
            out_shape=(o_shape, jax.ShapeDtypeStruct((m, d), BF16)),
            out_specs=(o_spec, pl.BlockSpec((tm, d), lambda i, j: (i, 0))),
            compiler_params=_params(("parallel", "arbitrary"), _vmem_limit(blocks + xn_bytes, 0, tm * d * 8)),
            **kwargs,
        )(x, g.reshape(1, d), w)
    return pl.pallas_call(
        _norm_matmul_kernel,
        out_shape=o_shape,
        out_specs=o_spec,
        scratch_shapes=[pltpu.VMEM((tm, d), BF16)],
        compiler_params=_params(("parallel", "arbitrary"), _vmem_limit(blocks, xn_bytes, tm * d * 8)),
        **kwargs,
    )(x, g.reshape(1, d), w)


def _rmsnorm_kernel(x_ref, g_ref, o_ref):
    x = x_ref[...]
    o_ref[...] = x * lax.rsqrt(jnp.mean(x * x, axis=-1, keepdims=True) + EPS) * g_ref[...]


def rmsnorm_rows(x, g, *, tm):
    m, d = x.shape
    return pl.pallas_call(
        _rmsnorm_kernel,
        out_shape=jax.ShapeDtypeStruct((m, d), F32),
        grid=(m // tm,),
        in_specs=[pl.BlockSpec((tm, d), lambda i: (i, 0)), pl.BlockSpec((1, d), lambda i: (0, 0))],
        out_specs=pl.BlockSpec((tm, d), lambda i: (i, 0)),
        compiler_params=_params(("parallel",), _vmem_limit(2 * tm * d * 4, 0, tm * d * 4)),
        name="final_rmsnorm",
    )(x, g.reshape(1, d))


def _gla_constants(chunk):
    nlev = int(round(math.log2(chunk)))
    assert 1 << nlev == chunk
    idx = np.arange(chunk)
    col, row = idx[None, :], idx[:, None]
    mats = [col <= row, col > row]
    masks = []
    for lev in range(nlev):
        m = chunk >> (lev + 1)
        grp = idx // (2 * m)
        mid = grp * 2 * m + m
        up = (idx % (2 * m)) >= m
        mats.append((col >= mid[:, None]) & (col <= row) & up[:, None])
        mats.append((col > row) & (col < mid[:, None]) & (~up)[:, None])
        masks.append((grp[:, None] == grp[None, :]) & up[:, None] & (~up)[None, :])
    return (np.concatenate(mats, axis=0).astype(np.float32), np.stack(masks).astype(np.float32))


def _gla_kernel(q_ref, k_ref, v_ref, r_ref, a_ref, wa_ref, ba_ref, gh_ref, s0_ref, cmat_ref, lmask_ref,
                o_ref, sout_ref, st_ref, *, chunk, n_chunks, t_valid, t_total, hps):
    c_rows = chunk
    nlev = int(round(math.log2(chunk)))
    dk, dv = GLA_DK, GLA_DV
    for hh in range(hps):
        st_ref[hh] = jnp.transpose(s0_ref[0, 0, hh])

    def load_head(hh, rows):
        ksl = slice(hh * dk, (hh + 1) * dk)
        vsl = slice(hh * dv, (hh + 1) * dv)
        return (q_ref[0, rows, ksl], k_ref[0, rows, ksl], v_ref[0, rows, vsl], r_ref[0, rows, vsl], st_ref[hh])

    def range_sums(r0, a_blk):
        z = jnp.dot(a_blk, wa_ref[...], preferred_element_type=F32) + ba_ref[...]
        g = (jnp.minimum(z, 0.0) - jnp.log1p(jnp.exp(-jnp.abs(z)))) * (1.0 / GLA_TAU)
        if t_valid < t_total:
            t_idx = r0 + lax.broadcasted_iota(jnp.int32, (c_rows, 1), 0)
            g = jnp.where(t_idx < t_valid, g, 0.0)
        e3 = jnp.dot(cmat_ref[...], jnp.concatenate(_split3(g), axis=1), preferred_element_type=F32)
        w = hps * dk
        return e3[:, 0:w] + e3[:, w:2 * w] + e3[:, 2 * w:3 * w]

    def one_head(hh, e_all, loaded):
        q, k, v, rr, st = loaded
        q = q * (dk ** -0.5)
        e = e_all[:, hh * dk:(hh + 1) * dk]
        qd = (q * jnp.exp(e[0:c_rows])).astype(BF16)
        kd = (k * jnp.exp(e[c_rows:2 * c_rows])).astype(BF16)
        att = jnp.zeros((c_rows, c_rows), F32)
        for lev in range(nlev):
            ea = e[(2 + 2 * lev) * c_rows:(3 + 2 * lev) * c_rows]
            eb = e[(3 + 2 * lev) * c_rows:(4 + 2 * lev) * c_rows]
            ql = (q * jnp.exp(ea)).astype(BF16)
            kl = (k * jnp.exp(eb)).astype(BF16)
            att = att + lmask_ref[lev] * _nt(ql, kl)
        vb = v.astype(BF16)
        o = jnp.dot(att.astype(BF16), vb, preferred_element_type=F32)
        o = o + jnp.sum(q * k, axis=-1, keepdims=True) * v
        o = o + _nt(qd, st.astype(BF16))
        b_last = e[c_rows - 1:c_rows, :]
        st_new = st * jnp.exp(b_last) + _tn(vb, kd)
        on = o * lax.rsqrt(jnp.mean(o * o, axis=-1, keepdims=True) + EPS) * gh_ref[...]
        return (on * (rr * jax.nn.sigmoid(rr))).astype(BF16), st_new

    def body(c, carry):
        r0 = pl.multiple_of(c * c_rows, c_rows)
        rows = pl.ds(r0, c_rows)
        a_blk = a_ref[0, rows, :].astype(BF16)
        loaded = [load_head(hh, rows) for hh in range(hps)]
        e_all = range_sums(r0, a_blk)
        results = [one_head(hh, e_all, loaded[hh]) for hh in range(hps)]
        for hh, (out, st_new) in enumerate(results):
            o_ref[0, rows, hh * dv:(hh + 1) * dv] = out
            st_ref[hh] = st_new
        return carry

    lax.fori_loop(0, n_chunks, body, 0)
    for hh in range(hps):
        sout_ref[0, 0, hh] = jnp.transpose(st_ref[hh])


def gla_mix(p, w_alpha_pad, b_alpha, g_head, state0, state_layer, *, chunk, t_valid):
    bsz, t_total, _ = p.shape
    assert t_total % chunk == 0
    cmat, lmask = _gla_constants(chunk)
    n_e = cmat.shape[0]
    dk, dv = GLA_DK, GLA_DV
    hps = GLA_HEADS_PER_STEP
    n_hg = GLA_HEADS // hps
    kblk = lambda off: (lambda b, h: (b, 0, off + h))
    blocks = hps * (t_total * (2 * dk + 2 * dv) * 4 + t_total * dv * 2 + 2 * dk * dv * 4) + t_total * LANE * 4
    mix, s_out = pl.pallas_call(
        functools.partial(_gla_kernel, chunk=chunk, n_chunks=t_total // chunk, t_valid=t_valid, t_total=t_total,
                          hps=hps),
        out_shape=(jax.ShapeDtypeStruct((bsz, t_total, GLA_HEADS * dv), BF16),
                   jax.ShapeDtypeStruct((bsz, 1, GLA_HEADS, dk, dv), F32)),
        grid=(bsz, n_hg),
        in_specs=[
            pl.BlockSpec((1, t_total, hps * dk), kblk(0)),
            pl.BlockSpec((1, t_total, hps * dk), kblk(n_hg)),
            pl.BlockSpec((1, t_total, hps * dv), kblk(n_hg)),
            pl.BlockSpec((1, t_total, hps * dv), kblk(2 * n_hg)),
            pl.BlockSpec((1, t_total, LANE), lambda b, h: (b, 0, GLA_A_COL // LANE)),
            pl.BlockSpec((LANE, hps * dk), lambda b, h: (0, h)),
            pl.BlockSpec((1, hps * dk), lambda b, h: (0, h)),
            pl.BlockSpec((1, dv), lambda b, h: (0, 0)),
            pl.BlockSpec((1, 1, hps, dk, dv), lambda b, h: (b, state_layer, h, 0, 0)),
            pl.BlockSpec((n_e, chunk), lambda b, h: (0, 0)),
            pl.BlockSpec(lmask.shape, lambda b, h: (0, 0, 0)),
        ],
        out_specs=(pl.BlockSpec((1, t_total, hps * dv), lambda b, h: (b, 0, h)),
                   pl.BlockSpec((1, 1, hps, dk, dv), lambda b, h: (b, 0, h, 0, 0))),
        scratch_shapes=[pltpu.VMEM((hps, dv, dk), F32)],
        compiler_params=_params(("parallel", "parallel"),
                                _vmem_limit(blocks, hps * dv * dk * 4, 2 * n_e * 3 * hps * dk * 4 + (4 << 20))),
        name="gla_mix",
    )(p, p, p, p, p, w_alpha_pad, b_alpha.reshape(1, -1), g_head.reshape(1, -1), state0,
      jnp.asarray(cmat, BF16), jnp.asarray(lmask, F32))
    return mix, s_out


GLA_A_COL = 2 * GLA_HEADS * GLA_DK + 2 * GLA_HEADS * GLA_DV
GLA_QX_COL = GLA_A_COL + XA_WIDTH
GLA_COLS = GLA_QX_COL + XA_WIDTH
MOBA_QX_COL = 3 * TOK_WIDTH


def _bucket_thresholds(max_dist):
    n = np.arange(max_dist + 1)
    max_exact = REL_BUCKETS // 2
    nf = np.maximum(n, 1).astype(np.float32)
    large = max_exact + (np.log(nf / np.float32(max_exact)) / np.float32(math.log(REL_MAX_DIST / max_exact))
                         * np.float32(REL_BUCKETS - max_exact)).astype(np.int32)
    large = np.minimum(large, REL_BUCKETS - 1)
    bucket = np.where(n < max_exact, n, large)
    assert np.all(np.diff(bucket) >= 0) and bucket[-1] == REL_BUCKETS - 1
    return [int(np.argmax(bucket >= b)) for b in range(REL_BUCKETS)]


def _moba_prompt_kernel(rb_ref, q_ref, k_ref, v_ref, o_ref, km_ref, vt_ref, bias_ref, *, nb, thr):
    h = pl.program_id(1)
    qi = pl.program_id(2)
    blk = MOBA_BLOCK
    n_sel = max(1, min(MOBA_TOPK, nb - 1))
    tk = lax.broadcasted_iota(jnp.int32, (blk, blk), 0)
    tq = lax.broadcasted_iota(jnp.int32, (blk, blk), 1)

    @pl.when(qi == 0)
    def _():
        for n in range(nb):
            km_ref[n:n + 1, :] = jnp.mean(k_ref[0, n * blk:(n + 1) * blk, :], axis=0, keepdims=True)
            vt_ref[n] = jnp.transpose(v_ref[0, n * blk:(n + 1) * blk, :]).astype(BF16)
        for which in range(2):
            d = tq - tk + which * blk
            bias = jnp.full((blk, blk), rb_ref[h, 0], F32)
            for bkt in range(1, REL_BUCKETS):
                bias = jnp.where(d >= thr[bkt], rb_ref[h, bkt], bias)
            bias_ref[which] = jnp.where(d >= 0, bias, NEG)

    q = q_ref[0]
    gate = _nt_hi(km_ref[...], q)
    gs = [gate[n:n + 1, :] for n in range(nb)]
    sels = []
    for n in range(nb):
        rank = jnp.zeros((1, blk), jnp.int32)
        for m in range(nb):
            if m != n:
                beats = (gs[m] >= gs[n]) if m < n else (gs[m] > gs[n])
                rank = rank + jnp.where(beats, 1, 0) * (m < qi).astype(jnp.int32)
        sels.append(jnp.where(rank < n_sel, 1, 0) * (n < qi).astype(jnp.int32))

    qb = q.astype(BF16)
    scale = HEAD_DIM ** -0.5
    far_bias = rb_ref[h, REL_BUCKETS - 1]

    def attend(n_tiles):
        tiles, blocks = [], []
        for dd in range(n_tiles):
            j = jnp.maximum(qi - dd, 0)
            kj = k_ref[0, pl.ds(pl.multiple_of(j * blk, blk), blk), :]
            s = _nt(kj.astype(BF16), qb) * scale
            if dd == 0:
                s = s + bias_ref[0]
            else:
                chosen = jnp.zeros((1, blk), jnp.int32)
                for n in range(nb):
                    chosen = chosen + sels[n] * (j == n).astype(jnp.int32)
                chosen = chosen * (dd <= qi).astype(jnp.int32)
                row = jnp.where(chosen > 0, 0.0 if dd == 1 else far_bias, NEG)
                s = (s + bias_ref[1] + row) if dd == 1 else (s + row)
            tiles.append(s)
            blocks.append(j)
        m = jnp.max(tiles[0], axis=0, keepdims=True)
        for s in tiles[1:]:
            m = jnp.maximum(m, jnp.max(s, axis=0, keepdims=True))
        l = jnp.zeros((1, blk), F32)
        acc = jnp.zeros((HEAD_DIM, blk), F32)
        for j, s in zip(blocks, tiles):
            p = jnp.exp(s - m)
            l = l + jnp.sum(p, axis=0, keepdims=True)
            acc = acc + jnp.dot(vt_ref[j], p.astype(BF16), preferred_element_type=F32)
        o_ref[0] = jnp.transpose(acc / l).astype(BF16)

    half = (nb + 1) // 2
    if half < nb:
        @pl.when(qi < half)
        def _():
            attend(half)

        @pl.when(qi >= half)
        def _():
            attend(nb)
    else:
        attend(nb)


def moba_prompt_mix(p, rel_bias):
    bsz, t_total, _ = p.shape
    blk = MOBA_BLOCK
    assert t_total % blk == 0
    nb = t_total // blk
    thr = _bucket_thresholds(2 * blk)
    assert thr[REL_BUCKETS - 1] <= blk + 1
    hd = HEAD_DIM
    blocks = blk * hd * 4 + 2 * t_total * hd * 4 + blk * hd * 2
    scratch = SUBLANE * hd * 4 + hd * t_total * 2 + 2 * blk * blk * 4
    return pl.pallas_call(
        functools.partial(_moba_prompt_kernel, nb=nb, thr=thr),
        out_shape=jax.ShapeDtypeStruct((bsz, t_total, MOBA_HEADS * hd), BF16),
        grid=(bsz, MOBA_HEADS, nb),
        in_specs=[
            pl.BlockSpec(memory_space=pltpu.SMEM),
            pl.BlockSpec((1, blk, hd), lambda b, h, i: (b, i, h)),
            pl.BlockSpec((1, t_total, hd), lambda b, h, i: (b, 0, MOBA_HEADS + h)),
            pl.BlockSpec((1, t_total, hd), lambda b, h, i: (b, 0, 2 * MOBA_HEADS + h)),
        ],
        out_specs=pl.BlockSpec((1, blk, hd), lambda b, h, i: (b, i, h)),
        scratch_shapes=[
            pltpu.VMEM((nb, hd), F32),
            pltpu.VMEM((nb, hd, blk), BF16),
            pltpu.VMEM((2, blk, blk), F32),
        ],
        compiler_params=_params(("parallel", "parallel", "arbitrary"), _vmem_limit(blocks, scratch, 2 * nb * blk * blk * 4)),
        name="moba_prompt",
    )(jnp.transpose(rel_bias), p, p, p)


def _moba_sample_kernel(pt_ref, q_ref, kn_ref, vn_ref, *rest, n_blocks, bps):
    del pt_ref
    npg = 2 * bps
    ck_refs, cv_refs = rest[:npg], rest[npg:2 * npg]
    (rbx_ref, o_ref, qbd_ref, ksum_ref, ma_ref, la_ref, oall_ref, blast_ref, bown_ref, kown_ref,
     vown_ref) = rest[2 * npg:]
    step = pl.program_id(1)
    n_steps = n_blocks // bps
    hd = HEAD_DIM
    blk = MOBA_BLOCK
    nrow = LANE
    width = MOBA_HEADS * hd
    scale = hd ** -0.5
    row = lax.broadcasted_iota(jnp.int32, (nrow, LANE), 0)
    lane = lax.broadcasted_iota(jnp.int32, (nrow, LANE), 1)
    t_row = row % SAMPLE_ROWS
    thr = _bucket_thresholds(blk + SAMPLE_ROWS)
    assert thr[REL_BUCKETS - 1] <= blk + 1

    @pl.when(step == 0)
    def _():
        q8 = q_ref[0]
        rowg = lax.broadcasted_iota(jnp.int32, (nrow, width), 0) // SAMPLE_ROWS
        colg = lax.broadcasted_iota(jnp.int32, (nrow, width), 1) // hd
        qbd_ref[...] = jnp.where(rowg == colg, jnp.concatenate([q8] * (nrow // SAMPLE_ROWS), axis=0), 0.0)
        ksum_ref[...] = jnp.zeros(ksum_ref.shape, F32)
        ma_ref[...] = jnp.full(ma_ref.shape, NEG, F32)
        la_ref[...] = jnp.zeros(la_ref.shape, F32)
        dlast = (blk + lax.broadcasted_iota(jnp.int32, (nrow, blk), 0) % SAMPLE_ROWS
                 - lax.broadcasted_iota(jnp.int32, (nrow, blk), 1))
        bias = jnp.broadcast_to(rbx_ref[:, 0:1], (nrow, blk))
        for bkt in range(1, REL_BUCKETS):
            bias = jnp.where(dlast >= thr[bkt], rbx_ref[:, bkt:bkt + 1], bias)
        blast_ref[...] = bias
        down = t_row - lane
        bias = jnp.zeros((nrow, LANE), F32)
        for dd in range(SAMPLE_ROWS):
            bias = jnp.where(down == dd, rbx_ref[:, dd:dd + 1], bias)
        bown_ref[...] = bias
        kown_ref[...] = jnp.zeros(kown_ref.shape, F32)
        vown_ref[...] = jnp.zeros(vown_ref.shape, F32)
        kown_ref[0:SAMPLE_ROWS, :] = kn_ref[0]
        vown_ref[0:SAMPLE_ROWS, :] = vn_ref[0]

    qbd = qbd_ref[...].astype(BF16)
    rowg128 = lax.broadcasted_iota(jnp.int32, (nrow, hd), 0) // SAMPLE_ROWS

    def partial_softmax(kb, vb, bias, valid):
        s = _nt(qbd, kb) * scale + bias
        if valid is not None:
            s = jnp.where(valid, s, NEG)
        m = jnp.max(s, axis=-1, keepdims=True)
        p = jnp.exp(s - m)
        if valid is not None:
            p = jnp.where(valid, p, 0.0)
        l = jnp.sum(p, axis=-1, keepdims=True)
        o_full = jnp.dot(p.astype(BF16), vb, preferred_element_type=F32)
        o = jnp.zeros((nrow, hd), F32)
        for hh in range(MOBA_HEADS):
            o = o + jnp.where(rowg128 == hh, o_full[:, hh * hd:(hh + 1) * hd], 0.0)
        return m, l, o

    def page_rows(ref):
        return jnp.concatenate([ref[hh] for hh in range(MOBA_HEADS)], axis=1)

    far_bias = rbx_ref[:, REL_BUCKETS - 1:REL_BUCKETS]
    ma_new = ma_ref[...]
    la_new = la_ref[...]
    for r in range(bps):
        b_idx = step * bps + r
        kb = jnp.concatenate([page_rows(ck_refs[2 * r]), page_rows(ck_refs[2 * r + 1])], axis=0)
        vb = jnp.concatenate([page_rows(cv_refs[2 * r]), page_rows(cv_refs[2 * r + 1])], axis=0)
        ksum_ref[pl.ds(b_idx, 1), :] = jnp.sum(kb, axis=0, keepdims=True)
        if r == bps - 1:
            bias = jnp.where(step == n_steps - 1, blast_ref[...], jnp.broadcast_to(far_bias, (nrow, blk)))
        else:
            bias = far_bias
        m, l, o = partial_softmax(kb.astype(BF16), vb.astype(BF16), bias, None)
        oall_ref[b_idx] = o
        ma_new = jnp.where(lane == b_idx, m, ma_new)
        la_new = jnp.where(lane == b_idx, l, la_new)
    ma_ref[...] = ma_new
    la_ref[...] = la_new

    @pl.when(step == n_steps - 1)
    def _():
        valid_own = jnp.logical_and(lane <= t_row, lane < SAMPLE_ROWS)
        m_own, l_own, o_own = partial_softmax(kown_ref[...].astype(BF16), vown_ref[...].astype(BF16),
                                              bown_ref[...], valid_own)
        kmean = ksum_ref[...] * (1.0 / MOBA_BLOCK)
        gate = _nt_hi(qbd_ref[...], kmean)
        g = jnp.where(lane < n_blocks, gate, -jnp.inf)
        sel_i = jnp.zeros((nrow, LANE), jnp.int32)
        for _ in range(max(1, min(MOBA_TOPK, n_blocks))):
            gmax = jnp.max(g, axis=-1, keepdims=True)
            first = jnp.min(jnp.where(g == gmax, lane, LANE), axis=-1, keepdims=True)
            pick = lane == first
            sel_i = jnp.where(pick, 1, sel_i)
            g = jnp.where(pick, -jnp.inf, g)
        sel = sel_i > 0
        ma = jnp.where(sel, ma_new, NEG)
        m_all = jnp.maximum(jnp.max(ma, axis=-1, keepdims=True), m_own)
        wa = jnp.where(sel, jnp.exp(ma - m_all), 0.0)
        w_own = jnp.exp(m_own - m_all)
        den = jnp.sum(wa * la_new, axis=-1, keepdims=True) + w_own * l_own
        num = w_own * o_own
        for n in range(n_blocks):
            num = num + wa[:, n:n + 1] * oall_ref[n]
        out = num / den
        o_ref[0] = jnp.concatenate([out[hh * SAMPLE_ROWS:(hh + 1) * SAMPLE_ROWS, :] for hh in range(MOBA_HEADS)],
                                   axis=1).astype(BF16)


def moba_sample_mix(p, cache_k, cache_v, page_table, layer, rel_bias):
    bs, rows, _ = p.shape
    n_pool, n_layers, page, heads, hd = cache_k.shape
    n_pages = page_table.shape[1]
    past_len = n_pages * page
    assert rows == SAMPLE_ROWS and heads == MOBA_HEADS and hd == HEAD_DIM and MOBA_BLOCK == 2 * page
    assert past_len % MOBA_BLOCK == 0 and MOBA_HEADS * SAMPLE_ROWS <= LANE and past_len // MOBA_BLOCK <= LANE
    n_blocks = past_len // MOBA_BLOCK
    bps = MOBA_SAMPLE_BLOCKS_PER_STEP if n_blocks % MOBA_SAMPLE_BLOCKS_PER_STEP == 0 else 1
    npg = 2 * bps
    width = heads * hd
    rbx = jnp.zeros((LANE, LANE), F32).at[:heads * SAMPLE_ROWS, :REL_BUCKETS].set(
        jnp.repeat(jnp.transpose(rel_bias), SAMPLE_ROWS, axis=0))

    def pmap(r):
        return lambda b, g, pt: (pt[b, g * npg + r], layer, 0, 0, 0)

    ck = jnp.transpose(cache_k, (0, 1, 3, 2, 4))
    cv = jnp.transpose(cache_v, (0, 1, 3, 2, 4))
    page_specs = [pl.BlockSpec((None, None, heads, page, hd), pmap(r)) for r in range(npg)]
    blocks = 3 * rows * width * 4 + 2 * npg * page * width * 4 + LANE * LANE * 4 + rows * width * 2
    scratch = (LANE * width * 4 * 4 + 3 * LANE * LANE * 4 + LANE * MOBA_BLOCK * 4 + n_blocks * LANE * hd * 4)
    return pl.pallas_call(
        functools.partial(_moba_sample_kernel, n_blocks=n_blocks, bps=bps),
        out_shape=jax.ShapeDtypeStruct((bs, rows, width), BF16),
        grid_spec=pltpu.PrefetchScalarGridSpec(
            num_scalar_prefetch=1,
            grid=(bs, n_blocks // bps),
            in_specs=[
                pl.BlockSpec((1, rows, width), lambda b, g, pt: (b, 0, 0)),
                pl.BlockSpec((1, rows, width), lambda b, g, pt: (b, 0, 1)),
                pl.BlockSpec((1, rows, width), lambda b, g, pt: (b, 0, 2)),
                *page_specs, *page_specs,
                pl.BlockSpec((LANE, LANE), lambda b, g, pt: (0, 0)),
            ],
            out_specs=pl.BlockSpec((1, rows, width), lambda b, g, pt: (b, 0, 0)),
            scratch_shapes=[
                pltpu.VMEM((LANE, width), F32),
                pltpu.VMEM((LANE, width), F32),
                pltpu.VMEM((LANE, LANE), F32), pltpu.VMEM((LANE, LANE), F32),
                pltpu.VMEM((n_blocks, LANE, hd), F32),
                pltpu.VMEM((LANE, MOBA_BLOCK), F32), pltpu.VMEM((LANE, LANE), F32),
                pltpu.VMEM((LANE, width), F32), pltpu.VMEM((LANE, width), F32),
            ],
        ),
        compiler_params=_params(("parallel", "arbitrary"), _vmem_limit(blocks, scratch, 12 << 20)),
        name="moba_sample",
    )(page_table, p, p, p, *([ck] * npg), *([cv] * npg), rbx)


def _xattn_kernel(q_ref, mk_ref, mv_ref, o_ref, *, stride, k_off, v_off):
    scale = HEAD_DIM ** -0.5
    for h in range(XA_HEADS):
        sl = slice(h * HEAD_DIM, (h + 1) * HEAD_DIM)
        mk = mk_ref[pl.ds(k_off + h, N_MEM, stride=stride), :].astype(BF16)
        mv = mv_ref[pl.ds(v_off + h, N_MEM, stride=stride), :].astype(BF16)
        s = _nt(q_ref[0, :, sl].astype(BF16), mk) * scale
        p = jnp.exp(s - jnp.max(s, axis=-1, keepdims=True))
        o = jnp.dot(p.astype(BF16), mv, preferred_element_type=F32)
        o_ref[0, :, sl] = (o / jnp.sum(p, axis=-1, keepdims=True)).astype(BF16)


def cross_attend(p, qx_col, mk, mv, mem_idx, *, stride, k_off, v_off, tq):
    bsz, t_total, _ = p.shape
    assert t_total % tq == 0 and qx_col % XA_WIDTH == 0
    mem_spec = pl.BlockSpec((None, N_MEM * stride, HEAD_DIM), lambda b, i: (mem_idx(b), 0, 0))
    mk_spec = mv_spec = mem_spec
    blocks = tq * XA_WIDTH * 4 + 2 * N_MEM * stride * HEAD_DIM * 4 + tq * XA_WIDTH * 2
    return pl.pallas_call(
        functools.partial(_xattn_kernel, stride=stride, k_off=k_off, v_off=v_off),
        out_shape=jax.ShapeDtypeStruct((bsz, t_total, XA_WIDTH), BF16),
        grid=(bsz, t_total // tq),
        in_specs=[
            pl.BlockSpec((1, tq, XA_WIDTH), lambda b, i: (b, i, qx_col // XA_WIDTH)),
            mk_spec,
            mv_spec,
        ],
        out_specs=pl.BlockSpec((1, tq, XA_WIDTH), lambda b, i: (b, i, 0)),
        compiler_params=_params(("parallel", "parallel"), _vmem_limit(blocks, 0, 8 * tq * N_MEM * 4)),
        name="cross_attend",
    )(p, mk, mv)


def _out_proj_kernel(mix_ref, xa_ref, w1_ref, w2_ref, res_ref, o_ref):
    o_ref[...] = (res_ref[...] + jnp.dot(mix_ref[...], w1_ref[...], preferred_element_type=F32)
                  + jnp.dot(xa_ref[...], w2_ref[...], preferred_element_type=F32))


def out_proj(mix, xa, w, res, *, tm, tn):
    m, d = res.shape
    assert m % tm == 0 and d % tn == 0 and TOK_WIDTH % XA_WIDTH == 0
    blocks = tm * D_MODEL * 2 + D_MODEL * tn * 2 + 2 * tm * tn * 4
    return pl.pallas_call(
        _out_proj_kernel,
        out_shape=jax.ShapeDtypeStruct((m, d), F32),
        grid=(m // tm, d // tn),
        in_specs=[
            pl.BlockSpec((tm, TOK_WIDTH), lambda i, j: (i, 0)),
            pl.BlockSpec((tm, XA_WIDTH), lambda i, j: (i, 0)),
            pl.BlockSpec((TOK_WIDTH, tn), lambda i, j: (0, j)),
            pl.BlockSpec((XA_WIDTH, tn), lambda i, j: (TOK_WIDTH // XA_WIDTH, j)),
            pl.BlockSpec((tm, tn), lambda i, j: (i, j)),
        ],
        out_specs=pl.BlockSpec((tm, tn), lambda i, j: (i, j)),
        compiler_params=_params(("parallel", "parallel"), _vmem_limit(blocks, 0, tm * tn * 4)),
        name="out_proj",
    )(mix, xa, w, w, res)


def _top_values(s, count):
    vals = []
    for _ in range(count):
        m = jnp.max(s, axis=0, keepdims=True)
        vals.append(m)
        s = jnp.where(s == m, -jnp.inf, s)
    return vals


def _prefix_count(test, tops):
    n = len(tops)
    assert n & (n - 1) == 0
    bits = []
    count = None
    step = n // 2
    while step >= 1:
        cands = [tops[base + step - 1] for base in range(0, n, 2 * step)]
        for c in reversed(bits):
            cands = [jnp.where(c, cands[2 * i + 1], cands[2 * i]) for i in range(len(cands) // 2)]
        hit = test(cands[0])
        inc = jnp.where(hit, float(step), 0.0)
        count = inc if count is None else count + inc
        bits.append(hit)
        step //= 2
    return count + jnp.where(test(tops[n - 1]), 1.0, 0.0)


def _peer_route_kernel(q_ref, sk_ref, e0_ref, cnt_ref, e1_ref, rnk_ref):
    kk = PEER_TOPK
    half = PEER_QDIM // 2
    e0_heads, cnt_heads = [], []
    for h in range(PEER_HEADS):
        s0 = _nt(sk_ref[2 * h], q_ref[:, (2 * h) * half:(2 * h + 1) * half].astype(BF16))
        s1 = _nt(sk_ref[2 * h + 1], q_ref[:, (2 * h + 1) * half:(2 * h + 2) * half].astype(BF16))
        top0 = _top_values(s0, kk)
        top1 = _top_values(s1, kk)
        cands = [top0[a] + top1[b] for a in range(kk) for b in range(kk) if (a + 1) * (b + 1) <= kk]
        n_pad = -len(cands) % SUBLANE
        cmat = jnp.concatenate(cands + [jnp.full_like(cands[0], -jnp.inf)] * n_pad, axis=0)
        rest = cmat
        for _ in range(kk):
            tau = jnp.max(rest, axis=0, keepdims=True)
            rest = jnp.where(rest == tau, -jnp.inf, rest)
        m_tot = top0[0] + top1[0]
        z = jnp.sum(jnp.where(cmat >= tau, jnp.exp(cmat - m_tot), 0.0), axis=0, keepdims=True)
        cnt = _prefix_count(lambda t: s0 + t >= tau, top1)
        rnk = _prefix_count(lambda t: t > s1, top1)
        e0_heads.append(jnp.exp(s0 - top0[0]) / z)
        cnt_heads.append(cnt)
        e1_ref[h] = jnp.exp(s1 - top1[0]).astype(BF16)
        rnk_ref[h] = rnk.astype(BF16)
    e0_ref[...] = pltpu.einshape("hit->iht", jnp.stack(e0_heads, axis=0))
    cnt_ref[...] = pltpu.einshape("hit->iht", jnp.stack(cnt_heads, axis=0))


def peer_route(q, subkeys, *, tb):
    m, _ = q.shape
    assert m % tb == 0
    sk = subkeys.reshape(PEER_HEADS * 2, PEER_NKEYS, PEER_QDIM // 2).astype(BF16)
    shp = jax.ShapeDtypeStruct((PEER_NKEYS, PEER_HEADS, m), F32)
    shp16 = jax.ShapeDtypeStruct((PEER_HEADS, PEER_NKEYS, m), BF16)
    ospec = pl.BlockSpec((PEER_NKEYS, PEER_HEADS, tb), lambda i: (0, 0, i))
    ospec16 = pl.BlockSpec((PEER_HEADS, PEER_NKEYS, tb), lambda i: (0, 0, i))
    blocks = tb * PEER_HEADS * PEER_QDIM * 4 + sk.size * 2 + 4 * PEER_HEADS * PEER_NKEYS * tb * 4
    return pl.pallas_call(
        _peer_route_kernel,
        out_shape=(shp, shp, shp16, shp16),
        grid=(m // tb,),
        in_specs=[pl.BlockSpec((tb, PEER_HEADS * PEER_QDIM), lambda i: (i, 0)),
                  pl.BlockSpec(sk.shape, lambda i: (0, 0, 0))],
        out_specs=(ospec, ospec, ospec16, ospec16),
        compiler_params=_params(("parallel",), _vmem_limit(blocks, 0, 16 * PEER_NKEYS * tb * 4)),
        name="peer_route",
    )(q, sk)


def _peer_expert_kernel(xn_ref, u_ref, v_ref, e0_ref, cnt_ref, e1_ref, rnk_ref, res_ref, o_ref, *cast_refs, n_i):
    et = pl.program_id(1)

    @pl.when(et == 0)
    def _():
        o_ref[...] = res_ref[...]

    u = u_ref[...]
    v = v_ref[...]
    if cast_refs:
        u = u.astype(BF16)
        v = v.astype(BF16)
        cast_refs[0][...] = u
        cast_refs[1][...] = v
    h_t = _nt(u, xn_ref[...])
    act = (0.5 * h_t * (1.0 + lax.erf(h_t * (1.0 / math.sqrt(2.0))))).astype(BF16)
    w_rows = []
    for ii in range(n_i):
        g = jnp.zeros((PEER_NKEYS, h_t.shape[1]), BF16)
        for h in range(PEER_HEADS):
            cnt_i = jnp.broadcast_to(cnt_ref[ii, h:h + 1, :], g.shape).astype(BF16)
            e0_i = jnp.broadcast_to(e0_ref[ii, h:h + 1, :], g.shape).astype(BF16)
            g = g + jnp.where(rnk_ref[h] < cnt_i, e1_ref[h], jnp.zeros((), BF16)) * e0_i
        w_rows.append(g * act[ii * PEER_NKEYS:(ii + 1) * PEER_NKEYS])
    w_t = jnp.concatenate(w_rows, axis=0)
    o_ref[...] += _tn(w_t, v)


def peer_experts(xn, u, v, layer, route, res, *, tb, te, emit_cast):
    m, d = xn.shape
    n_exp = u.shape[-2]
    assert m % tb == 0 and n_exp % te == 0 and te % PEER_NKEYS == 0 and n_exp == PEER_NKEYS * PEER_NKEYS
    assert emit_cast == (u.ndim == 3) and (not emit_cast or m == tb)
    if u.ndim == 3:
        tspec = pl.BlockSpec((None, te, d), lambda i, e: (layer, e, 0))
    else:
        tspec = pl.BlockSpec((te, d), lambda i, e: (e, 0))
    n_i = te // PEER_NKEYS
    ispec = pl.BlockSpec((n_i, PEER_HEADS, tb), lambda i, e: (e, 0, i))
    rspec = pl.BlockSpec((PEER_HEADS, PEER_NKEYS, tb), lambda i, e: (0, 0, i))
    ospec = pl.BlockSpec((tb, d), lambda i, e: (i, 0))
    o_shape = jax.ShapeDtypeStruct((m, d), F32)
    tbytes = u.dtype.itemsize
    blocks = (tb * d * 2 + 2 * te * d * tbytes + 2 * n_i * PEER_HEADS * tb * 4 + 2 * PEER_HEADS * PEER_NKEYS * tb * 2
              + 2 * tb * d * 4 + (2 * te * d * 2 if emit_cast else 0))
    if emit_cast:
        cshape = jax.ShapeDtypeStruct((n_exp, d), BF16)
        cspec = pl.BlockSpec((te, d), lambda i, e: (e, 0))
        out_shape, out_specs = (o_shape, cshape, cshape), (ospec, cspec, cspec)
    else:
        out_shape, out_specs = o_shape, ospec
    return pl.pallas_call(
        functools.partial(_peer_expert_kernel, n_i=n_i),
        out_shape=out_shape,
        grid=(m // tb, n_exp // te),
        in_specs=[pl.BlockSpec((tb, d), lambda i, e: (i, 0)), tspec, tspec, ispec, ispec, rspec, rspec,
                  pl.BlockSpec((tb, d), lambda i, e: (i, 0))],
        out_specs=out_specs,
        compiler_params=_params(("parallel", "arbitrary"), _vmem_limit(blocks, 0, 6 * te * tb * 4)),
        name="peer_experts",
    )(xn, u, v, *route, res)


def peer_layer(x, g, w_q, subkeys, u, v, layer, *, tm, tb_route, tb_exp, te, emit_cast):
    q, xn = norm_matmul(x, g, w_q, tm=tm, tn=w_q.shape[1], emit_xn=True)
    route = peer_route(q, subkeys, tb=tb_route)
    return peer_experts(xn, u, v, layer, route, x, tb=tb_exp, te=te, emit_cast=emit_cast)


def _pad_gla_weight(w):
    d = w.shape[0]
    n_tok = GLA_A_COL + GLA_RANK
    return jnp.concatenate([w[:, :n_tok], jnp.zeros((d, GLA_QX_COL - n_tok), w.dtype), w[:, n_tok:]],
                           axis=1).astype(BF16)


def kernel(x_prompt, x_sample, state_gla, cache_moba_k, cache_moba_v, cache_mem_k, cache_mem_v, page_table, mem_prompt, ln_mix, ln_mem, ln_ffn, ln_final, w_in_gla, w_alpha_gla, b_alpha_gla, g_head_gla, w_in_moba, rel_bias, w_mem_kv, w_out, w_peer_q, peer_subkeys, peer_u, peer_v):
    bsz, seq, d = x_prompt.shape
    dec_b, dec_t, _ = x_sample.shape
    assert dec_t <= SAMPLE_ROWS
    xp = x_prompt.reshape(bsz * seq, d)
    xs = jnp.pad(x_sample, ((0, 0), (0, SAMPLE_ROWS - dec_t), (0, 0))).reshape(dec_b * SAMPLE_ROWS, d)
    mem2d = mem_prompt.reshape(bsz * N_MEM, d)
    n_dec = dec_b * SAMPLE_ROWS
    cmk = cache_mem_k.reshape(dec_b * DEPTH, N_MEM * XA_HEADS, HEAD_DIM)
    cmv = cache_mem_v.reshape(dec_b * DEPTH, N_MEM * XA_HEADS, HEAD_DIM)

    gla_p, gla_s, kp_l, vp_l, ks_l, vs_l, mkp_l, mvp_l = [], [], [], [], [], [], [], []
    for i in range(DEPTH):
        j = i // N_MIXERS
        kv = norm_matmul(mem2d, ln_mem[i], w_mem_kv[i].astype(BF16), tm=512, tn=512).reshape(bsz, N_MEM, 2 * XA_WIDTH)
        mkp_l.append(kv[..., :XA_WIDTH].reshape(bsz, N_MEM, XA_HEADS, HEAD_DIM))
        mvp_l.append(kv[..., XA_WIDTH:].reshape(bsz, N_MEM, XA_HEADS, HEAD_DIM))
        if i % N_MIXERS == 0:
            w_in = _pad_gla_weight(w_in_gla[j])
            wa = jnp.zeros((LANE, GLA_HEADS * GLA_DK), F32).at[:GLA_RANK].set(w_alpha_gla[j]).astype(BF16)
            pp = norm_matmul(xp, ln_mix[i], w_in, tm=1024, tn=GLA_COLS // 4).reshape(bsz, seq, GLA_COLS)
            ps = norm_matmul(xs, ln_mix[i], w_in, tm=n_dec, tn=512).reshape(dec_b, SAMPLE_ROWS, GLA_COLS)
            zeros0 = jnp.zeros((bsz, 1, GLA_HEADS, GLA_DK, GLA_DV), F32)
            mix_p, s_p = gla_mix(pp, wa, b_alpha_gla[j], g_head_gla[j], zeros0, 0, chunk=GLA_PROMPT_CHUNK, t_valid=seq)
            ps_pad = jnp.pad(ps, ((0, 0), (0, GLA_SAMPLE_CHUNK - SAMPLE_ROWS), (0, 0)))
            mix_s, s_s = gla_mix(ps_pad, wa, b_alpha_gla[j], g_head_gla[j], state_gla, j,
                                 chunk=GLA_SAMPLE_CHUNK, t_valid=dec_t)
            mix_s = mix_s[:, :SAMPLE_ROWS]
            gla_p.append(s_p[:, 0])
            gla_s.append(s_s[:, 0])
            qx_col = GLA_QX_COL
        else:
            w_in = w_in_moba[j].astype(BF16)
            pp = norm_matmul(xp, ln_mix[i], w_in, tm=1024, tn=w_in.shape[1] // 4).reshape(bsz, seq, -1)
            ps = norm_matmul(xs, ln_mix[i], w_in, tm=n_dec, tn=512).reshape(dec_b, SAMPLE_ROWS, -1)
            mix_p = moba_prompt_mix(pp, rel_bias)
            mix_s = moba_sample_mix(ps, cache_moba_k, cache_moba_v, page_table, j, rel_bias)
            sh = lambda t, n: t.reshape(t.shape[0], n, MOBA_HEADS, HEAD_DIM)
            kp_l.append(sh(pp[..., TOK_WIDTH:2 * TOK_WIDTH], seq))
            vp_l.append(sh(pp[..., 2 * TOK_WIDTH:3 * TOK_WIDTH], seq))
            ks_l.append(sh(ps[:, :dec_t, TOK_WIDTH:2 * TOK_WIDTH], dec_t))
            vs_l.append(sh(ps[:, :dec_t, 2 * TOK_WIDTH:3 * TOK_WIDTH], dec_t))
            qx_col = MOBA_QX_COL
        kv_rows = kv.reshape(bsz, N_MEM * 2 * XA_HEADS, HEAD_DIM)
        xa_p = cross_attend(pp, qx_col, kv_rows, kv_rows, lambda b: b, stride=2 * XA_HEADS, k_off=0, v_off=XA_HEADS,
                            tq=512)
        xa_s = cross_attend(ps, qx_col, cmk, cmv, lambda b, i=i: b * DEPTH + i, stride=XA_HEADS, k_off=0, v_off=0,
                            tq=SAMPLE_ROWS)
        w_o = w_out[i].astype(BF16)
        xp = out_proj(mix_p.reshape(bsz * seq, TOK_WIDTH), xa_p.reshape(bsz * seq, XA_WIDTH), w_o, xp, tm=512, tn=1024)
        xs = out_proj(mix_s.reshape(n_dec, TOK_WIDTH), xa_s.reshape(n_dec, XA_WIDTH), w_o, xs, tm=n_dec, tn=1024)
        w_q = w_peer_q[i].astype(BF16)
        xs, u16, v16 = peer_layer(xs, ln_ffn[i], w_q, peer_subkeys[i], peer_u, peer_v, i, tm=n_dec, tb_route=n_dec,
                                  tb_exp=n_dec, te=256, emit_cast=True)
        xp = peer_layer(xp, ln_ffn[i], w_q, peer_subkeys[i], u16, v16, None, tm=512, tb_route=256,
                        tb_exp=512, te=1024, emit_cast=False)
    y_prompt = rmsnorm_rows(xp, ln_final, tm=512).reshape(bsz, seq, d)
    y_sample = rmsnorm_rows(xs, ln_final, tm=n_dec).reshape(dec_b, SAMPLE_ROWS, d)[:, :dec_t]
    return (y_prompt, y_sample,
            jnp.stack(gla_p, axis=1), jnp.stack(gla_s, axis=1),
            jnp.stack(kp_l, axis=1), jnp.stack(vp_l, axis=1),
            jnp.stack(ks_l, axis=1), jnp.stack(vs_l, axis=1),
            jnp.stack(mkp_l, axis=1), jnp.stack(mvp_l, axis=1))
```

```python
import functools
import math

import numpy as np
import jax
import jax.numpy as jnp
from jax import lax
from jax.experimental import pallas as pl
from jax.experimental.pallas import tpu as pltpu

D_MODEL = 2048
DEPTH = 2
N_MIXERS = 2
HEAD_DIM = 128
N_MEM = 256
XA_HEADS = 4
XA_WIDTH = XA_HEADS * HEAD_DIM
TOK_WIDTH = D_MODEL - XA_WIDTH
GLA_HEADS = 6
GLA_DV = TOK_WIDTH // GLA_HEADS
GLA_DK = GLA_DV // 2
GLA_RANK = 16
GLA_TAU = 16.0
GLA_CHUNK = 64
MOBA_HEADS = TOK_WIDTH // HEAD_DIM
MOBA_BLOCK = 256
MOBA_TOPK = 3
REL_BUCKETS = 32
REL_MAX_DIST = 128
PEER_HEADS = 8
PEER_NKEYS = 128
PEER_QDIM = 256
PEER_TOPK = 16
EPS = 1e-6

F32 = jnp.float32
BF16 = jnp.bfloat16

LANE = 128
SUBLANE = 8
VMEM_CAP_BYTES = 56 * 1024 * 1024

NEG = -1e30
_NT = (((1,), (1,)), ((), ()))
_TN = (((0,), (0,)), ((), ()))

GLA_PROMPT_CHUNK = 128
GLA_SAMPLE_CHUNK = 16
SAMPLE_ROWS = 8
MOBA_SAMPLE_BLOCKS_PER_STEP = 4
GLA_HEADS_PER_STEP = 2


def _nt(a, b):
    return lax.dot_general(a, b, _NT, preferred_element_type=F32)


def _tn(a, b):
    return lax.dot_general(a, b, _TN, preferred_element_type=F32)


def _split3(x):
    x1 = x.astype(BF16)
    r1 = x - x1.astype(F32)
    x2 = r1.astype(BF16)
    x3 = (r1 - x2.astype(F32)).astype(BF16)
    return x1, x2, x3


def _nt_hi(a, b):
    a1 = a.astype(BF16)
    a2 = (a - a1.astype(F32)).astype(BF16)
    b1 = b.astype(BF16)
    b2 = (b - b1.astype(F32)).astype(BF16)
    return _nt(a1, b1) + _nt(a1, b2) + _nt(a2, b1)


def _vmem_limit(block_bytes, scratch_bytes=0, temp_bytes=0):
    return int(min(VMEM_CAP_BYTES, 2 * block_bytes + scratch_bytes + temp_bytes + (4 << 20)))


def _params(sem, vmem):
    return pltpu.CompilerParams(dimension_semantics=sem, vmem_limit_bytes=vmem)


def _norm_matmul_kernel(x_ref, g_ref, w_ref, o_ref, xn_ref):
    @pl.when(pl.program_id(1) == 0)
    def _():
        x = x_ref[...]
        y = x * lax.rsqrt(jnp.mean(x * x, axis=-1, keepdims=True) + EPS)
        xn_ref[...] = (y * g_ref[...]).astype(BF16)

    o_ref[...] = jnp.dot(xn_ref[...], w_ref[...], preferred_element_type=F32)


def norm_matmul(x, g, w, *, tm, tn, emit_xn=False):
    m, d = x.shape
    n = w.shape[1]
    assert m % tm == 0 and n % tn == 0 and w.shape[0] == d
    o_shape = jax.ShapeDtypeStruct((m, n), F32)
    o_spec = pl.BlockSpec((tm, tn), lambda i, j: (i, j))
    xn_bytes = tm * d * 2
    blocks = tm * d * 4 + d * 4 + d * tn * 2 + tm * tn * 4
    kwargs = dict(
        grid=(m // tm, n // tn),
        in_specs=[
            pl.BlockSpec((tm, d), lambda i, j: (i, 0)),
            pl.BlockSpec((1, d), lambda i, j: (0, 0)),
            pl.BlockSpec((d, tn), lambda i, j: (0, j)),
        ],
        name="norm_matmul",
    )
    if emit_xn:
        return pl.pallas_call(
            _norm_matmul_kernel,
            out_shape=(o_shape, jax.ShapeDtypeStruct((m, d), BF16)),
            out_specs=(o_spec, pl.BlockSpec((tm, d), lambda i, j: (i, 0))),
            compiler_params=_params(("parallel", "arbitrary"), _vmem_limit(blocks + xn_bytes, 0, tm * d * 8)),
            **kwargs,
        )(x, g.reshape(1, d), w)
    return pl.pallas_call(
        _norm_matmul_kernel,
        out_shape=o_shape,
        out_specs=o_spec,
        scratch_shapes=[pltpu.VMEM((tm, d), BF16)],
        compiler_params=_params(("parallel", "arbitrary"), _vmem_limit(blocks, xn_bytes, tm * d * 8)),
        **kwargs,
    )(x, g.reshape(1, d), w)


def _rmsnorm_kernel(x_ref, g_ref, o_ref):
    x = x_ref[...]
    o_ref[...] = x * lax.rsqrt(jnp.mean(x * x, axis=-1, keepdims=True) + EPS) * g_ref[...]


def rmsnorm_rows(x, g, *, tm):
    m, d = x.shape
    return pl.pallas_call(
        _rmsnorm_kernel,
        out_shape=jax.ShapeDtypeStruct((m, d), F32),
        grid=(m // tm,),
        in_specs=[pl.BlockSpec((tm, d), lambda i: (i, 0)), pl.BlockSpec((1, d), lambda i: (0, 0))],
        out_specs=pl.BlockSpec((tm, d), lambda i: (i, 0)),
        compiler_params=_params(("parallel",), _vmem_limit(2 * tm * d * 4, 0, tm * d * 4)),
        name="final_rmsnorm",
    )(x, g.reshape(1, d))


def _gla_constants(chunk):
    nlev = int(round(math.log2(chunk)))
    assert 1 << nlev == chunk
    idx = np.arange(chunk)
    col, row = idx[None, :], idx[:, None]
    mats = [col <= row, col > row]
    masks = []
    for lev in range(nlev):
        m = chunk >> (lev + 1)
        grp = idx // (2 * m)
        mid = grp * 2 * m + m
        up = (idx % (2 * m)) >= m
        mats.append((col >= mid[:, None]) & (col <= row) & up[:, None])
        mats.append((col > row) & (col < mid[:, None]) & (~up)[:, None])
        masks.append((grp[:, None] == grp[None, :]) & up[:, None] & (~up)[None, :])
    return (np.concatenate(mats, axis=0).astype(np.float32), np.stack(masks).astype(np.float32))


def _gla_kernel(q_ref, k_ref, v_ref, r_ref, a_ref, wa_ref, ba_ref, gh_ref, s0_ref, cmat_ref, lmask_ref,
                o_ref, sout_ref, st_ref, *, chunk, n_chunks, t_valid, t_total, hps):
    c_rows = chunk
    nlev = int(round(math.log2(chunk)))
    dk, dv = GLA_DK, GLA_DV
    for hh in range(hps):
        st_ref[hh] = jnp.transpose(s0_ref[0, 0, hh])

    def load_head(hh, rows):
        ksl = slice(hh * dk, (hh + 1) * dk)
        vsl = slice(hh * dv, (hh + 1) * dv)
        return (q_ref[0, rows, ksl], k_ref[0, rows, ksl], v_ref[0, rows, vsl], r_ref[0, rows, vsl], st_ref[hh])

    def range_sums(r0, a_blk):
        z = jnp.dot(a_blk, wa_ref[...], preferred_element_type=F32) + ba_ref[...]
        g = (jnp.minimum(z, 0.0) - jnp.log1p(jnp.exp(-jnp.abs(z)))) * (1.0 / GLA_TAU)
        if t_valid < t_total:
            t_idx = r0 + lax.broadcasted_iota(jnp.int32, (c_rows, 1), 0)
            g = jnp.where(t_idx < t_valid, g, 0.0)
        e3 = jnp.dot(cmat_ref[...], jnp.concatenate(_split3(g), axis=1), preferred_element_type=F32)
        w = hps * dk
        return e3[:, 0:w] + e3[:, w:2 * w] + e3[:, 2 * w:3 * w]

    def one_head(hh, e_all, loaded):
        q, k, v, rr, st = loaded
        q = q * (dk ** -0.5)
        e = e_all[:, hh * dk:(hh + 1) * dk]
        qd = (q * jnp.exp(e[0:c_rows])).astype(BF16)
        kd = (k * jnp.exp(e[c_rows:2 * c_rows])).astype(BF16)
        att = jnp.zeros((c_rows, c_rows), F32)
        for lev in range(nlev):
            ea = e[(2 + 2 * lev) * c_rows:(3 + 2 * lev) * c_rows]
            eb = e[(3 + 2 * lev) * c_rows:(4 + 2 * lev) * c_rows]
            ql = (q * jnp.exp(ea)).astype(BF16)
            kl = (k * jnp.exp(eb)).astype(BF16)
            att = att + lmask_ref[lev] * _nt(ql, kl)
        vb = v.astype(BF16)
        o = jnp.dot(att.astype(BF16), vb, preferred_element_type=F32)
        o = o + jnp.sum(q * k, axis=-1, keepdims=True) * v
        o = o + _nt(qd, st.astype(BF16))
        b_last = e[c_rows - 1:c_rows, :]
        st_new = st * jnp.exp(b_last) + _tn(vb, kd)
        on = o * lax.rsqrt(jnp.mean(o * o, axis=-1, keepdims=True) + EPS) * gh_ref[...]
        return (on * (rr * jax.nn.sigmoid(rr))).astype(BF16), st_new

    def body(c, carry):
        r0 = pl.multiple_of(c * c_rows, c_rows)
        rows = pl.ds(r0, c_rows)
        a_blk = a_ref[0, rows, :].astype(BF16)
        loaded = [load_head(hh, rows) for hh in range(hps)]
        e_all = range_sums(r0, a_blk)
        results = [one_head(hh, e_all, loaded[hh]) for hh in range(hps)]
        for hh, (out, st_new) in enumerate(results):
            o_ref[0, rows, hh * dv:(hh + 1) * dv] = out
            st_ref[hh] = st_new
        return carry

    lax.fori_loop(0, n_chunks, body, 0)
    for hh in range(hps):
        sout_ref[0, 0, hh] = jnp.transpose(st_ref[hh])


def gla_mix(p, w_alpha_pad, b_alpha, g_head, state0, state_layer, *, chunk, t_valid):
    bsz, t_total, _ = p.shape
    assert t_total % chunk == 0
    cmat, lmask = _gla_constants(chunk)
    n_e = cmat.shape[0]
    dk, dv = GLA_DK, GLA_DV
    hps = GLA_HEADS_PER_STEP
    n_hg = GLA_HEADS // hps
    kblk = lambda off: (lambda b, h: (b, 0, off + h))
    blocks = hps * (t_total * (2 * dk + 2 * dv) * 4 + t_total * dv * 2 + 2 * dk * dv * 4) + t_total * LANE * 4
    mix, s_out = pl.pallas_call(
        functools.partial(_gla_kernel, chunk=chunk, n_chunks=t_total // chunk, t_valid=t_valid, t_total=t_total,
                          hps=hps),
        out_shape=(jax.ShapeDtypeStruct((bsz, t_total, GLA_HEADS * dv), BF16),
                   jax.ShapeDtypeStruct((bsz, 1, GLA_HEADS, dk, dv), F32)),
        grid=(bsz, n_hg),
        in_specs=[
            pl.BlockSpec((1, t_total, hps * dk), kblk(0)),
            pl.BlockSpec((1, t_total, hps * dk), kblk(n_hg)),
            pl.BlockSpec((1, t_total, hps * dv), kblk(n_hg)),
            pl.BlockSpec((1, t_total, hps * dv), kblk(2 * n_hg)),
            pl.BlockSpec((1, t_total, LANE), lambda b, h: (b, 0, GLA_A_COL // LANE)),
            pl.BlockSpec((LANE, hps * dk), lambda b, h: (0, h)),
            pl.BlockSpec((1, hps * dk), lambda b, h: (0, h)),
            pl.BlockSpec((1, dv), lambda b, h: (0, 0)),
            pl.BlockSpec((1, 1, hps, dk, dv), lambda b, h: (b, state_layer, h, 0, 0)),
            pl.BlockSpec((n_e, chunk), lambda b, h: (0, 0)),
            pl.BlockSpec(lmask.shape, lambda b, h: (0, 0, 0)),
        ],
        out_specs=(pl.BlockSpec((1, t_total, hps * dv), lambda b, h: (b, 0, h)),
                   pl.BlockSpec((1, 1, hps, dk, dv), lambda b, h: (b, 0, h, 0, 0))),
        scratch_shapes=[pltpu.VMEM((hps, dv, dk), F32)],
        compiler_params=_params(("parallel", "parallel"),
                                _vmem_limit(blocks, hps * dv * dk * 4, 2 * n_e * 3 * hps * dk * 4 + (4 << 20))),
        name="gla_mix",
    )(p, p, p, p, p, w_alpha_pad, b_alpha.reshape(1, -1), g_head.reshape(1, -1), state0,
      jnp.asarray(cmat, BF16), jnp.asarray(lmask, F32))
    return mix, s_out


GLA_A_COL = 2 * GLA_HEADS * GLA_DK + 2 * GLA_HEADS * GLA_DV
GLA_QX_COL = GLA_A_COL + XA_WIDTH
GLA_COLS = GLA_QX_COL + XA_WIDTH
MOBA_QX_COL = 3 * TOK_WIDTH


def _bucket_thresholds(max_dist):
    n = np.arange(max_dist + 1)
    max_exact = REL_BUCKETS // 2
    nf = np.maximum(n, 1).astype(np.float32)
    large = max_exact + (np.log(nf / np.float32(max_exact)) / np.float32(math.log(REL_MAX_DIST / max_exact))
                         * np.float32(REL_BUCKETS - max_exact)).astype(np.int32)
    large = np.minimum(large, REL_BUCKETS - 1)
    bucket = np.where(n < max_exact, n, large)
    assert np.all(np.diff(bucket) >= 0) and bucket[-1] == REL_BUCKETS - 1
    return [int(np.argmax(bucket >= b)) for b in range(REL_BUCKETS)]


def _moba_prompt_kernel(rb_ref, q_ref, k_ref, v_ref, o_ref, ko_ref, vo_ref, km_ref, vt_ref, bias_ref, *, nb, thr):
    h = pl.program_id(1)
    qi = pl.program_id(2)
    blk = MOBA_BLOCK
    n_sel = max(1, min(MOBA_TOPK, nb - 1))
    tk = lax.broadcasted_iota(jnp.int32, (blk, blk), 0)
    tq = lax.broadcasted_iota(jnp.int32, (blk, blk), 1)

    @pl.when(qi == 0)
    def _():
        ko_ref[0, 0] = k_ref[0]
        vo_ref[0, 0] = v_ref[0]
        for n in range(nb):
            km_ref[n:n + 1, :] = jnp.mean(k_ref[0, n * blk:(n + 1) * blk, :], axis=0, keepdims=True)
            vt_ref[n] = jnp.transpose(v_ref[0, n * blk:(n + 1) * blk, :]).astype(BF16)
        for which in range(2):
            d = tq - tk + which * blk
            bias = jnp.full((blk, blk), rb_ref[h, 0], F32)
            for bkt in range(1, REL_BUCKETS):
                bias = jnp.where(d >= thr[bkt], rb_ref[h, bkt], bias)
            bias_ref[which] = jnp.where(d >= 0, bias, NEG)

    q = q_ref[0]
    gate = _nt_hi(km_ref[...], q)
    gs = [gate[n:n + 1, :] for n in range(nb)]
    sels = []
    for n in range(nb):
        rank = jnp.zeros((1, blk), jnp.int32)
        for m in range(nb):
            if m != n:
                beats = (gs[m] >= gs[n]) if m < n else (gs[m] > gs[n])
                rank = rank + jnp.where(beats, 1, 0) * (m < qi).astype(jnp.int32)
        sels.append(jnp.where(rank < n_sel, 1, 0) * (n < qi).astype(jnp.int32))

    qb = q.astype(BF16)
    scale = HEAD_DIM ** -0.5
    far_bias = rb_ref[h, REL_BUCKETS - 1]

    def attend(n_tiles):
        tiles, blocks = [], []
        for dd in range(n_tiles):
            j = jnp.maximum(qi - dd, 0)
            kj = k_ref[0, pl.ds(pl.multiple_of(j * blk, blk), blk), :]
            s = _nt(kj.astype(BF16), qb) * scale
            if dd == 0:
                s = s + bias_ref[0]
            else:
                chosen = jnp.zeros((1, blk), jnp.int32)
                for n in range(nb):
                    chosen = chosen + sels[n] * (j == n).astype(jnp.int32)
                chosen = chosen * (dd <= qi).astype(jnp.int32)
                row = jnp.where(chosen > 0, 0.0 if dd == 1 else far_bias, NEG)
                s = (s + bias_ref[1] + row) if dd == 1 else (s + row)
            tiles.append(s)
            blocks.append(j)
        m = jnp.max(tiles[0], axis=0, keepdims=True)
        for s in tiles[1:]:
            m = jnp.maximum(m, jnp.max(s, axis=0, keepdims=True))
        l = jnp.zeros((1, blk), F32)
        acc = jnp.zeros((HEAD_DIM, blk), F32)
        for j, s in zip(blocks, tiles):
            p = jnp.exp(s - m)
            l = l + jnp.sum(p, axis=0, keepdims=True)
            acc = acc + jnp.dot(vt_ref[j], p.astype(BF16), preferred_element_type=F32)
        o_ref[0] = jnp.transpose(acc / l).astype(BF16)

    half = (nb + 1) // 2
    if half < nb:
        @pl.when(qi < half)
        def _():
            attend(half)

        @pl.when(qi >= half)
        def _():
            attend(nb)
    else:
        attend(nb)


def moba_prompt_mix(p, rel_bias):
    bsz, t_total, _ = p.shape
    blk = MOBA_BLOCK
    assert t_total % blk == 0
    nb = t_total // blk
    thr = _bucket_thresholds(2 * blk)
    assert thr[REL_BUCKETS - 1] <= blk + 1
    hd = HEAD_DIM
    blocks = blk * hd * 4 + 4 * t_total * hd * 4 + blk * hd * 2
    scratch = SUBLANE * hd * 4 + hd * t_total * 2 + 2 * blk * blk * 4
    kv_shape = jax.ShapeDtypeStruct((bsz, MOBA_HEADS, t_total, hd), F32)
    kv_spec = pl.BlockSpec((1, 1, t_total, hd), lambda b, h, i: (b, h, 0, 0))
    return pl.pallas_call(
        functools.partial(_moba_prompt_kernel, nb=nb, thr=thr),
        out_shape=(jax.ShapeDtypeStruct((bsz, t_total, MOBA_HEADS * hd), BF16), kv_shape, kv_shape),
        grid=(bsz, MOBA_HEADS, nb),
        in_specs=[
            pl.BlockSpec(memory_space=pltpu.SMEM),
            pl.BlockSpec((1, blk, hd), lambda b, h, i: (b, i, h)),
            pl.BlockSpec((1, t_total, hd), lambda b, h, i: (b, 0, MOBA_HEADS + h)),
            pl.BlockSpec((1, t_total, hd), lambda b, h, i: (b, 0, 2 * MOBA_HEADS + h)),
        ],
        out_specs=(pl.BlockSpec((1, blk, hd), lambda b, h, i: (b, i, h)), kv_spec, kv_spec),
        scratch_shapes=[
            pltpu.VMEM((nb, hd), F32),
            pltpu.VMEM((nb, hd, blk), BF16),
            pltpu.VMEM((2, blk, blk), F32),
        ],
        compiler_params=_params(("parallel", "parallel", "arbitrary"), _vmem_limit(blocks, scratch, 2 * nb * blk * blk * 4)),
        name="moba_prompt",
    )(jnp.transpose(rel_bias), p, p, p)


def _moba_sample_kernel(pt_ref, q_ref, kn_ref, vn_ref, *rest, n_blocks, bps):
    del pt_ref
    npg = 2 * bps
    ck_refs, cv_refs = rest[:npg], rest[npg:2 * npg]
    (rbx_ref, o_ref, qbd_ref, ksum_ref, ma_ref, la_ref, oall_ref, blast_ref, bown_ref, kown_ref,
     vown_ref) = rest[2 * npg:]
    step = pl.program_id(1)
    n_steps = n_blocks // bps
    hd = HEAD_DIM
    blk = MOBA_BLOCK
    nrow = LANE
    width = MOBA_HEADS * hd
    scale = hd ** -0.5
    row = lax.broadcasted_iota(jnp.int32, (nrow, LANE), 0)
    lane = lax.broadcasted_iota(jnp.int32, (nrow, LANE), 1)
    t_row = row % SAMPLE_ROWS
    thr = _bucket_thresholds(blk + SAMPLE_ROWS)
    assert thr[REL_BUCKETS - 1] <= blk + 1

    @pl.when(step == 0)
    def _():
        q8 = q_ref[0]
        rowg = lax.broadcasted_iota(jnp.int32, (nrow, width), 0) // SAMPLE_ROWS
        colg = lax.broadcasted_iota(jnp.int32, (nrow, width), 1) // hd
        qbd_ref[...] = jnp.where(rowg == colg, jnp.concatenate([q8] * (nrow // SAMPLE_ROWS), axis=0), 0.0)
        ksum_ref[...] = jnp.zeros(ksum_ref.shape, F32)
        ma_ref[...] = jnp.full(ma_ref.shape, NEG, F32)
        la_ref[...] = jnp.zeros(la_ref.shape, F32)
        dlast = (blk + lax.broadcasted_iota(jnp.int32, (nrow, blk), 0) % SAMPLE_ROWS
                 - lax.broadcasted_iota(jnp.int32, (nrow, blk), 1))
        bias = jnp.broadcast_to(rbx_ref[:, 0:1], (nrow, blk))
        for bkt in range(1, REL_BUCKETS):
            bias = jnp.where(dlast >= thr[bkt], rbx_ref[:, bkt:bkt + 1], bias)
        blast_ref[...] = bias
        down = t_row - lane
        bias = jnp.zeros((nrow, LANE), F32)
        for dd in range(SAMPLE_ROWS):
            bias = jnp.where(down == dd, rbx_ref[:, dd:dd + 1], bias)
        bown_ref[...] = bias
        kown_ref[...] = jnp.zeros(kown_ref.shape, F32)
        vown_ref[...] = jnp.zeros(vown_ref.shape, F32)
        kown_ref[0:SAMPLE_ROWS, :] = kn_ref[0]
        vown_ref[0:SAMPLE_ROWS, :] = vn_ref[0]

    qbd = qbd_ref[...].astype(BF16)
    rowg128 = lax.broadcasted_iota(jnp.int32, (nrow, hd), 0) // SAMPLE_ROWS

    def partial_softmax(kb, vb, bias, valid):
        s = _nt(qbd, kb) * scale + bias
        if valid is not None:
            s = jnp.where(valid, s, NEG)
        m = jnp.max(s, axis=-1, keepdims=True)
        p = jnp.exp(s - m)
        if valid is not None:
            p = jnp.where(valid, p, 0.0)
        l = jnp.sum(p, axis=-1, keepdims=True)
        o_full = jnp.dot(p.astype(BF16), vb, preferred_element_type=F32)
        o = jnp.zeros((nrow, hd), F32)
        for hh in range(MOBA_HEADS):
            o = o + jnp.where(rowg128 == hh, o_full[:, hh * hd:(hh + 1) * hd], 0.0)
        return m, l, o

    def page_rows(ref):
        return jnp.concatenate([ref[hh] for hh in range(MOBA_HEADS)], axis=1)

    far_bias = rbx_ref[:, REL_BUCKETS - 1:REL_BUCKETS]
    ma_new = ma_ref[...]
    la_new = la_ref[...]
    for r in range(bps):
        b_idx = step * bps + r
        kb = jnp.concatenate([page_rows(ck_refs[2 * r]), page_rows(ck_refs[2 * r + 1])], axis=0)
        vb = jnp.concatenate([page_rows(cv_refs[2 * r]), page_rows(cv_refs[2 * r + 1])], axis=0)
        ksum_ref[pl.ds(b_idx, 1), :] = jnp.sum(kb, axis=0, keepdims=True)
        if r == bps - 1:
            bias = jnp.where(step == n_steps - 1, blast_ref[...], jnp.broadcast_to(far_bias, (nrow, blk)))
        else:
            bias = far_bias
        m, l, o = partial_softmax(kb.astype(BF16), vb.astype(BF16), bias, None)
        oall_ref[b_idx] = o
        ma_new = jnp.where(lane == b_idx, m, ma_new)
        la_new = jnp.where(lane == b_idx, l, la_new)
    ma_ref[...] = ma_new
    la_ref[...] = la_new

    @pl.when(step == n_steps - 1)
    def _():
        valid_own = jnp.logical_and(lane <= t_row, lane < SAMPLE_ROWS)
        m_own, l_own, o_own = partial_softmax(kown_ref[...].astype(BF16), vown_ref[...].astype(BF16),
                                              bown_ref[...], valid_own)
        kmean = ksum_ref[...] * (1.0 / MOBA_BLOCK)
        gate = _nt_hi(qbd_ref[...], kmean)
        g = jnp.where(lane < n_blocks, gate, -jnp.inf)
        sel_i = jnp.zeros((nrow, LANE), jnp.int32)
        for _ in range(max(1, min(MOBA_TOPK, n_blocks))):
            gmax = jnp.max(g, axis=-1, keepdims=True)
            first = jnp.min(jnp.where(g == gmax, lane, LANE), axis=-1, keepdims=True)
            pick = lane == first
            sel_i = jnp.where(pick, 1, sel_i)
            g = jnp.where(pick, -jnp.inf, g)
        sel = sel_i > 0
        ma = jnp.where(sel, ma_new, NEG)
        m_all = jnp.maximum(jnp.max(ma, axis=-1, keepdims=True), m_own)
        wa = jnp.where(sel, jnp.exp(ma - m_all), 0.0)
        w_own = jnp.exp(m_own - m_all)
        den = jnp.sum(wa * la_new, axis=-1, keepdims=True) + w_own * l_own
        num = w_own * o_own
        for n in range(n_blocks):
            num = num + wa[:, n:n + 1] * oall_ref[n]
        out = num / den
        o_ref[0] = jnp.concatenate([out[hh * SAMPLE_ROWS:(hh + 1) * SAMPLE_ROWS, :] for hh in range(MOBA_HEADS)],
                                   axis=1).astype(BF16)


def moba_sample_mix(p, cache_k, cache_v, page_table, layer, rel_bias):
    bs, rows, _ = p.shape
    n_pool, n_layers, page, heads, hd = cache_k.shape
    n_pages = page_table.shape[1]
    past_len = n_pages * page
    assert rows == SAMPLE_ROWS and heads == MOBA_HEADS and hd == HEAD_DIM and MOBA_BLOCK == 2 * page
    assert past_len % MOBA_BLOCK == 0 and MOBA_HEADS * SAMPLE_ROWS <= LANE and past_len // MOBA_BLOCK <= LANE
    n_blocks = past_len // MOBA_BLOCK
    bps = MOBA_SAMPLE_BLOCKS_PER_STEP if n_blocks % MOBA_SAMPLE_BLOCKS_PER_STEP == 0 else 1
    npg = 2 * bps
    width = heads * hd
    rbx = jnp.zeros((LANE, LANE), F32).at[:heads * SAMPLE_ROWS, :REL_BUCKETS].set(
        jnp.repeat(jnp.transpose(rel_bias), SAMPLE_ROWS, axis=0))

    def pmap(r):
        return lambda b, g, pt: (pt[b, g * npg + r], layer, 0, 0, 0)

    ck = jnp.transpose(cache_k, (0, 1, 3, 2, 4))
    cv = jnp.transpose(cache_v, (0, 1, 3, 2, 4))
    page_specs = [pl.BlockSpec((None, None, heads, page, hd), pmap(r)) for r in range(npg)]
    blocks = 3 * rows * width * 4 + 2 * npg * page * width * 4 + LANE * LANE * 4 + rows * width * 2
    scratch = (LANE * width * 4 * 4 + 3 * LANE * LANE * 4 + LANE * MOBA_BLOCK * 4 + n_blocks * LANE * hd * 4)
    return pl.pallas_call(
        functools.partial(_moba_sample_kernel, n_blocks=n_blocks, bps=bps),
        out_shape=jax.ShapeDtypeStruct((bs, rows, width), BF16),
        grid_spec=pltpu.PrefetchScalarGridSpec(
            num_scalar_prefetch=1,
            grid=(bs, n_blocks // bps),
            in_specs=[
                pl.BlockSpec((1, rows, width), lambda b, g, pt: (b, 0, 0)),
                pl.BlockSpec((1, rows, width), lambda b, g, pt: (b, 0, 1)),
                pl.BlockSpec((1, rows, width), lambda b, g, pt: (b, 0, 2)),
                *page_specs, *page_specs,
                pl.BlockSpec((LANE, LANE), lambda b, g, pt: (0, 0)),
            ],
            out_specs=pl.BlockSpec((1, rows, width), lambda b, g, pt: (b, 0, 0)),
            scratch_shapes=[
                pltpu.VMEM((LANE, width), F32),
                pltpu.VMEM((LANE, width), F32),
                pltpu.VMEM((LANE, LANE), F32), pltpu.VMEM((LANE, LANE), F32),
                pltpu.VMEM((n_blocks, LANE, hd), F32),
                pltpu.VMEM((LANE, MOBA_BLOCK), F32), pltpu.VMEM((LANE, LANE), F32),
                pltpu.VMEM((LANE, width), F32), pltpu.VMEM((LANE, width), F32),
            ],
        ),
        compiler_params=_params(("parallel", "arbitrary"), _vmem_limit(blocks, scratch, 12 << 20)),
        name="moba_sample",
    )(page_table, p, p, p, *([ck] * npg), *([cv] * npg), rbx)


def _xattn_kernel(q_ref, mk_ref, mv_ref, o_ref, *, stride, k_off, v_off):
    scale = HEAD_DIM ** -0.5
    for h in range(XA_HEADS):
        sl = slice(h * HEAD_DIM, (h + 1) * HEAD_DIM)
        mk = mk_ref[pl.ds(k_off + h, N_MEM, stride=stride), :].astype(BF16)
        mv = mv_ref[pl.ds(v_off + h, N_MEM, stride=stride), :].astype(BF16)
        s = _nt(q_ref[0, :, sl].astype(BF16), mk) * scale
        p = jnp.exp(s - jnp.max(s, axis=-1, keepdims=True))
        o = jnp.dot(p.astype(BF16), mv, preferred_element_type=F32)
        o_ref[0, :, sl] = (o / jnp.sum(p, axis=-1, keepdims=True)).astype(BF16)


def cross_attend(p, qx_col, mk, mv, mem_idx, *, stride, k_off, v_off, tq):
    bsz, t_total, _ = p.shape
    assert t_total % tq == 0 and qx_col % XA_WIDTH == 0
    mem_spec = pl.BlockSpec((None, N_MEM * stride, HEAD_DIM), lambda b, i: (mem_idx(b), 0, 0))
    mk_spec = mv_spec = mem_spec
    blocks = tq * XA_WIDTH * 4 + 2 * N_MEM * stride * HEAD_DIM * 4 + tq * XA_WIDTH * 2
    return pl.pallas_call(
        functools.partial(_xattn_kernel, stride=stride, k_off=k_off, v_off=v_off),
        out_shape=jax.ShapeDtypeStruct((bsz, t_total, XA_WIDTH), BF16),
        grid=(bsz, t_total // tq),
        in_specs=[
            pl.BlockSpec((1, tq, XA_WIDTH), lambda b, i: (b, i, qx_col // XA_WIDTH)),
            mk_spec,
            mv_spec,
        ],
        out_specs=pl.BlockSpec((1, tq, XA_WIDTH), lambda b, i: (b, i, 0)),
        compiler_params=_params(("parallel", "parallel"), _vmem_limit(blocks, 0, 8 * tq * N_MEM * 4)),
        name="cross_attend",
    )(p, mk, mv)


def _out_proj_kernel(mix_ref, xa_ref, w1_ref, w2_ref, res_ref, o_ref):
    o_ref[...] = (res_ref[...] + jnp.dot(mix_ref[...], w1_ref[...], preferred_element_type=F32)
                  + jnp.dot(xa_ref[...], w2_ref[...], preferred_element_type=F32))


def out_proj(mix, xa, w, res, *, tm, tn):
    m, d = res.shape
    assert m % tm == 0 and d % tn == 0 and TOK_WIDTH % XA_WIDTH == 0
    blocks = tm * D_MODEL * 2 + D_MODEL * tn * 2 + 2 * tm * tn * 4
    return pl.pallas_call(
        _out_proj_kernel,
        out_shape=jax.ShapeDtypeStruct((m, d), F32),
        grid=(m // tm, d // tn),
        in_specs=[
            pl.BlockSpec((tm, TOK_WIDTH), lambda i, j: (i, 0)),
            pl.BlockSpec((tm, XA_WIDTH), lambda i, j: (i, 0)),
            pl.BlockSpec((TOK_WIDTH, tn), lambda i, j: (0, j)),
            pl.BlockSpec((XA_WIDTH, tn), lambda i, j: (TOK_WIDTH // XA_WIDTH, j)),
            pl.BlockSpec((tm, tn), lambda i, j: (i, j)),
        ],
        out_specs=pl.BlockSpec((tm, tn), lambda i, j: (i, j)),
        compiler_params=_params(("parallel", "parallel"), _vmem_limit(blocks, 0, tm * tn * 4)),
        name="out_proj",
    )(mix, xa, w, w, res)


def _compare_exchange(a, i, l, descending):
    hi, lo = jnp.maximum(a[i], a[l]), jnp.minimum(a[i], a[l])
    a[i], a[l] = (hi, lo) if descending else (lo, hi)


def _bitonic_merge_desc(a):
    n = len(a)
    j = n // 2
    while j >= 1:
        for i in range(n):
            if i ^ j > i:
                _compare_exchange(a, i, i ^ j, True)
        j //= 2
    return a


def _top_values(s, count):
    assert s.shape[0] == SUBLANE * count and count & (count - 1) == 0
    a = [s[SUBLANE * v:SUBLANE * (v + 1), :] for v in range(count)]
    k = 2
    while k <= count:
        j = k // 2
        while j >= 1:
            for i in range(count):
                if i ^ j > i:
                    _compare_exchange(a, i, i ^ j, (i & k) == 0)
            j //= 2
        k *= 2
    shift = SUBLANE // 2
    while shift >= 1:
        other = [pltpu.roll(x, shift, 0) for x in a]
        a = _bitonic_merge_desc([jnp.maximum(a[i], other[count - 1 - i]) for i in range(count)])
        shift //= 2
    return [x[0:1, :] for x in a]


def _prefix_count(test, tops):
    n = len(tops)
    assert n & (n - 1) == 0
    bits = []
    count = None
    step = n // 2
    while step >= 1:
        cands = [tops[base + step - 1] for base in range(0, n, 2 * step)]
        for c in reversed(bits):
            cands = [jnp.where(c, cands[2 * i + 1], cands[2 * i]) for i in range(len(cands) // 2)]
        hit = test(cands[0])
        inc = jnp.where(hit, float(step), 0.0)
        count = inc if count is None else count + inc
        bits.append(hit)
        step //= 2
    return count + jnp.where(test(tops[n - 1]), 1.0, 0.0)


def _peer_route_kernel(q_ref, sk_ref, e0_ref, cnt_ref, e1_ref, rnk_ref):
    kk = PEER_TOPK
    half = PEER_QDIM // 2
    e0_heads, cnt_heads = [], []
    for h in range(PEER_HEADS):
        s0 = _nt(sk_ref[2 * h], q_ref[:, (2 * h) * half:(2 * h + 1) * half].astype(BF16))
        s1 = _nt(sk_ref[2 * h + 1], q_ref[:, (2 * h + 1) * half:(2 * h + 2) * half].astype(BF16))
        top0 = _top_values(s0, kk)
        top1 = _top_values(s1, kk)
        cands = [top0[a] + top1[b] for a in range(kk) for b in range(kk) if (a + 1) * (b + 1) <= kk]
        n_pad = -len(cands) % SUBLANE
        cmat = jnp.concatenate(cands + [jnp.full_like(cands[0], -jnp.inf)] * n_pad, axis=0)
        rest = cmat
        for _ in range(kk):
            tau = jnp.max(rest, axis=0, keepdims=True)
            rest = jnp.where(rest == tau, -jnp.inf, rest)
        m_tot = top0[0] + top1[0]
        z = jnp.sum(jnp.where(cmat >= tau, jnp.exp(cmat - m_tot), 0.0), axis=0, keepdims=True)
        cnt = _prefix_count(lambda t: s0 + t >= tau, top1)
        rnk = _prefix_count(lambda t: t > s1, top1)
        e0_heads.append(jnp.exp(s0 - top0[0]) / z)
        cnt_heads.append(cnt)
        e1_ref[h] = jnp.exp(s1 - top1[0]).astype(BF16)
        rnk_ref[h] = rnk.astype(BF16)
    e0_ref[...] = pltpu.einshape("hit->iht", jnp.stack(e0_heads, axis=0))
    cnt_ref[...] = pltpu.einshape("hit->iht", jnp.stack(cnt_heads, axis=0))


def peer_route(q, subkeys, *, tb):
    m, _ = q.shape
    assert m % tb == 0
    sk = subkeys.reshape(PEER_HEADS * 2, PEER_NKEYS, PEER_QDIM // 2).astype(BF16)
    shp = jax.ShapeDtypeStruct((PEER_NKEYS, PEER_HEADS, m), F32)
    shp16 = jax.ShapeDtypeStruct((PEER_HEADS, PEER_NKEYS, m), BF16)
    ospec = pl.BlockSpec((PEER_NKEYS, PEER_HEADS, tb), lambda i: (0, 0, i))
    ospec16 = pl.BlockSpec((PEER_HEADS, PEER_NKEYS, tb), lambda i: (0, 0, i))
    blocks = tb * PEER_HEADS * PEER_QDIM * 4 + sk.size * 2 + 4 * PEER_HEADS * PEER_NKEYS * tb * 4
    return pl.pallas_call(
        _peer_route_kernel,
        out_shape=(shp, shp, shp16, shp16),
        grid=(m // tb,),
        in_specs=[pl.BlockSpec((tb, PEER_HEADS * PEER_QDIM), lambda i: (i, 0)),
                  pl.BlockSpec(sk.shape, lambda i: (0, 0, 0))],
        out_specs=(ospec, ospec, ospec16, ospec16),
        compiler_params=_params(("parallel",), _vmem_limit(blocks, 0, 16 * PEER_NKEYS * tb * 4)),
        name="peer_route",
    )(q, sk)


def _peer_expert_kernel(xn_ref, u_ref, v_ref, e0_ref, cnt_ref, e1_ref, rnk_ref, res_ref, o_ref, *cast_refs, n_i):
    et = pl.program_id(1)

    @pl.when(et == 0)
    def _():
        o_ref[...] = res_ref[...]

    u = u_ref[...]
    v = v_ref[...]
    if cast_refs:
        u = u.astype(BF16)
        v = v.astype(BF16)
        cast_refs[0][...] = u
        cast_refs[1][...] = v
    h_t = _nt(u, xn_ref[...])
    act = (0.5 * h_t * (1.0 + lax.erf(h_t * (1.0 / math.sqrt(2.0))))).astype(BF16)
    w_rows = []
    for ii in range(n_i):
        g = jnp.zeros((PEER_NKEYS, h_t.shape[1]), BF16)
        for h in range(PEER_HEADS):
            cnt_i = jnp.broadcast_to(cnt_ref[ii, h:h + 1, :], g.shape).astype(BF16)
            e0_i = jnp.broadcast_to(e0_ref[ii, h:h + 1, :], g.shape).astype(BF16)
            g = g + jnp.where(rnk_ref[h] < cnt_i, e1_ref[h], jnp.zeros((), BF16)) * e0_i
        w_rows.append(g * act[ii * PEER_NKEYS:(ii + 1) * PEER_NKEYS])
    w_t = jnp.concatenate(w_rows, axis=0)
    o_ref[...] += _tn(w_t, v)


def peer_experts(xn, u, v, layer, route, res, *, tb, te, emit_cast):
    m, d = xn.shape
    n_exp = u.shape[-2]
    assert m % tb == 0 and n_exp % te == 0 and te % PEER_NKEYS == 0 and n_exp == PEER_NKEYS * PEER_NKEYS
    assert emit_cast == (u.ndim == 3) and (not emit_cast or m == tb)
    if u.ndim == 3:
        tspec = pl.BlockSpec((None, te, d), lambda i, e: (layer, e, 0))
    else:
        tspec = pl.BlockSpec((te, d), lambda i, e: (e, 0))
    n_i = te // PEER_NKEYS
    ispec = pl.BlockSpec((n_i, PEER_HEADS, tb), lambda i, e: (e, 0, i))
    rspec = pl.BlockSpec((PEER_HEADS, PEER_NKEYS, tb), lambda i, e: (0, 0, i))
    ospec = pl.BlockSpec((tb, d), lambda i, e: (i, 0))
    o_shape = jax.ShapeDtypeStruct((m, d), F32)
    tbytes = u.dtype.itemsize
    blocks = (tb * d * 2 + 2 * te * d * tbytes + 2 * n_i * PEER_HEADS * tb * 4 + 2 * PEER_HEADS * PEER_NKEYS * tb * 2
              + 2 * tb * d * 4 + (2 * te * d * 2 if emit_cast else 0))
    if emit_cast:
        cshape = jax.ShapeDtypeStruct((n_exp, d), BF16)
        cspec = pl.BlockSpec((te, d), lambda i, e: (e, 0))
        out_shape, out_specs = (o_shape, cshape, cshape), (ospec, cspec, cspec)
    else:
        out_shape, out_specs = o_shape, ospec
    return pl.pallas_call(
        functools.partial(_peer_expert_kernel, n_i=n_i),
        out_shape=out_shape,
        grid=(m // tb, n_exp // te),
        in_specs=[pl.BlockSpec((tb, d), lambda i, e: (i, 0)), tspec, tspec, ispec, ispec, rspec, rspec,
                  pl.BlockSpec((tb, d), lambda i, e: (i, 0))],
        out_specs=out_specs,
        compiler_params=_params(("parallel", "arbitrary"), _vmem_limit(blocks, 0, 6 * te * tb * 4)),
        name="peer_experts",
    )(xn, u, v, *route, res)


def peer_layer(x, g, w_q, subkeys, u, v, layer, *, tm, tb_route, tb_exp, te, emit_cast):
    q, xn = norm_matmul(x, g, w_q, tm=tm, tn=w_q.shape[1], emit_xn=True)
    route = peer_route(q, subkeys, tb=tb_route)
    return peer_experts(xn, u, v, layer, route, x, tb=tb_exp, te=te, emit_cast=emit_cast)


def _pad_gla_weight(w):
    d = w.shape[0]
    n_tok = GLA_A_COL + GLA_RANK
    return jnp.concatenate([w[:, :n_tok], jnp.zeros((d, GLA_QX_COL - n_tok), w.dtype), w[:, n_tok:]],
                           axis=1).astype(BF16)


def kernel(x_prompt, x_sample, state_gla, cache_moba_k, cache_moba_v, cache_mem_k, cache_mem_v, page_table, mem_prompt, ln_mix, ln_mem, ln_ffn, ln_final, w_in_gla, w_alpha_gla, b_alpha_gla, g_head_gla, w_in_moba, rel_bias, w_mem_kv, w_out, w_peer_q, peer_subkeys, peer_u, peer_v):
    bsz, seq, d = x_prompt.shape
    dec_b, dec_t, _ = x_sample.shape
    assert dec_t <= SAMPLE_ROWS
    xp = x_prompt.reshape(bsz * seq, d)
    xs = jnp.pad(x_sample, ((0, 0), (0, SAMPLE_ROWS - dec_t), (0, 0))).reshape(dec_b * SAMPLE_ROWS, d)
    mem2d = mem_prompt.reshape(bsz * N_MEM, d)
    n_dec = dec_b * SAMPLE_ROWS
    cmk = cache_mem_k.reshape(dec_b * DEPTH, N_MEM * XA_HEADS, HEAD_DIM)
    cmv = cache_mem_v.reshape(dec_b * DEPTH, N_MEM * XA_HEADS, HEAD_DIM)

    gla_p, gla_s, kp_l, vp_l, ks_l, vs_l, mkp_l, mvp_l = [], [], [], [], [], [], [], []
    for i in range(DEPTH):
        j = i // N_MIXERS
        kv = norm_matmul(mem2d, ln_mem[i], w_mem_kv[i].astype(BF16), tm=512, tn=512).reshape(bsz, N_MEM, 2 * XA_WIDTH)
        mkp_l.append(kv[..., :XA_WIDTH].reshape(bsz, N_MEM, XA_HEADS, HEAD_DIM))
        mvp_l.append(kv[..., XA_WIDTH:].reshape(bsz, N_MEM, XA_HEADS, HEAD_DIM))
        if i % N_MIXERS == 0:
            w_in = _pad_gla_weight(w_in_gla[j])
            wa = jnp.zeros((LANE, GLA_HEADS * GLA_DK), F32).at[:GLA_RANK].set(w_alpha_gla[j]).astype(BF16)
            pp = norm_matmul(xp, ln_mix[i], w_in, tm=1024, tn=GLA_COLS // 4).reshape(bsz, seq, GLA_COLS)
            ps = norm_matmul(xs, ln_mix[i], w_in, tm=n_dec, tn=512).reshape(dec_b, SAMPLE_ROWS, GLA_COLS)
            zeros0 = jnp.zeros((bsz, 1, GLA_HEADS, GLA_DK, GLA_DV), F32)
            mix_p, s_p = gla_mix(pp, wa, b_alpha_gla[j], g_head_gla[j], zeros0, 0, chunk=GLA_PROMPT_CHUNK, t_valid=seq)
            ps_pad = jnp.pad(ps, ((0, 0), (0, GLA_SAMPLE_CHUNK - SAMPLE_ROWS), (0, 0)))
            mix_s, s_s = gla_mix(ps_pad, wa, b_alpha_gla[j], g_head_gla[j], state_gla, j,
                                 chunk=GLA_SAMPLE_CHUNK, t_valid=dec_t)
            mix_s = mix_s[:, :SAMPLE_ROWS]
            gla_p.append(s_p[:, 0])
            gla_s.append(s_s[:, 0])
            qx_col = GLA_QX_COL
        else:
            w_in = w_in_moba[j].astype(BF16)
            pp = norm_matmul(xp, ln_mix[i], w_in, tm=1024, tn=w_in.shape[1] // 4).reshape(bsz, seq, -1)
            ps = norm_matmul(xs, ln_mix[i], w_in, tm=n_dec, tn=512).reshape(dec_b, SAMPLE_ROWS, -1)
            mix_p, k_hm, v_hm = moba_prompt_mix(pp, rel_bias)
            mix_s = moba_sample_mix(ps, cache_moba_k, cache_moba_v, page_table, j, rel_bias)
            sh = lambda t, n: t.reshape(t.shape[0], n, MOBA_HEADS, HEAD_DIM)
            kp_l.append(jnp.transpose(k_hm, (0, 2, 1, 3)))
            vp_l.append(jnp.transpose(v_hm, (0, 2, 1, 3)))
            ks_l.append(sh(ps[:, :dec_t, TOK_WIDTH:2 * TOK_WIDTH], dec_t))
            vs_l.append(sh(ps[:, :dec_t, 2 * TOK_WIDTH:3 * TOK_WIDTH], dec_t))
            qx_col = MOBA_QX_COL
        kv_rows = kv.reshape(bsz, N_MEM * 2 * XA_HEADS, HEAD_DIM)
        xa_p = cross_attend(pp, qx_col, kv_rows, kv_rows, lambda b: b, stride=2 * XA_HEADS, k_off=0, v_off=XA_HEADS,
                            tq=512)
        xa_s = cross_attend(ps, qx_col, cmk, cmv, lambda b, i=i: b * DEPTH + i, stride=XA_HEADS, k_off=0, v_off=0,
                            tq=SAMPLE_ROWS)
        w_o = w_out[i].astype(BF16)
        xp = out_proj(mix_p.reshape(bsz * seq, TOK_WIDTH), xa_p.reshape(bsz * seq, XA_WIDTH), w_o, xp, tm=512, tn=1024)
        xs = out_proj(mix_s.reshape(n_dec, TOK_WIDTH), xa_s.reshape(n_dec, XA_WIDTH), w_o, xs, tm=n_dec, tn=1024)
        w_q = w_peer_q[i].astype(BF16)
        xs, u16, v16 = peer_layer(xs, ln_ffn[i], w_q, peer_subkeys[i], peer_u, peer_v, i, tm=n_dec, tb_route=n_dec,
                                  tb_exp=n_dec, te=256, emit_cast=True)
        xp = peer_layer(xp, ln_ffn[i], w_q, peer_subkeys[i], u16, v16, None, tm=512, tb_route=256,
                        tb_exp=512, te=1024, emit_cast=False)
    y_prompt = rmsnorm_rows(xp, ln_final, tm=512).reshape(bsz, seq, d)
    y_sample = rmsnorm_rows(xs, ln_final, tm=n_dec).reshape(dec_b, SAMPLE_ROWS, d)[:, :dec_t]
    return (y_prompt, y_sample,
            jnp.stack(gla_p, axis=1), jnp.stack(gla_s, axis=1),
            jnp.stack(kp_l, axis=1), jnp.stack(vp_l, axis=1),
            jnp.stack(ks_l, axis=1), jnp.stack(vs_l, axis=1),
            jnp.stack(mkp_l, axis=1), jnp.stack(mvp_l, axis=1))
```

```python
import functools
import math

import numpy as np
import jax
import jax.numpy as jnp
from jax import lax
from jax.experimental import pallas as pl
from jax.experimental.pallas import tpu as pltpu

D_MODEL = 2048
DEPTH = 2
N_MIXERS = 2
HEAD_DIM = 128
N_MEM = 256
XA_HEADS = 4
XA_WIDTH = XA_HEADS * HEAD_DIM
TOK_WIDTH = D_MODEL - XA_WIDTH
GLA_HEADS = 6
GLA_DV = TOK_WIDTH // GLA_HEADS
GLA_DK = GLA_DV // 2
GLA_RANK = 16
GLA_TAU = 16.0
GLA_CHUNK = 64
MOBA_HEADS = TOK_WIDTH // HEAD_DIM
MOBA_BLOCK = 256
MOBA_TOPK = 3
REL_BUCKETS = 32
REL_MAX_DIST = 128
PEER_HEADS = 8
PEER_NKEYS = 128
PEER_QDIM = 256
PEER_TOPK = 16
EPS = 1e-6

F32 = jnp.float32
BF16 = jnp.bfloat16

LANE = 128
SUBLANE = 8
VMEM_CAP_BYTES = 56 * 1024 * 1024

NEG = -1e30
_NT = (((1,), (1,)), ((), ()))
_TN = (((0,), (0,)), ((), ()))

GLA_PROMPT_CHUNK = 128
GLA_SAMPLE_CHUNK = 16
SAMPLE_ROWS = 8
MOBA_SAMPLE_BLOCKS_PER_STEP = 4
GLA_HEADS_PER_STEP = 2


def _nt(a, b):
    return lax.dot_general(a, b, _NT, preferred_element_type=F32)


def _tn(a, b):
    return lax.dot_general(a, b, _TN, preferred_element_type=F32)


def _split3(x):
    x1 = x.astype(BF16)
    r1 = x - x1.astype(F32)
    x2 = r1.astype(BF16)
    x3 = (r1 - x2.astype(F32)).astype(BF16)
    return x1, x2, x3


def _nt_hi(a, b):
    a1 = a.astype(BF16)
    a2 = (a - a1.astype(F32)).astype(BF16)
    b1 = b.astype(BF16)
    b2 = (b - b1.astype(F32)).astype(BF16)
    return _nt(a1, b1) + _nt(a1, b2) + _nt(a2, b1)


def _vmem_limit(block_bytes, scratch_bytes=0, temp_bytes=0):
    return int(min(VMEM_CAP_BYTES, 2 * block_bytes + scratch_bytes + temp_bytes + (4 << 20)))


def _params(sem, vmem):
    return pltpu.CompilerParams(dimension_semantics=sem, vmem_limit_bytes=vmem)


def _norm_matmul_kernel(x_ref, g_ref, w_ref, o_ref, xn_ref):
    @pl.when(pl.program_id(1) == 0)
    def _():
        x = x_ref[...]
        y = x * lax.rsqrt(jnp.mean(x * x, axis=-1, keepdims=True) + EPS)
        xn_ref[...] = (y * g_ref[...]).astype(BF16)

    o_ref[...] = jnp.dot(xn_ref[...], w_ref[...], preferred_element_type=F32)


def norm_matmul(x, g, w, *, tm, tn, emit_xn=False):
    m, d = x.shape
    n = w.shape[1]
    assert m % tm == 0 and n % tn == 0 and w.shape[0] == d
    o_shape = jax.ShapeDtypeStruct((m, n), F32)
    o_spec = pl.BlockSpec((tm, tn), lambda i, j: (i, j))
    xn_bytes = tm * d * 2
    blocks = tm * d * 4 + d * 4 + d * tn * 2 + tm * tn * 4
    kwargs = dict(
        grid=(m // tm, n // tn),
        in_specs=[
            pl.BlockSpec((tm, d), lambda i, j: (i, 0)),
            pl.BlockSpec((1, d), lambda i, j: (0, 0)),
            pl.BlockSpec((d, tn), lambda i, j: (0, j)),
        ],
        name="norm_matmul",
    )
    if emit_xn:
        return pl.pallas_call(
            _norm_matmul_kernel,
            out_shape=(o_shape, jax.ShapeDtypeStruct((m, d), BF16)),
            out_specs=(o_spec, pl.BlockSpec((tm, d), lambda i, j: (i, 0))),
            compiler_params=_params(("parallel", "arbitrary"), _vmem_limit(blocks + xn_bytes, 0, tm * d * 8)),
            **kwargs,
        )(x, g.reshape(1, d), w)
    return pl.pallas_call(
        _norm_matmul_kernel,
        out_shape=o_shape,
        out_specs=o_spec,
        scratch_shapes=[pltpu.VMEM((tm, d), BF16)],
        compiler_params=_params(("parallel", "arbitrary"), _vmem_limit(blocks, xn_bytes, tm * d * 8)),
        **kwargs,
    )(x, g.reshape(1, d), w)


def _gla_constants(chunk):
    nlev = int(round(math.log2(chunk)))
    assert 1 << nlev == chunk
    idx = np.arange(chunk)
    col, row = idx[None, :], idx[:, None]
    mats = [col <= row, col > row]
    masks = []
    for lev in range(nlev):
        m = chunk >> (lev + 1)
        grp = idx // (2 * m)
        mid = grp * 2 * m + m
        up = (idx % (2 * m)) >= m
        mats.append((col >= mid[:, None]) & (col <= row) & up[:, None])
        mats.append((col > row) & (col < mid[:, None]) & (~up)[:, None])
        masks.append((grp[:, None] == grp[None, :]) & up[:, None] & (~up)[None, :])
    return (np.concatenate(mats, axis=0).astype(np.float32), np.stack(masks).astype(np.float32))


def _gla_kernel(q_ref, k_ref, v_ref, r_ref, a_ref, wa_ref, ba_ref, gh_ref, s0_ref, cmat_ref, lmask_ref,
                o_ref, sout_ref, st_ref, *, chunk, n_chunks, t_valid, t_total, hps):
    c_rows = chunk
    nlev = int(round(math.log2(chunk)))
    dk, dv = GLA_DK, GLA_DV
    for hh in range(hps):
        st_ref[hh] = jnp.transpose(s0_ref[0, 0, hh])

    def load_head(hh, rows):
        ksl = slice(hh * dk, (hh + 1) * dk)
        vsl = slice(hh * dv, (hh + 1) * dv)
        return (q_ref[0, rows, ksl], k_ref[0, rows, ksl], v_ref[0, rows, vsl], r_ref[0, rows, vsl], st_ref[hh])

    def range_sums(r0, a_blk):
        z = jnp.dot(a_blk, wa_ref[...], preferred_element_type=F32) + ba_ref[...]
        g = (jnp.minimum(z, 0.0) - jnp.log1p(jnp.exp(-jnp.abs(z)))) * (1.0 / GLA_TAU)
        if t_valid < t_total:
            t_idx = r0 + lax.broadcasted_iota(jnp.int32, (c_rows, 1), 0)
            g = jnp.where(t_idx < t_valid, g, 0.0)
        e3 = jnp.dot(cmat_ref[...], jnp.concatenate(_split3(g), axis=1), preferred_element_type=F32)
        w = hps * dk
        return e3[:, 0:w] + e3[:, w:2 * w] + e3[:, 2 * w:3 * w]

    def one_head(hh, e_all, loaded):
        q, k, v, rr, st = loaded
        q = q * (dk ** -0.5)
        e = e_all[:, hh * dk:(hh + 1) * dk]
        qd = (q * jnp.exp(e[0:c_rows])).astype(BF16)
        kd = (k * jnp.exp(e[c_rows:2 * c_rows])).astype(BF16)
        att = jnp.zeros((c_rows, c_rows), F32)
        for lev in range(nlev):
            ea = e[(2 + 2 * lev) * c_rows:(3 + 2 * lev) * c_rows]
            eb = e[(3 + 2 * lev) * c_rows:(4 + 2 * lev) * c_rows]
            ql = (q * jnp.exp(ea)).astype(BF16)
            kl = (k * jnp.exp(eb)).astype(BF16)
            att = att + lmask_ref[lev] * _nt(ql, kl)
        vb = v.astype(BF16)
        o = jnp.dot(att.astype(BF16), vb, preferred_element_type=F32)
        o = o + jnp.sum(q * k, axis=-1, keepdims=True) * v
        o = o + _nt(qd, st.astype(BF16))
        b_last = e[c_rows - 1:c_rows, :]
        st_new = st * jnp.exp(b_last) + _tn(vb, kd)
        on = o * lax.rsqrt(jnp.mean(o * o, axis=-1, keepdims=True) + EPS) * gh_ref[...]
        return (on * (rr * jax.nn.sigmoid(rr))).astype(BF16), st_new

    def body(c, carry):
        r0 = pl.multiple_of(c * c_rows, c_rows)
        rows = pl.ds(r0, c_rows)
        a_blk = a_ref[0, rows, :].astype(BF16)
        loaded = [load_head(hh, rows) for hh in range(hps)]
        e_all = range_sums(r0, a_blk)
        results = [one_head(hh, e_all, loaded[hh]) for hh in range(hps)]
        for hh, (out, st_new) in enumerate(results):
            o_ref[0, rows, hh * dv:(hh + 1) * dv] = out
            st_ref[hh] = st_new
        return carry

    lax.fori_loop(0, n_chunks, body, 0)
    for hh in range(hps):
        sout_ref[0, 0, hh] = jnp.transpose(st_ref[hh])


def gla_mix(p, w_alpha_pad, b_alpha, g_head, state0, state_layer, *, chunk, t_valid):
    bsz, t_total, _ = p.shape
    assert t_total % chunk == 0
    cmat, lmask = _gla_constants(chunk)
    n_e = cmat.shape[0]
    dk, dv = GLA_DK, GLA_DV
    hps = GLA_HEADS_PER_STEP
    n_hg = GLA_HEADS // hps
    kblk = lambda off: (lambda b, h: (b, 0, off + h))
    blocks = hps * (t_total * (2 * dk + 2 * dv) * 4 + t_total * dv * 2 + 2 * dk * dv * 4) + t_total * LANE * 4
    mix, s_out = pl.pallas_call(
        functools.partial(_gla_kernel, chunk=chunk, n_chunks=t_total // chunk, t_valid=t_valid, t_total=t_total,
                          hps=hps),
        out_shape=(jax.ShapeDtypeStruct((bsz, t_total, GLA_HEADS * dv), BF16),
                   jax.ShapeDtypeStruct((bsz, 1, GLA_HEADS, dk, dv), F32)),
        grid=(bsz, n_hg),
        in_specs=[
            pl.BlockSpec((1, t_total, hps * dk), kblk(0)),
            pl.BlockSpec((1, t_total, hps * dk), kblk(n_hg)),
            pl.BlockSpec((1, t_total, hps * dv), kblk(n_hg)),
            pl.BlockSpec((1, t_total, hps * dv), kblk(2 * n_hg)),
            pl.BlockSpec((1, t_total, LANE), lambda b, h: (b, 0, GLA_A_COL // LANE)),
            pl.BlockSpec((LANE, hps * dk), lambda b, h: (0, h)),
            pl.BlockSpec((1, hps * dk), lambda b, h: (0, h)),
            pl.BlockSpec((1, dv), lambda b, h: (0, 0)),
            pl.BlockSpec((1, 1, hps, dk, dv), lambda b, h: (b, state_layer, h, 0, 0)),
            pl.BlockSpec((n_e, chunk), lambda b, h: (0, 0)),
            pl.BlockSpec(lmask.shape, lambda b, h: (0, 0, 0)),
        ],
        out_specs=(pl.BlockSpec((1, t_total, hps * dv), lambda b, h: (b, 0, h)),
                   pl.BlockSpec((1, 1, hps, dk, dv), lambda b, h: (b, 0, h, 0, 0))),
        scratch_shapes=[pltpu.VMEM((hps, dv, dk), F32)],
        compiler_params=_params(("parallel", "parallel"),
                                _vmem_limit(blocks, hps * dv * dk * 4, 2 * n_e * 3 * hps * dk * 4 + (4 << 20))),
        name="gla_mix",
    )(p, p, p, p, p, w_alpha_pad, b_alpha.reshape(1, -1), g_head.reshape(1, -1), state0,
      jnp.asarray(cmat, BF16), jnp.asarray(lmask, F32))
    return mix, s_out


GLA_A_COL = 2 * GLA_HEADS * GLA_DK + 2 * GLA_HEADS * GLA_DV
GLA_QX_COL = GLA_A_COL + XA_WIDTH
GLA_COLS = GLA_QX_COL + XA_WIDTH
MOBA_QX_COL = 3 * TOK_WIDTH


def _bucket_thresholds(max_dist):
    n = np.arange(max_dist + 1)
    max_exact = REL_BUCKETS // 2
    nf = np.maximum(n, 1).astype(np.float32)
    large = max_exact + (np.log(nf / np.float32(max_exact)) / np.float32(math.log(REL_MAX_DIST / max_exact))
                         * np.float32(REL_BUCKETS - max_exact)).astype(np.int32)
    large = np.minimum(large, REL_BUCKETS - 1)
    bucket = np.where(n < max_exact, n, large)
    assert np.all(np.diff(bucket) >= 0) and bucket[-1] == REL_BUCKETS - 1
    return [int(np.argmax(bucket >= b)) for b in range(REL_BUCKETS)]


def _moba_prompt_kernel(rb_ref, q_ref, k_ref, v_ref, o_ref, ko_ref, vo_ref, km_ref, vt_ref, bias_ref, *, nb, thr):
    h = pl.program_id(1)
    qi = pl.program_id(2)
    blk = MOBA_BLOCK
    n_sel = max(1, min(MOBA_TOPK, nb - 1))
    tk = lax.broadcasted_iota(jnp.int32, (blk, blk), 0)
    tq = lax.broadcasted_iota(jnp.int32, (blk, blk), 1)

    @pl.when(qi == 0)
    def _():
        ko_ref[0, 0] = k_ref[0]
        vo_ref[0, 0] = v_ref[0]
        for n in range(nb):
            km_ref[n:n + 1, :] = jnp.mean(k_ref[0, n * blk:(n + 1) * blk, :], axis=0, keepdims=True)
            vt_ref[n] = jnp.transpose(v_ref[0, n * blk:(n + 1) * blk, :]).astype(BF16)
        for which in range(2):
            d = tq - tk + which * blk
            bias = jnp.full((blk, blk), rb_ref[h, 0], F32)
            for bkt in range(1, REL_BUCKETS):
                bias = jnp.where(d >= thr[bkt], rb_ref[h, bkt], bias)
            bias_ref[which] = jnp.where(d >= 0, bias, NEG)

    q = q_ref[0]
    gate = _nt_hi(km_ref[...], q)
    gs = [gate[n:n + 1, :] for n in range(nb)]
    sels = []
    for n in range(nb):
        rank = jnp.zeros((1, blk), jnp.int32)
        for m in range(nb):
            if m != n:
                beats = (gs[m] >= gs[n]) if m < n else (gs[m] > gs[n])
                rank = rank + jnp.where(beats, 1, 0) * (m < qi).astype(jnp.int32)
        sels.append(jnp.where(rank < n_sel, 1, 0) * (n < qi).astype(jnp.int32))

    qb = q.astype(BF16)
    scale = HEAD_DIM ** -0.5
    far_bias = rb_ref[h, REL_BUCKETS - 1]

    def attend(n_tiles):
        tiles, blocks = [], []
        for dd in range(n_tiles):
            j = jnp.maximum(qi - dd, 0)
            kj = k_ref[0, pl.ds(pl.multiple_of(j * blk, blk), blk), :]
            s = _nt(kj.astype(BF16), qb) * scale
            if dd == 0:
                s = s + bias_ref[0]
            else:
                chosen = jnp.zeros((1, blk), jnp.int32)
                for n in range(nb):
                    chosen = chosen + sels[n] * (j == n).astype(jnp.int32)
                chosen = chosen * (dd <= qi).astype(jnp.int32)
                row = jnp.where(chosen > 0, 0.0 if dd == 1 else far_bias, NEG)
                s = (s + bias_ref[1] + row) if dd == 1 else (s + row)
            tiles.append(s)
            blocks.append(j)
        m = jnp.max(tiles[0], axis=0, keepdims=True)
        for s in tiles[1:]:
            m = jnp.maximum(m, jnp.max(s, axis=0, keepdims=True))
        l = jnp.zeros((1, blk), F32)
        acc = jnp.zeros((HEAD_DIM, blk), F32)
        for j, s in zip(blocks, tiles):
            p = jnp.exp(s - m)
            l = l + jnp.sum(p, axis=0, keepdims=True)
            acc = acc + jnp.dot(vt_ref[j], p.astype(BF16), preferred_element_type=F32)
        o_ref[0] = jnp.transpose(acc / l).astype(BF16)

    half = (nb + 1) // 2
    if half < nb:
        @pl.when(qi < half)
        def _():
            attend(half)

        @pl.when(qi >= half)
        def _():
            attend(nb)
    else:
        attend(nb)


def moba_prompt_mix(p, rel_bias):
    bsz, t_total, _ = p.shape
    blk = MOBA_BLOCK
    assert t_total % blk == 0
    nb = t_total // blk
    thr = _bucket_thresholds(2 * blk)
    assert thr[REL_BUCKETS - 1] <= blk + 1
    hd = HEAD_DIM
    blocks = blk * hd * 4 + 4 * t_total * hd * 4 + blk * hd * 2
    scratch = SUBLANE * hd * 4 + hd * t_total * 2 + 2 * blk * blk * 4
    kv_shape = jax.ShapeDtypeStruct((bsz, MOBA_HEADS, t_total, hd), F32)
    kv_spec = pl.BlockSpec((1, 1, t_total, hd), lambda b, h, i: (b, h, 0, 0))
    return pl.pallas_call(
        functools.partial(_moba_prompt_kernel, nb=nb, thr=thr),
        out_shape=(jax.ShapeDtypeStruct((bsz, t_total, MOBA_HEADS * hd), BF16), kv_shape, kv_shape),
        grid=(bsz, MOBA_HEADS, nb),
        in_specs=[
            pl.BlockSpec(memory_space=pltpu.SMEM),
            pl.BlockSpec((1, blk, hd), lambda b, h, i: (b, i, h)),
            pl.BlockSpec((1, t_total, hd), lambda b, h, i: (b, 0, MOBA_HEADS + h)),
            pl.BlockSpec((1, t_total, hd), lambda b, h, i: (b, 0, 2 * MOBA_HEADS + h)),
        ],
        out_specs=(pl.BlockSpec((1, blk, hd), lambda b, h, i: (b, i, h)), kv_spec, kv_spec),
        scratch_shapes=[
            pltpu.VMEM((nb, hd), F32),
            pltpu.VMEM((nb, hd, blk), BF16),
            pltpu.VMEM((2, blk, blk), F32),
        ],
        compiler_params=_params(("parallel", "parallel", "arbitrary"), _vmem_limit(blocks, scratch, 2 * nb * blk * blk * 4)),
        name="moba_prompt",
    )(jnp.transpose(rel_bias), p, p, p)


def _moba_sample_kernel(pt_ref, q_ref, kn_ref, vn_ref, *rest, n_blocks, bps):
    del pt_ref
    npg = 2 * bps
    ck_refs, cv_refs = rest[:npg], rest[npg:2 * npg]
    (rbx_ref, o_ref, qbd_ref, ksum_ref, ma_ref, la_ref, oall_ref, blast_ref, bown_ref, kown_ref,
     vown_ref) = rest[2 * npg:]
    step = pl.program_id(1)
    n_steps = n_blocks // bps
    hd = HEAD_DIM
    blk = MOBA_BLOCK
    nrow = LANE
    width = MOBA_HEADS * hd
    scale = hd ** -0.5
    row = lax.broadcasted_iota(jnp.int32, (nrow, LANE), 0)
    lane = lax.broadcasted_iota(jnp.int32, (nrow, LANE), 1)
    t_row = row % SAMPLE_ROWS
    thr = _bucket_thresholds(blk + SAMPLE_ROWS)
    assert thr[REL_BUCKETS - 1] <= blk + 1

    @pl.when(step == 0)
    def _():
        q8 = q_ref[0]
        rowg = lax.broadcasted_iota(jnp.int32, (nrow, width), 0) // SAMPLE_ROWS
        colg = lax.broadcasted_iota(jnp.int32, (nrow, width), 1) // hd
        qbd_ref[...] = jnp.where(rowg == colg, jnp.concatenate([q8] * (nrow // SAMPLE_ROWS), axis=0), 0.0)
        ksum_ref[...] = jnp.zeros(ksum_ref.shape, F32)
        ma_ref[...] = jnp.full(ma_ref.shape, NEG, F32)
        la_ref[...] = jnp.zeros(la_ref.shape, F32)
        dlast = (blk + lax.broadcasted_iota(jnp.int32, (nrow, blk), 0) % SAMPLE_ROWS
                 - lax.broadcasted_iota(jnp.int32, (nrow, blk), 1))
        bias = jnp.broadcast_to(rbx_ref[:, 0:1], (nrow, blk))
        for bkt in range(1, REL_BUCKETS):
            bias = jnp.where(dlast >= thr[bkt], rbx_ref[:, bkt:bkt + 1], bias)
        blast_ref[...] = bias
        down = t_row - lane
        bias = jnp.zeros((nrow, LANE), F32)
        for dd in range(SAMPLE_ROWS):
            bias = jnp.where(down == dd, rbx_ref[:, dd:dd + 1], bias)
        bown_ref[...] = bias
        kown_ref[...] = jnp.zeros(kown_ref.shape, F32)
        vown_ref[...] = jnp.zeros(vown_ref.shape, F32)
        kown_ref[0:SAMPLE_ROWS, :] = kn_ref[0]
        vown_ref[0:SAMPLE_ROWS, :] = vn_ref[0]

    qbd = qbd_ref[...].astype(BF16)
    rowg128 = lax.broadcasted_iota(jnp.int32, (nrow, hd), 0) // SAMPLE_ROWS

    def partial_softmax(kb, vb, bias, valid):
        s = _nt(qbd, kb) * scale + bias
        if valid is not None:
            s = jnp.where(valid, s, NEG)
        m = jnp.max(s, axis=-1, keepdims=True)
        p = jnp.exp(s - m)
        if valid is not None:
            p = jnp.where(valid, p, 0.0)
        l = jnp.sum(p, axis=-1, keepdims=True)
        o_full = jnp.dot(p.astype(BF16), vb, preferred_element_type=F32)
        o = jnp.zeros((nrow, hd), F32)
        for hh in range(MOBA_HEADS):
            o = o + jnp.where(rowg128 == hh, o_full[:, hh * hd:(hh + 1) * hd], 0.0)
        return m, l, o

    def page_rows(ref):
        return jnp.concatenate([ref[hh] for hh in range(MOBA_HEADS)], axis=1)

    far_bias = rbx_ref[:, REL_BUCKETS - 1:REL_BUCKETS]
    ma_new = ma_ref[...]
    la_new = la_ref[...]
    for r in range(bps):
        b_idx = step * bps + r
        kb = jnp.concatenate([page_rows(ck_refs[2 * r]), page_rows(ck_refs[2 * r + 1])], axis=0)
        vb = jnp.concatenate([page_rows(cv_refs[2 * r]), page_rows(cv_refs[2 * r + 1])], axis=0)
        ksum_ref[pl.ds(b_idx, 1), :] = jnp.sum(kb, axis=0, keepdims=True)
        if r == bps - 1:
            bias = jnp.where(step == n_steps - 1, blast_ref[...], jnp.broadcast_to(far_bias, (nrow, blk)))
        else:
            bias = far_bias
        m, l, o = partial_softmax(kb.astype(BF16), vb.astype(BF16), bias, None)
        oall_ref[b_idx] = o
        ma_new = jnp.where(lane == b_idx, m, ma_new)
        la_new = jnp.where(lane == b_idx, l, la_new)
    ma_ref[...] = ma_new
    la_ref[...] = la_new

    @pl.when(step == n_steps - 1)
    def _():
        valid_own = jnp.logical_and(lane <= t_row, lane < SAMPLE_ROWS)
        m_own, l_own, o_own = partial_softmax(kown_ref[...].astype(BF16), vown_ref[...].astype(BF16),
                                              bown_ref[...], valid_own)
        kmean = ksum_ref[...] * (1.0 / MOBA_BLOCK)
        gate = _nt_hi(qbd_ref[...], kmean)
        g = jnp.where(lane < n_blocks, gate, -jnp.inf)
        sel_i = jnp.zeros((nrow, LANE), jnp.int32)
        for _ in range(max(1, min(MOBA_TOPK, n_blocks))):
            gmax = jnp.max(g, axis=-1, keepdims=True)
            first = jnp.min(jnp.where(g == gmax, lane, LANE), axis=-1, keepdims=True)
            pick = lane == first
            sel_i = jnp.where(pick, 1, sel_i)
            g = jnp.where(pick, -jnp.inf, g)
        sel = sel_i > 0
        ma = jnp.where(sel, ma_new, NEG)
        m_all = jnp.maximum(jnp.max(ma, axis=-1, keepdims=True), m_own)
        wa = jnp.where(sel, jnp.exp(ma - m_all), 0.0)
        w_own = jnp.exp(m_own - m_all)
        den = jnp.sum(wa * la_new, axis=-1, keepdims=True) + w_own * l_own
        num = w_own * o_own
        for n in range(n_blocks):
            num = num + wa[:, n:n + 1] * oall_ref[n]
        out = num / den
        o_ref[0] = jnp.concatenate([out[hh * SAMPLE_ROWS:(hh + 1) * SAMPLE_ROWS, :] for hh in range(MOBA_HEADS)],
                                   axis=1).astype(BF16)


def moba_sample_mix(p, cache_k, cache_v, page_table, layer, rel_bias):
    bs, rows, _ = p.shape
    n_pool, n_layers, page, heads, hd = cache_k.shape
    n_pages = page_table.shape[1]
    past_len = n_pages * page
    assert rows == SAMPLE_ROWS and heads == MOBA_HEADS and hd == HEAD_DIM and MOBA_BLOCK == 2 * page
    assert past_len % MOBA_BLOCK == 0 and MOBA_HEADS * SAMPLE_ROWS <= LANE and past_len // MOBA_BLOCK <= LANE
    n_blocks = past_len // MOBA_BLOCK
    bps = MOBA_SAMPLE_BLOCKS_PER_STEP if n_blocks % MOBA_SAMPLE_BLOCKS_PER_STEP == 0 else 1
    npg = 2 * bps
    width = heads * hd
    rbx = jnp.zeros((LANE, LANE), F32).at[:heads * SAMPLE_ROWS, :REL_BUCKETS].set(
        jnp.repeat(jnp.transpose(rel_bias), SAMPLE_ROWS, axis=0))

    def pmap(r):
        return lambda b, g, pt: (pt[b, g * npg + r], layer, 0, 0, 0)

    ck = jnp.transpose(cache_k, (0, 1, 3, 2, 4))
    cv = jnp.transpose(cache_v, (0, 1, 3, 2, 4))
    page_specs = [pl.BlockSpec((None, None, heads, page, hd), pmap(r)) for r in range(npg)]
    blocks = 3 * rows * width * 4 + 2 * npg * page * width * 4 + LANE * LANE * 4 + rows * width * 2
    scratch = (LANE * width * 4 * 4 + 3 * LANE * LANE * 4 + LANE * MOBA_BLOCK * 4 + n_blocks * LANE * hd * 4)
    return pl.pallas_call(
        functools.partial(_moba_sample_kernel, n_blocks=n_blocks, bps=bps),
        out_shape=jax.ShapeDtypeStruct((bs, rows, width), BF16),
        grid_spec=pltpu.PrefetchScalarGridSpec(
            num_scalar_prefetch=1,
            grid=(bs, n_blocks // bps),
            in_specs=[
                pl.BlockSpec((1, rows, width), lambda b, g, pt: (b, 0, 0)),
                pl.BlockSpec((1, rows, width), lambda b, g, pt: (b, 0, 1)),
                pl.BlockSpec((1, rows, width), lambda b, g, pt: (b, 0, 2)),
                *page_specs, *page_specs,
                pl.BlockSpec((LANE, LANE), lambda b, g, pt: (0, 0)),
            ],
            out_specs=pl.BlockSpec((1, rows, width), lambda b, g, pt: (b, 0, 0)),
            scratch_shapes=[
                pltpu.VMEM((LANE, width), F32),
                pltpu.VMEM((LANE, width), F32),
                pltpu.VMEM((LANE, LANE), F32), pltpu.VMEM((LANE, LANE), F32),
                pltpu.VMEM((n_blocks, LANE, hd), F32),
                pltpu.VMEM((LANE, MOBA_BLOCK), F32), pltpu.VMEM((LANE, LANE), F32),
                pltpu.VMEM((LANE, width), F32), pltpu.VMEM((LANE, width), F32),
            ],
        ),
        compiler_params=_params(("parallel", "arbitrary"), _vmem_limit(blocks, scratch, 12 << 20)),
        name="moba_sample",
    )(page_table, p, p, p, *([ck] * npg), *([cv] * npg), rbx)


def _xattn_kernel(q_ref, mk_ref, mv_ref, o_ref, *, stride, k_off, v_off):
    scale = HEAD_DIM ** -0.5
    for h in range(XA_HEADS):
        sl = slice(h * HEAD_DIM, (h + 1) * HEAD_DIM)
        mk = mk_ref[pl.ds(k_off + h, N_MEM, stride=stride), :].astype(BF16)
        mv = mv_ref[pl.ds(v_off + h, N_MEM, stride=stride), :].astype(BF16)
        s = _nt(q_ref[0, :, sl].astype(BF16), mk) * scale
        p = jnp.exp(s - jnp.max(s, axis=-1, keepdims=True))
        o = jnp.dot(p.astype(BF16), mv, preferred_element_type=F32)
        o_ref[0, :, sl] = (o / jnp.sum(p, axis=-1, keepdims=True)).astype(BF16)


def cross_attend(p, qx_col, mk, mv, mem_idx, *, stride, k_off, v_off, tq):
    bsz, t_total, _ = p.shape
    assert t_total % tq == 0 and qx_col % XA_WIDTH == 0
    mem_spec = pl.BlockSpec((None, N_MEM * stride, HEAD_DIM), lambda b, i: (mem_idx(b), 0, 0))
    mk_spec = mv_spec = mem_spec
    blocks = tq * XA_WIDTH * 4 + 2 * N_MEM * stride * HEAD_DIM * 4 + tq * XA_WIDTH * 2
    return pl.pallas_call(
        functools.partial(_xattn_kernel, stride=stride, k_off=k_off, v_off=v_off),
        out_shape=jax.ShapeDtypeStruct((bsz, t_total, XA_WIDTH), BF16),
        grid=(bsz, t_total // tq),
        in_specs=[
            pl.BlockSpec((1, tq, XA_WIDTH), lambda b, i: (b, i, qx_col // XA_WIDTH)),
            mk_spec,
            mv_spec,
        ],
        out_specs=pl.BlockSpec((1, tq, XA_WIDTH), lambda b, i: (b, i, 0)),
        compiler_params=_params(("parallel", "parallel"), _vmem_limit(blocks, 0, 8 * tq * N_MEM * 4)),
        name="cross_attend",
    )(p, mk, mv)


def _out_proj_kernel(mix_ref, xa_ref, w1_ref, w2_ref, res_ref, o_ref):
    o_ref[...] = (res_ref[...] + jnp.dot(mix_ref[...], w1_ref[...], preferred_element_type=F32)
                  + jnp.dot(xa_ref[...], w2_ref[...], preferred_element_type=F32))


def out_proj(mix, xa, w, res, *, tm, tn):
    m, d = res.shape
    assert m % tm == 0 and d % tn == 0 and TOK_WIDTH % XA_WIDTH == 0
    blocks = tm * D_MODEL * 2 + D_MODEL * tn * 2 + 2 * tm * tn * 4
    return pl.pallas_call(
        _out_proj_kernel,
        out_shape=jax.ShapeDtypeStruct((m, d), F32),
        grid=(m // tm, d // tn),
        in_specs=[
            pl.BlockSpec((tm, TOK_WIDTH), lambda i, j: (i, 0)),
            pl.BlockSpec((tm, XA_WIDTH), lambda i, j: (i, 0)),
            pl.BlockSpec((TOK_WIDTH, tn), lambda i, j: (0, j)),
            pl.BlockSpec((XA_WIDTH, tn), lambda i, j: (TOK_WIDTH // XA_WIDTH, j)),
            pl.BlockSpec((tm, tn), lambda i, j: (i, j)),
        ],
        out_specs=pl.BlockSpec((tm, tn), lambda i, j: (i, j)),
        compiler_params=_params(("parallel", "parallel"), _vmem_limit(blocks, 0, tm * tn * 4)),
        name="out_proj",
    )(mix, xa, w, w, res)


def _compare_exchange(a, i, l, descending):
    hi, lo = jnp.maximum(a[i], a[l]), jnp.minimum(a[i], a[l])
    a[i], a[l] = (hi, lo) if descending else (lo, hi)


def _bitonic_merge_desc(a):
    n = len(a)
    j = n // 2
    while j >= 1:
        for i in range(n):
            if i ^ j > i:
                _compare_exchange(a, i, i ^ j, True)
        j //= 2
    return a


def _top_values(s, count):
    assert s.shape[0] == SUBLANE * count and count & (count - 1) == 0
    a = [s[SUBLANE * v:SUBLANE * (v + 1), :] for v in range(count)]
    k = 2
    while k <= count:
        j = k // 2
        while j >= 1:
            for i in range(count):
                if i ^ j > i:
                    _compare_exchange(a, i, i ^ j, (i & k) == 0)
            j //= 2
        k *= 2
    shift = SUBLANE // 2
    while shift >= 1:
        other = [pltpu.roll(x, shift, 0) for x in a]
        a = _bitonic_merge_desc([jnp.maximum(a[i], other[count - 1 - i]) for i in range(count)])
        shift //= 2
    return [x[0:1, :] for x in a]


def _prefix_count(test, tops):
    n = len(tops)
    assert n & (n - 1) == 0
    bits = []
    count = None
    step = n // 2
    while step >= 1:
        cands = [tops[base + step - 1] for base in range(0, n, 2 * step)]
        for c in reversed(bits):
            cands = [jnp.where(c, cands[2 * i + 1], cands[2 * i]) for i in range(len(cands) // 2)]
        hit = test(cands[0])
        inc = jnp.where(hit, float(step), 0.0)
        count = inc if count is None else count + inc
        bits.append(hit)
        step //= 2
    return count + jnp.where(test(tops[n - 1]), 1.0, 0.0)


def _peer_route_kernel(q_ref, sk_ref, e0_ref, cnt_ref, e1_ref, rnk_ref):
    kk = PEER_TOPK
    half = PEER_QDIM // 2
    e0_heads, cnt_heads = [], []
    for h in range(PEER_HEADS):
        s0 = _nt(sk_ref[2 * h], q_ref[:, (2 * h) * half:(2 * h + 1) * half].astype(BF16))
        s1 = _nt(sk_ref[2 * h + 1], q_ref[:, (2 * h + 1) * half:(2 * h + 2) * half].astype(BF16))
        top0 = _top_values(s0, kk)
        top1 = _top_values(s1, kk)
        cands = [top0[a] + top1[b] for a in range(kk) for b in range(kk) if (a + 1) * (b + 1) <= kk]
        n_pad = -len(cands) % SUBLANE
        cmat = jnp.concatenate(cands + [jnp.full_like(cands[0], -jnp.inf)] * n_pad, axis=0)
        rest = cmat
        for _ in range(kk):
            tau = jnp.max(rest, axis=0, keepdims=True)
            rest = jnp.where(rest == tau, -jnp.inf, rest)
        m_tot = top0[0] + top1[0]
        z = jnp.sum(jnp.where(cmat >= tau, jnp.exp(cmat - m_tot), 0.0), axis=0, keepdims=True)
        cnt = _prefix_count(lambda t: s0 + t >= tau, top1)
        rnk = _prefix_count(lambda t: t > s1, top1)
        e0_heads.append(jnp.exp(s0 - top0[0]) / z)
        cnt_heads.append(cnt)
        e1_ref[h] = jnp.exp(s1 - top1[0]).astype(BF16)
        rnk_ref[h] = rnk.astype(BF16)
    e0_ref[...] = pltpu.einshape("hit->iht", jnp.stack(e0_heads, axis=0))
    cnt_ref[...] = pltpu.einshape("hit->iht", jnp.stack(cnt_heads, axis=0))


def peer_route(q, subkeys, *, tb):
    m, _ = q.shape
    assert m % tb == 0
    sk = subkeys.reshape(PEER_HEADS * 2, PEER_NKEYS, PEER_QDIM // 2).astype(BF16)
    shp = jax.ShapeDtypeStruct((PEER_NKEYS, PEER_HEADS, m), F32)
    shp16 = jax.ShapeDtypeStruct((PEER_HEADS, PEER_NKEYS, m), BF16)
    ospec = pl.BlockSpec((PEER_NKEYS, PEER_HEADS, tb), lambda i: (0, 0, i))
    ospec16 = pl.BlockSpec((PEER_HEADS, PEER_NKEYS, tb), lambda i: (0, 0, i))
    blocks = tb * PEER_HEADS * PEER_QDIM * 4 + sk.size * 2 + 4 * PEER_HEADS * PEER_NKEYS * tb * 4
    return pl.pallas_call(
        _peer_route_kernel,
        out_shape=(shp, shp, shp16, shp16),
        grid=(m // tb,),
        in_specs=[pl.BlockSpec((tb, PEER_HEADS * PEER_QDIM), lambda i: (i, 0)),
                  pl.BlockSpec(sk.shape, lambda i: (0, 0, 0))],
        out_specs=(ospec, ospec, ospec16, ospec16),
        compiler_params=_params(("parallel",), _vmem_limit(blocks, 0, 16 * PEER_NKEYS * tb * 4)),
        name="peer_route",
    )(q, sk)


def _peer_expert_kernel(xn_ref, u_ref, v_ref, e0_ref, cnt_ref, e1_ref, rnk_ref, res_ref, gf_ref, o_ref, *cast_refs,
                        n_i, final_norm):
    et = pl.program_id(1)

    @pl.when(et == 0)
    def _():
        o_ref[...] = res_ref[...]

    u = u_ref[...]
    v = v_ref[...]
    if cast_refs:
        u = u.astype(BF16)
        v = v.astype(BF16)
        cast_refs[0][...] = u
        cast_refs[1][...] = v
    h_t = _nt(u, xn_ref[...])
    act = (0.5 * h_t * (1.0 + lax.erf(h_t * (1.0 / math.sqrt(2.0))))).astype(BF16)
    w_rows = []
    for ii in range(n_i):
        g = jnp.zeros((PEER_NKEYS, h_t.shape[1]), BF16)
        for h in range(PEER_HEADS):
            cnt_i = jnp.broadcast_to(cnt_ref[ii, h:h + 1, :], g.shape).astype(BF16)
            e0_i = jnp.broadcast_to(e0_ref[ii, h:h + 1, :], g.shape).astype(BF16)
            g = g + jnp.where(rnk_ref[h] < cnt_i, e1_ref[h], jnp.zeros((), BF16)) * e0_i
        w_rows.append(g * act[ii * PEER_NKEYS:(ii + 1) * PEER_NKEYS])
    w_t = jnp.concatenate(w_rows, axis=0)
    o_ref[...] += _tn(w_t, v)

    if final_norm:
        @pl.when(et == pl.num_programs(1) - 1)
        def _():
            x = o_ref[...]
            o_ref[...] = x * lax.rsqrt(jnp.mean(x * x, axis=-1, keepdims=True) + EPS) * gf_ref[...]


def peer_experts(xn, u, v, layer, route, res, final_gain, *, tb, te, emit_cast, final_norm):
    m, d = xn.shape
    n_exp = u.shape[-2]
    assert m % tb == 0 and n_exp % te == 0 and te % PEER_NKEYS == 0 and n_exp == PEER_NKEYS * PEER_NKEYS
    assert emit_cast == (u.ndim == 3) and (not emit_cast or m == tb)
    if u.ndim == 3:
        tspec = pl.BlockSpec((None, te, d), lambda i, e: (layer, e, 0))
    else:
        tspec = pl.BlockSpec((te, d), lambda i, e: (e, 0))
    n_i = te // PEER_NKEYS
    ispec = pl.BlockSpec((n_i, PEER_HEADS, tb), lambda i, e: (e, 0, i))
    rspec = pl.BlockSpec((PEER_HEADS, PEER_NKEYS, tb), lambda i, e: (0, 0, i))
    ospec = pl.BlockSpec((tb, d), lambda i, e: (i, 0))
    o_shape = jax.ShapeDtypeStruct((m, d), F32)
    tbytes = u.dtype.itemsize
    blocks = (tb * d * 2 + 2 * te * d * tbytes + 2 * n_i * PEER_HEADS * tb * 4 + 2 * PEER_HEADS * PEER_NKEYS * tb * 2
              + 2 * tb * d * 4 + (2 * te * d * 2 if emit_cast else 0))
    if emit_cast:
        cshape = jax.ShapeDtypeStruct((n_exp, d), BF16)
        cspec = pl.BlockSpec((te, d), lambda i, e: (e, 0))
        out_shape, out_specs = (o_shape, cshape, cshape), (ospec, cspec, cspec)
    else:
        out_shape, out_specs = o_shape, ospec
    return pl.pallas_call(
        functools.partial(_peer_expert_kernel, n_i=n_i, final_norm=final_norm),
        out_shape=out_shape,
        grid=(m // tb, n_exp // te),
        in_specs=[pl.BlockSpec((tb, d), lambda i, e: (i, 0)), tspec, tspec, ispec, ispec, rspec, rspec,
                  pl.BlockSpec((tb, d), lambda i, e: (i, 0)), pl.BlockSpec((1, d), lambda i, e: (0, 0))],
        out_specs=out_specs,
        compiler_params=_params(("parallel", "arbitrary"), _vmem_limit(blocks, 0, 6 * te * tb * 4)),
        name="peer_experts",
    )(xn, u, v, *route, res, final_gain.reshape(1, d))


def peer_layer(x, g, w_q, subkeys, u, v, layer, final_gain, *, tm, tb_route, tb_exp, te, emit_cast, final_norm):
    q, xn = norm_matmul(x, g, w_q, tm=tm, tn=w_q.shape[1], emit_xn=True)
    route = peer_route(q, subkeys, tb=tb_route)
    return peer_experts(xn, u, v, layer, route, x, final_gain, tb=tb_exp, te=te, emit_cast=emit_cast,
                        final_norm=final_norm)


def _pad_gla_weight(w):
    d = w.shape[0]
    n_tok = GLA_A_COL + GLA_RANK
    return jnp.concatenate([w[:, :n_tok], jnp.zeros((d, GLA_QX_COL - n_tok), w.dtype), w[:, n_tok:]],
                           axis=1).astype(BF16)


def kernel(x_prompt, x_sample, state_gla, cache_moba_k, cache_moba_v, cache_mem_k, cache_mem_v, page_table, mem_prompt, ln_mix, ln_mem, ln_ffn, ln_final, w_in_gla, w_alpha_gla, b_alpha_gla, g_head_gla, w_in_moba, rel_bias, w_mem_kv, w_out, w_peer_q, peer_subkeys, peer_u, peer_v):
    bsz, seq, d = x_prompt.shape
    dec_b, dec_t, _ = x_sample.shape
    assert dec_t <= SAMPLE_ROWS
    xp = x_prompt.reshape(bsz * seq, d)
    xs = jnp.pad(x_sample, ((0, 0), (0, SAMPLE_ROWS - dec_t), (0, 0))).reshape(dec_b * SAMPLE_ROWS, d)
    mem2d = mem_prompt.reshape(bsz * N_MEM, d)
    n_dec = dec_b * SAMPLE_ROWS
    cmk = cache_mem_k.reshape(dec_b * DEPTH, N_MEM * XA_HEADS, HEAD_DIM)
    cmv = cache_mem_v.reshape(dec_b * DEPTH, N_MEM * XA_HEADS, HEAD_DIM)

    gla_p, gla_s, kp_l, vp_l, ks_l, vs_l, mkp_l, mvp_l = [], [], [], [], [], [], [], []
    for i in range(DEPTH):
        j = i // N_MIXERS
        kv = norm_matmul(mem2d, ln_mem[i], w_mem_kv[i].astype(BF16), tm=512, tn=512).reshape(bsz, N_MEM, 2 * XA_WIDTH)
        mkp_l.append(kv[..., :XA_WIDTH].reshape(bsz, N_MEM, XA_HEADS, HEAD_DIM))
        mvp_l.append(kv[..., XA_WIDTH:].reshape(bsz, N_MEM, XA_HEADS, HEAD_DIM))
        if i % N_MIXERS == 0:
            w_in = _pad_gla_weight(w_in_gla[j])
            wa = jnp.zeros((LANE, GLA_HEADS * GLA_DK), F32).at[:GLA_RANK].set(w_alpha_gla[j]).astype(BF16)
            pp = norm_matmul(xp, ln_mix[i], w_in, tm=1024, tn=GLA_COLS // 4).reshape(bsz, seq, GLA_COLS)
            ps = norm_matmul(xs, ln_mix[i], w_in, tm=n_dec, tn=512).reshape(dec_b, SAMPLE_ROWS, GLA_COLS)
            zeros0 = jnp.zeros((bsz, 1, GLA_HEADS, GLA_DK, GLA_DV), F32)
            mix_p, s_p = gla_mix(pp, wa, b_alpha_gla[j], g_head_gla[j], zeros0, 0, chunk=GLA_PROMPT_CHUNK, t_valid=seq)
            ps_pad = jnp.pad(ps, ((0, 0), (0, GLA_SAMPLE_CHUNK - SAMPLE_ROWS), (0, 0)))
            mix_s, s_s = gla_mix(ps_pad, wa, b_alpha_gla[j], g_head_gla[j], state_gla, j,
                                 chunk=GLA_SAMPLE_CHUNK, t_valid=dec_t)
            mix_s = mix_s[:, :SAMPLE_ROWS]
            gla_p.append(s_p[:, 0])
            gla_s.append(s_s[:, 0])
            qx_col = GLA_QX_COL
        else:
            w_in = w_in_moba[j].astype(BF16)
            pp = norm_matmul(xp, ln_mix[i], w_in, tm=1024, tn=w_in.shape[1] // 4).reshape(bsz, seq, -1)
            ps = norm_matmul(xs, ln_mix[i], w_in, tm=n_dec, tn=512).reshape(dec_b, SAMPLE_ROWS, -1)
            mix_p, k_hm, v_hm = moba_prompt_mix(pp, rel_bias)
            mix_s = moba_sample_mix(ps, cache_moba_k, cache_moba_v, page_table, j, rel_bias)
            sh = lambda t, n: t.reshape(t.shape[0], n, MOBA_HEADS, HEAD_DIM)
            kp_l.append(jnp.transpose(k_hm, (0, 2, 1, 3)))
            vp_l.append(jnp.transpose(v_hm, (0, 2, 1, 3)))
            ks_l.append(sh(ps[:, :dec_t, TOK_WIDTH:2 * TOK_WIDTH], dec_t))
            vs_l.append(sh(ps[:, :dec_t, 2 * TOK_WIDTH:3 * TOK_WIDTH], dec_t))
            qx_col = MOBA_QX_COL
        kv_rows = kv.reshape(bsz, N_MEM * 2 * XA_HEADS, HEAD_DIM)
        xa_p = cross_attend(pp, qx_col, kv_rows, kv_rows, lambda b: b, stride=2 * XA_HEADS, k_off=0, v_off=XA_HEADS,
                            tq=512)
        xa_s = cross_attend(ps, qx_col, cmk, cmv, lambda b, i=i: b * DEPTH + i, stride=XA_HEADS, k_off=0, v_off=0,
                            tq=SAMPLE_ROWS)
        w_o = w_out[i].astype(BF16)
        xp = out_proj(mix_p.reshape(bsz * seq, TOK_WIDTH), xa_p.reshape(bsz * seq, XA_WIDTH), w_o, xp, tm=512, tn=1024)
        xs = out_proj(mix_s.reshape(n_dec, TOK_WIDTH), xa_s.reshape(n_dec, XA_WIDTH), w_o, xs, tm=n_dec, tn=1024)
        w_q = w_peer_q[i].astype(BF16)
        last = i == DEPTH - 1
        xs, u16, v16 = peer_layer(xs, ln_ffn[i], w_q, peer_subkeys[i], peer_u, peer_v, i, ln_final, tm=n_dec,
                                  tb_route=n_dec, tb_exp=n_dec, te=256, emit_cast=True, final_norm=last)
        xp = peer_layer(xp, ln_ffn[i], w_q, peer_subkeys[i], u16, v16, None, ln_final, tm=512, tb_route=256,
                        tb_exp=512, te=1024, emit_cast=False, final_norm=last)
    y_prompt = xp.reshape(bsz, seq, d)
    y_sample = xs.reshape(dec_b, SAMPLE_ROWS, d)[:, :dec_t]
    return (y_prompt, y_sample,
            jnp.stack(gla_p, axis=1), jnp.stack(gla_s, axis=1),
            jnp.stack(kp_l, axis=1), jnp.stack(vp_l, axis=1),
            jnp.stack(ks_l, axis=1), jnp.stack(vs_l, axis=1),
            jnp.stack(mkp_l, axis=1), jnp.stack(mvp_l, axis=1))
```

```python
import functools
import math

import numpy as np
import jax
import jax.numpy as jnp
from jax import lax
from jax.experimental import pallas as pl
from jax.experimental.pallas import tpu as pltpu

D_MODEL = 2048
DEPTH = 2
N_MIXERS = 2
HEAD_DIM = 128
N_MEM = 256
XA_HEADS = 4
XA_WIDTH = XA_HEADS * HEAD_DIM
TOK_WIDTH = D_MODEL - XA_WIDTH
GLA_HEADS = 6
GLA_DV = TOK_WIDTH // GLA_HEADS
GLA_DK = GLA_DV // 2
GLA_RANK = 16
GLA_TAU = 16.0
GLA_CHUNK = 64
MOBA_HEADS = TOK_WIDTH // HEAD_DIM
MOBA_BLOCK = 256
MOBA_TOPK = 3
REL_BUCKETS = 32
REL_MAX_DIST = 128
PEER_HEADS = 8
PEER_NKEYS = 128
PEER_QDIM = 256
PEER_TOPK = 16
EPS = 1e-6

F32 = jnp.float32
BF16 = jnp.bfloat16

LANE = 128
SUBLANE = 8
VMEM_CAP_BYTES = 56 * 1024 * 1024

NEG = -1e30
_NT = (((1,), (1,)), ((), ()))
_TN = (((0,), (0,)), ((), ()))

GLA_PROMPT_CHUNK = 128
GLA_SAMPLE_CHUNK = 16
SAMPLE_ROWS = 8
MOBA_SAMPLE_BLOCKS_PER_STEP = 4
GLA_HEADS_PER_STEP = 2


def _nt(a, b):
    return lax.dot_general(a, b, _NT, preferred_element_type=F32)


def _tn(a, b):
    return lax.dot_general(a, b, _TN, preferred_element_type=F32)


def _split3(x):
    x1 = x.astype(BF16)
    r1 = x - x1.astype(F32)
    x2 = r1.astype(BF16)
    x3 = (r1 - x2.astype(F32)).astype(BF16)
    return x1, x2, x3


def _nt_hi(a, b):
    a1 = a.astype(BF16)
    a2 = (a - a1.astype(F32)).astype(BF16)
    b1 = b.astype(BF16)
    b2 = (b - b1.astype(F32)).astype(BF16)
    return _nt(a1, b1) + _nt(a1, b2) + _nt(a2, b1)


def _vmem_limit(block_bytes, scratch_bytes=0, temp_bytes=0):
    return int(min(VMEM_CAP_BYTES, 2 * block_bytes + scratch_bytes + temp_bytes + (4 << 20)))


def _params(sem, vmem):
    return pltpu.CompilerParams(dimension_semantics=sem, vmem_limit_bytes=vmem)


def _norm_matmul_kernel(x_ref, g_ref, w_ref, o_ref, xn_ref):
    @pl.when(pl.program_id(1) == 0)
    def _():
        x = x_ref[...]
        y = x * lax.rsqrt(jnp.mean(x * x, axis=-1, keepdims=True) + EPS)
        xn_ref[...] = (y * g_ref[...]).astype(BF16)

    o_ref[...] = jnp.dot(xn_ref[...], w_ref[...], preferred_element_type=F32)


def norm_matmul(x, g, w, *, tm, tn, emit_xn=False):
    m, d = x.shape
    n = w.shape[1]
    assert m % tm == 0 and n % tn == 0 and w.shape[0] == d
    o_shape = jax.ShapeDtypeStruct((m, n), F32)
    o_spec = pl.BlockSpec((tm, tn), lambda i, j: (i, j))
    xn_bytes = tm * d * 2
    blocks = tm * d * 4 + d * 4 + d * tn * 2 + tm * tn * 4
    kwargs = dict(
        grid=(m // tm, n // tn),
        in_specs=[
            pl.BlockSpec((tm, d), lambda i, j: (i, 0)),
            pl.BlockSpec((1, d), lambda i, j: (0, 0)),
            pl.BlockSpec((d, tn), lambda i, j: (0, j)),
        ],
        name="norm_matmul",
    )
    if emit_xn:
        return pl.pallas_call(
            _norm_matmul_kernel,
            out_shape=(o_shape, jax.ShapeDtypeStruct((m, d), BF16)),
            out_specs=(o_spec, pl.BlockSpec((tm, d), lambda i, j: (i, 0))),
            compiler_params=_params(("parallel", "arbitrary"), _vmem_limit(blocks + xn_bytes, 0, tm * d * 8)),
            **kwargs,
        )(x, g.reshape(1, d), w)
    return pl.pallas_call(
        _norm_matmul_kernel,
        out_shape=o_shape,
        out_specs=o_spec,
        scratch_shapes=[pltpu.VMEM((tm, d), BF16)],
        compiler_params=_params(("parallel", "arbitrary"), _vmem_limit(blocks, xn_bytes, tm * d * 8)),
        **kwargs,
    )(x, g.reshape(1, d), w)


def _gla_constants(chunk):
    nlev = int(round(math.log2(chunk)))
    assert 1 << nlev == chunk
    idx = np.arange(chunk)
    col, row = idx[None, :], idx[:, None]
    mats = [col <= row, col > row]
    masks = []
    for lev in range(nlev):
        m = chunk >> (lev + 1)
        grp = idx // (2 * m)
        mid = grp * 2 * m + m
        up = (idx % (2 * m)) >= m
        mats.append((col >= mid[:, None]) & (col <= row) & up[:, None])
        mats.append((col > row) & (col < mid[:, None]) & (~up)[:, None])
        masks.append((grp[:, None] == grp[None, :]) & up[:, None] & (~up)[None, :])
    return (np.concatenate(mats, axis=0).astype(np.float32), np.stack(masks).astype(np.float32))


def _gla_kernel(q_ref, k_ref, v_ref, r_ref, a_ref, wa_ref, ba_ref, gh_ref, s0_ref, cmat_ref, lmask_ref,
                o_ref, sout_ref, st_ref, *, chunk, n_chunks, t_valid, t_total, hps):
    c_rows = chunk
    nlev = int(round(math.log2(chunk)))
    dk, dv = GLA_DK, GLA_DV
    for hh in range(hps):
        st_ref[hh] = jnp.transpose(s0_ref[0, 0, hh])

    def load_head(hh, rows):
        ksl = slice(hh * dk, (hh + 1) * dk)
        vsl = slice(hh * dv, (hh + 1) * dv)
        return (q_ref[0, rows, ksl], k_ref[0, rows, ksl], v_ref[0, rows, vsl], r_ref[0, rows, vsl], st_ref[hh])

    def range_sums(r0, a_blk):
        z = jnp.dot(a_blk, wa_ref[...], preferred_element_type=F32) + ba_ref[...]
        g = (jnp.minimum(z, 0.0) - jnp.log1p(jnp.exp(-jnp.abs(z)))) * (1.0 / GLA_TAU)
        if t_valid < t_total:
            t_idx = r0 + lax.broadcasted_iota(jnp.int32, (c_rows, 1), 0)
            g = jnp.where(t_idx < t_valid, g, 0.0)
        e3 = jnp.dot(cmat_ref[...], jnp.concatenate(_split3(g), axis=1), preferred_element_type=F32)
        w = hps * dk
        return e3[:, 0:w] + e3[:, w:2 * w] + e3[:, 2 * w:3 * w]

    def one_head(hh, e_all, loaded):
        q, k, v, rr, st = loaded
        q = q * (dk ** -0.5)
        e = e_all[:, hh * dk:(hh + 1) * dk]
        qd = (q * jnp.exp(e[0:c_rows])).astype(BF16)
        kd = (k * jnp.exp(e[c_rows:2 * c_rows])).astype(BF16)
        att = jnp.zeros((c_rows, c_rows), F32)
        for lev in range(nlev):
            ea = e[(2 + 2 * lev) * c_rows:(3 + 2 * lev) * c_rows]
            eb = e[(3 + 2 * lev) * c_rows:(4 + 2 * lev) * c_rows]
            ql = (q * jnp.exp(ea)).astype(BF16)
            kl = (k * jnp.exp(eb)).astype(BF16)
            att = att + lmask_ref[lev] * _nt(ql, kl)
        vb = v.astype(BF16)
        o = jnp.dot(att.astype(BF16), vb, preferred_element_type=F32)
        o = o + jnp.sum(q * k, axis=-1, keepdims=True) * v
        o = o + _nt(qd, st.astype(BF16))
        b_last = e[c_rows - 1:c_rows, :]
        st_new = st * jnp.exp(b_last) + _tn(vb, kd)
        on = o * lax.rsqrt(jnp.mean(o * o, axis=-1, keepdims=True) + EPS) * gh_ref[...]
        return (on * (rr * jax.nn.sigmoid(rr))).astype(BF16), st_new

    def body(c, carry):
        r0 = pl.multiple_of(c * c_rows, c_rows)
        rows = pl.ds(r0, c_rows)
        a_blk = a_ref[0, rows, :].astype(BF16)
        loaded = [load_head(hh, rows) for hh in range(hps)]
        e_all = range_sums(r0, a_blk)
        results = [one_head(hh, e_all, loaded[hh]) for hh in range(hps)]
        for hh, (out, st_new) in enumerate(results):
            o_ref[0, rows, hh * dv:(hh + 1) * dv] = out
            st_ref[hh] = st_new
        return carry

    lax.fori_loop(0, n_chunks, body, 0)
    for hh in range(hps):
        sout_ref[0, 0, hh] = jnp.transpose(st_ref[hh])


def gla_mix(p, w_alpha_pad, b_alpha, g_head, state0, state_layer, *, chunk, t_valid):
    bsz, t_total, _ = p.shape
    assert t_total % chunk == 0
    cmat, lmask = _gla_constants(chunk)
    n_e = cmat.shape[0]
    dk, dv = GLA_DK, GLA_DV
    hps = GLA_HEADS_PER_STEP
    n_hg = GLA_HEADS // hps
    kblk = lambda off: (lambda b, h: (b, 0, off + h))
    blocks = hps * (t_total * (2 * dk + 2 * dv) * 4 + t_total * dv * 2 + 2 * dk * dv * 4) + t_total * LANE * 4
    mix, s_out = pl.pallas_call(
        functools.partial(_gla_kernel, chunk=chunk, n_chunks=t_total // chunk, t_valid=t_valid, t_total=t_total,
                          hps=hps),
        out_shape=(jax.ShapeDtypeStruct((bsz, t_total, GLA_HEADS * dv), BF16),
                   jax.ShapeDtypeStruct((bsz, 1, GLA_HEADS, dk, dv), F32)),
        grid=(bsz, n_hg),
        in_specs=[
            pl.BlockSpec((1, t_total, hps * dk), kblk(0)),
            pl.BlockSpec((1, t_total, hps * dk), kblk(n_hg)),
            pl.BlockSpec((1, t_total, hps * dv), kblk(n_hg)),
            pl.BlockSpec((1, t_total, hps * dv), kblk(2 * n_hg)),
            pl.BlockSpec((1, t_total, LANE), lambda b, h: (b, 0, GLA_A_COL // LANE)),
            pl.BlockSpec((LANE, hps * dk), lambda b, h: (0, h)),
            pl.BlockSpec((1, hps * dk), lambda b, h: (0, h)),
            pl.BlockSpec((1, dv), lambda b, h: (0, 0)),
            pl.BlockSpec((1, 1, hps, dk, dv), lambda b, h: (b, state_layer, h, 0, 0)),
            pl.BlockSpec((n_e, chunk), lambda b, h: (0, 0)),
            pl.BlockSpec(lmask.shape, lambda b, h: (0, 0, 0)),
        ],
        out_specs=(pl.BlockSpec((1, t_total, hps * dv), lambda b, h: (b, 0, h)),
                   pl.BlockSpec((1, 1, hps, dk, dv), lambda b, h: (b, 0, h, 0, 0))),
        scratch_shapes=[pltpu.VMEM((hps, dv, dk), F32)],
        compiler_params=_params(("parallel", "parallel"),
                                _vmem_limit(blocks, hps * dv * dk * 4, 2 * n_e * 3 * hps * dk * 4 + (4 << 20))),
        name="gla_mix",
    )(p, p, p, p, p, w_alpha_pad, b_alpha.reshape(1, -1), g_head.reshape(1, -1), state0,
      jnp.asarray(cmat, BF16), jnp.asarray(lmask, F32))
    return mix, s_out


GLA_A_COL = 2 * GLA_HEADS * GLA_DK + 2 * GLA_HEADS * GLA_DV
GLA_QX_COL = GLA_A_COL + XA_WIDTH
GLA_COLS = GLA_QX_COL + XA_WIDTH
MOBA_QX_COL = 3 * TOK_WIDTH


def _bucket_thresholds(max_dist):
    n = np.arange(max_dist + 1)
    max_exact = REL_BUCKETS // 2
    nf = np.maximum(n, 1).astype(np.float32)
    large = max_exact + (np.log(nf / np.float32(max_exact)) / np.float32(math.log(REL_MAX_DIST / max_exact))
                         * np.float32(REL_BUCKETS - max_exact)).astype(np.int32)
    large = np.minimum(large, REL_BUCKETS - 1)
    bucket = np.where(n < max_exact, n, large)
    assert np.all(np.diff(bucket) >= 0) and bucket[-1] == REL_BUCKETS - 1
    return [int(np.argmax(bucket >= b)) for b in range(REL_BUCKETS)]


def _moba_prompt_kernel(rb_ref, q_ref, k_ref, v_ref, o_ref, ko_ref, vo_ref, km_ref, vt_ref, bias_ref, *, nb, thr):
    h = pl.program_id(1)
    qi = pl.program_id(2)
    blk = MOBA_BLOCK
    n_sel = max(1, min(MOBA_TOPK, nb - 1))
    tk = lax.broadcasted_iota(jnp.int32, (blk, blk), 0)
    tq = lax.broadcasted_iota(jnp.int32, (blk, blk), 1)

    @pl.when(qi == 0)
    def _():
        ko_ref[0, 0] = k_ref[0]
        vo_ref[0, 0] = v_ref[0]
        for n in range(nb):
            km_ref[n:n + 1, :] = jnp.mean(k_ref[0, n * blk:(n + 1) * blk, :], axis=0, keepdims=True)
            vt_ref[n] = jnp.transpose(v_ref[0, n * blk:(n + 1) * blk, :]).astype(BF16)
        for which in range(2):
            d = tq - tk + which * blk
            bias = jnp.full((blk, blk), rb_ref[h, 0], F32)
            for bkt in range(1, REL_BUCKETS):
                bias = jnp.where(d >= thr[bkt], rb_ref[h, bkt], bias)
            bias_ref[which] = jnp.where(d >= 0, bias, NEG)

    q = q_ref[0]
    gate = _nt_hi(km_ref[...], q)
    gs = [gate[n:n + 1, :] for n in range(nb)]
    sels = []
    for n in range(nb):
        rank = jnp.zeros((1, blk), jnp.int32)
        for m in range(nb):
            if m != n:
                beats = (gs[m] >= gs[n]) if m < n else (gs[m] > gs[n])
                rank = rank + jnp.where(beats, 1, 0) * (m < qi).astype(jnp.int32)
        sels.append(jnp.where(rank < n_sel, 1, 0) * (n < qi).astype(jnp.int32))

    qb = q.astype(BF16)
    scale = HEAD_DIM ** -0.5
    far_bias = rb_ref[h, REL_BUCKETS - 1]

    def attend(n_tiles):
        tiles, blocks = [], []
        for dd in range(n_tiles):
            j = jnp.maximum(qi - dd, 0)
            kj = k_ref[0, pl.ds(pl.multiple_of(j * blk, blk), blk), :]
            s = _nt(kj.astype(BF16), qb) * scale
            if dd == 0:
                s = s + bias_ref[0]
            else:
                chosen = jnp.zeros((1, blk), jnp.int32)
                for n in range(nb):
                    chosen = chosen + sels[n] * (j == n).astype(jnp.int32)
                chosen = chosen * (dd <= qi).astype(jnp.int32)
                row = jnp.where(chosen > 0, 0.0 if dd == 1 else far_bias, NEG)
                s = (s + bias_ref[1] + row) if dd == 1 else (s + row)
            tiles.append(s)
            blocks.append(j)
        m = jnp.max(tiles[0], axis=0, keepdims=True)
        for s in tiles[1:]:
            m = jnp.maximum(m, jnp.max(s, axis=0, keepdims=True))
        l = jnp.zeros((1, blk), F32)
        acc = jnp.zeros((HEAD_DIM, blk), F32)
        for j, s in zip(blocks, tiles):
            p = jnp.exp(s - m)
            l = l + jnp.sum(p, axis=0, keepdims=True)
            acc = acc + jnp.dot(vt_ref[j], p.astype(BF16), preferred_element_type=F32)
        o_ref[0] = jnp.transpose(acc / l).astype(BF16)

    half = (nb + 1) // 2
    if half < nb:
        @pl.when(qi < half)
        def _():
            attend(half)

        @pl.when(qi >= half)
        def _():
            attend(nb)
    else:
        attend(nb)


def moba_prompt_mix(p, rel_bias):
    bsz, t_total, _ = p.shape
    blk = MOBA_BLOCK
    assert t_total % blk == 0
    nb = t_total // blk
    thr = _bucket_thresholds(2 * blk)
    assert thr[REL_BUCKETS - 1] <= blk + 1
    hd = HEAD_DIM
    blocks = blk * hd * 4 + 4 * t_total * hd * 4 + blk * hd * 2
    scratch = SUBLANE * hd * 4 + hd * t_total * 2 + 2 * blk * blk * 4
    kv_shape = jax.ShapeDtypeStruct((bsz, MOBA_HEADS, t_total, hd), F32)
    kv_spec = pl.BlockSpec((1, 1, t_total, hd), lambda b, h, i: (b, h, 0, 0))
    return pl.pallas_call(
        functools.partial(_moba_prompt_kernel, nb=nb, thr=thr),
        out_shape=(jax.ShapeDtypeStruct((bsz, t_total, MOBA_HEADS * hd), BF16), kv_shape, kv_shape),
        grid=(bsz, MOBA_HEADS, nb),
        in_specs=[
            pl.BlockSpec(memory_space=pltpu.SMEM),
            pl.BlockSpec((1, blk, hd), lambda b, h, i: (b, i, h)),
            pl.BlockSpec((1, t_total, hd), lambda b, h, i: (b, 0, MOBA_HEADS + h)),
            pl.BlockSpec((1, t_total, hd), lambda b, h, i: (b, 0, 2 * MOBA_HEADS + h)),
        ],
        out_specs=(pl.BlockSpec((1, blk, hd), lambda b, h, i: (b, i, h)), kv_spec, kv_spec),
        scratch_shapes=[
            pltpu.VMEM((nb, hd), F32),
            pltpu.VMEM((nb, hd, blk), BF16),
            pltpu.VMEM((2, blk, blk), F32),
        ],
        compiler_params=_params(("parallel", "parallel", "arbitrary"), _vmem_limit(blocks, scratch, 2 * nb * blk * blk * 4)),
        name="moba_prompt",
    )(jnp.transpose(rel_bias), p, p, p)


def _moba_sample_kernel(pt_ref, q_ref, kn_ref, vn_ref, *rest, n_blocks, bps):
    del pt_ref
    npg = 2 * bps
    ck_refs, cv_refs = rest[:npg], rest[npg:2 * npg]
    (rbx_ref, o_ref, qbd_ref, ksum_ref, ma_ref, la_ref, oall_ref, blast_ref, bown_ref, kown_ref,
     vown_ref) = rest[2 * npg:]
    step = pl.program_id(1)
    n_steps = n_blocks // bps
    hd = HEAD_DIM
    blk = MOBA_BLOCK
    nrow = LANE
    width = MOBA_HEADS * hd
    scale = hd ** -0.5
    row = lax.broadcasted_iota(jnp.int32, (nrow, LANE), 0)
    lane = lax.broadcasted_iota(jnp.int32, (nrow, LANE), 1)
    t_row = row % SAMPLE_ROWS
    thr = _bucket_thresholds(blk + SAMPLE_ROWS)
    assert thr[REL_BUCKETS - 1] <= blk + 1

    @pl.when(step == 0)
    def _():
        q8 = q_ref[0]
        rowg = lax.broadcasted_iota(jnp.int32, (nrow, width), 0) // SAMPLE_ROWS
        colg = lax.broadcasted_iota(jnp.int32, (nrow, width), 1) // hd
        qbd_ref[...] = jnp.where(rowg == colg, jnp.concatenate([q8] * (nrow // SAMPLE_ROWS), axis=0), 0.0)
        ksum_ref[...] = jnp.zeros(ksum_ref.shape, F32)
        ma_ref[...] = jnp.full(ma_ref.shape, NEG, F32)
        la_ref[...] = jnp.zeros(la_ref.shape, F32)
        dlast = (blk + lax.broadcasted_iota(jnp.int32, (nrow, blk), 0) % SAMPLE_ROWS
                 - lax.broadcasted_iota(jnp.int32, (nrow, blk), 1))
        bias = jnp.broadcast_to(rbx_ref[:, 0:1], (nrow, blk))
        for bkt in range(1, REL_BUCKETS):
            bias = jnp.where(dlast >= thr[bkt], rbx_ref[:, bkt:bkt + 1], bias)
        blast_ref[...] = bias
        down = t_row - lane
        bias = jnp.zeros((nrow, LANE), F32)
        for dd in range(SAMPLE_ROWS):
            bias = jnp.where(down == dd, rbx_ref[:, dd:dd + 1], bias)
        bown_ref[...] = bias
        kown_ref[...] = jnp.zeros(kown_ref.shape, F32)
        vown_ref[...] = jnp.zeros(vown_ref.shape, F32)
        kown_ref[0:SAMPLE_ROWS, :] = kn_ref[0]
        vown_ref[0:SAMPLE_ROWS, :] = vn_ref[0]

    qbd = qbd_ref[...].astype(BF16)
    rowg128 = lax.broadcasted_iota(jnp.int32, (nrow, hd), 0) // SAMPLE_ROWS

    def partial_softmax(kb, vb, bias, valid):
        s = _nt(qbd, kb) * scale + bias
        if valid is not None:
            s = jnp.where(valid, s, NEG)
        m = jnp.max(s, axis=-1, keepdims=True)
        p = jnp.exp(s - m)
        if valid is not None:
            p = jnp.where(valid, p, 0.0)
        l = jnp.sum(p, axis=-1, keepdims=True)
        o_full = jnp.dot(p.astype(BF16), vb, preferred_element_type=F32)
        o = jnp.zeros((nrow, hd), F32)
        for hh in range(MOBA_HEADS):
            o = o + jnp.where(rowg128 == hh, o_full[:, hh * hd:(hh + 1) * hd], 0.0)
        return m, l, o

    def page_rows(ref):
        return jnp.concatenate([ref[hh] for hh in range(MOBA_HEADS)], axis=1)

    far_bias = rbx_ref[:, REL_BUCKETS - 1:REL_BUCKETS]
    ma_new = ma_ref[...]
    la_new = la_ref[...]
    for r in range(bps):
        b_idx = step * bps + r
        kb = jnp.concatenate([page_rows(ck_refs[2 * r]), page_rows(ck_refs[2 * r + 1])], axis=0)
        vb = jnp.concatenate([page_rows(cv_refs[2 * r]), page_rows(cv_refs[2 * r + 1])], axis=0)
        ksum_ref[pl.ds(b_idx, 1), :] = jnp.sum(kb, axis=0, keepdims=True)
        if r == bps - 1:
            bias = jnp.where(step == n_steps - 1, blast_ref[...], jnp.broadcast_to(far_bias, (nrow, blk)))
        else:
            bias = far_bias
        m, l, o = partial_softmax(kb.astype(BF16), vb.astype(BF16), bias, None)
        oall_ref[b_idx] = o
        ma_new = jnp.where(lane == b_idx, m, ma_new)
        la_new = jnp.where(lane == b_idx, l, la_new)
    ma_ref[...] = ma_new
    la_ref[...] = la_new

    @pl.when(step == n_steps - 1)
    def _():
        valid_own = jnp.logical_and(lane <= t_row, lane < SAMPLE_ROWS)
        m_own, l_own, o_own = partial_softmax(kown_ref[...].astype(BF16), vown_ref[...].astype(BF16),
                                              bown_ref[...], valid_own)
        kmean = ksum_ref[...] * (1.0 / MOBA_BLOCK)
        gate = _nt_hi(qbd_ref[...], kmean)
        g = jnp.where(lane < n_blocks, gate, -jnp.inf)
        sel_i = jnp.zeros((nrow, LANE), jnp.int32)
        for _ in range(max(1, min(MOBA_TOPK, n_blocks))):
            gmax = jnp.max(g, axis=-1, keepdims=True)
            first = jnp.min(jnp.where(g == gmax, lane, LANE), axis=-1, keepdims=True)
            pick = lane == first
            sel_i = jnp.where(pick, 1, sel_i)
            g = jnp.where(pick, -jnp.inf, g)
        sel = sel_i > 0
        ma = jnp.where(sel, ma_new, NEG)
        m_all = jnp.maximum(jnp.max(ma, axis=-1, keepdims=True), m_own)
        wa = jnp.where(sel, jnp.exp(ma - m_all), 0.0)
        w_own = jnp.exp(m_own - m_all)
        den = jnp.sum(wa * la_new, axis=-1, keepdims=True) + w_own * l_own
        num = w_own * o_own
        for n in range(n_blocks):
            num = num + wa[:, n:n + 1] * oall_ref[n]
        out = num / den
        o_ref[0] = jnp.concatenate([out[hh * SAMPLE_ROWS:(hh + 1) * SAMPLE_ROWS, :] for hh in range(MOBA_HEADS)],
                                   axis=1).astype(BF16)


def moba_sample_mix(p, cache_k, cache_v, page_table, layer, rel_bias):
    bs, rows, _ = p.shape
    n_pool, n_layers, page, heads, hd = cache_k.shape
    n_pages = page_table.shape[1]
    past_len = n_pages * page
    assert rows == SAMPLE_ROWS and heads == MOBA_HEADS and hd == HEAD_DIM and MOBA_BLOCK == 2 * page
    assert past_len % MOBA_BLOCK == 0 and MOBA_HEADS * SAMPLE_ROWS <= LANE and past_len // MOBA_BLOCK <= LANE
    n_blocks = past_len // MOBA_BLOCK
    bps = MOBA_SAMPLE_BLOCKS_PER_STEP if n_blocks % MOBA_SAMPLE_BLOCKS_PER_STEP == 0 else 1
    npg = 2 * bps
    width = heads * hd
    rbx = jnp.zeros((LANE, LANE), F32).at[:heads * SAMPLE_ROWS, :REL_BUCKETS].set(
        jnp.repeat(jnp.transpose(rel_bias), SAMPLE_ROWS, axis=0))

    def pmap(r):
        return lambda b, g, pt: (pt[b, g * npg + r], layer, 0, 0, 0)

    ck = jnp.transpose(cache_k, (0, 1, 3, 2, 4))
    cv = jnp.transpose(cache_v, (0, 1, 3, 2, 4))
    page_specs = [pl.BlockSpec((None, None, heads, page, hd), pmap(r)) for r in range(npg)]
    blocks = 3 * rows * width * 4 + 2 * npg * page * width * 4 + LANE * LANE * 4 + rows * width * 2
    scratch = (LANE * width * 4 * 4 + 3 * LANE * LANE * 4 + LANE * MOBA_BLOCK * 4 + n_blocks * LANE * hd * 4)
    return pl.pallas_call(
        functools.partial(_moba_sample_kernel, n_blocks=n_blocks, bps=bps),
        out_shape=jax.ShapeDtypeStruct((bs, rows, width), BF16),
        grid_spec=pltpu.PrefetchScalarGridSpec(
            num_scalar_prefetch=1,
            grid=(bs, n_blocks // bps),
            in_specs=[
                pl.BlockSpec((1, rows, width), lambda b, g, pt: (b, 0, 0)),
                pl.BlockSpec((1, rows, width), lambda b, g, pt: (b, 0, 1)),
                pl.BlockSpec((1, rows, width), lambda b, g, pt: (b, 0, 2)),
                *page_specs, *page_specs,
                pl.BlockSpec((LANE, LANE), lambda b, g, pt: (0, 0)),
            ],
            out_specs=pl.BlockSpec((1, rows, width), lambda b, g, pt: (b, 0, 0)),
            scratch_shapes=[
                pltpu.VMEM((LANE, width), F32),
                pltpu.VMEM((LANE, width), F32),
                pltpu.VMEM((LANE, LANE), F32), pltpu.VMEM((LANE, LANE), F32),
                pltpu.VMEM((n_blocks, LANE, hd), F32),
                pltpu.VMEM((LANE, MOBA_BLOCK), F32), pltpu.VMEM((LANE, LANE), F32),
                pltpu.VMEM((LANE, width), F32), pltpu.VMEM((LANE, width), F32),
            ],
        ),
        compiler_params=_params(("parallel", "arbitrary"), _vmem_limit(blocks, scratch, 12 << 20)),
        name="moba_sample",
    )(page_table, p, p, p, *([ck] * npg), *([cv] * npg), rbx)


def _xattn_kernel(q_ref, mk_ref, mv_ref, o_ref, *, stride, k_off, v_off):
    scale = HEAD_DIM ** -0.5
    for h in range(XA_HEADS):
        sl = slice(h * HEAD_DIM, (h + 1) * HEAD_DIM)
        mk = mk_ref[pl.ds(k_off + h, N_MEM, stride=stride), :].astype(BF16)
        mv = mv_ref[pl.ds(v_off + h, N_MEM, stride=stride), :].astype(BF16)
        s = _nt(q_ref[0, :, sl].astype(BF16), mk) * scale
        p = jnp.exp(s - jnp.max(s, axis=-1, keepdims=True))
        o = jnp.dot(p.astype(BF16), mv, preferred_element_type=F32)
        o_ref[0, :, sl] = (o / jnp.sum(p, axis=-1, keepdims=True)).astype(BF16)


def cross_attend(p, qx_col, mk, mv, mem_idx, *, stride, k_off, v_off, tq):
    bsz, t_total, _ = p.shape
    assert t_total % tq == 0 and qx_col % XA_WIDTH == 0
    mem_spec = pl.BlockSpec((None, N_MEM * stride, HEAD_DIM), lambda b, i: (mem_idx(b), 0, 0))
    mk_spec = mv_spec = mem_spec
    blocks = tq * XA_WIDTH * 4 + 2 * N_MEM * stride * HEAD_DIM * 4 + tq * XA_WIDTH * 2
    return pl.pallas_call(
        functools.partial(_xattn_kernel, stride=stride, k_off=k_off, v_off=v_off),
        out_shape=jax.ShapeDtypeStruct((bsz, t_total, XA_WIDTH), BF16),
        grid=(bsz, t_total // tq),
        in_specs=[
            pl.BlockSpec((1, tq, XA_WIDTH), lambda b, i: (b, i, qx_col // XA_WIDTH)),
            mk_spec,
            mv_spec,
        ],
        out_specs=pl.BlockSpec((1, tq, XA_WIDTH), lambda b, i: (b, i, 0)),
        compiler_params=_params(("parallel", "parallel"), _vmem_limit(blocks, 0, 8 * tq * N_MEM * 4)),
        name="cross_attend",
    )(p, mk, mv)


def _out_proj_kernel(mix_ref, xa_ref, w1_ref, w2_ref, res_ref, o_ref):
    o_ref[...] = (res_ref[...] + jnp.dot(mix_ref[...], w1_ref[...], preferred_element_type=F32)
                  + jnp.dot(xa_ref[...], w2_ref[...], preferred_element_type=F32))


def out_proj(mix, xa, w, res, *, tm, tn):
    m, d = res.shape
    assert m % tm == 0 and d % tn == 0 and TOK_WIDTH % XA_WIDTH == 0
    blocks = tm * D_MODEL * 2 + D_MODEL * tn * 2 + 2 * tm * tn * 4
    return pl.pallas_call(
        _out_proj_kernel,
        out_shape=jax.ShapeDtypeStruct((m, d), F32),
        grid=(m // tm, d // tn),
        in_specs=[
            pl.BlockSpec((tm, TOK_WIDTH), lambda i, j: (i, 0)),
            pl.BlockSpec((tm, XA_WIDTH), lambda i, j: (i, 0)),
            pl.BlockSpec((TOK_WIDTH, tn), lambda i, j: (0, j)),
            pl.BlockSpec((XA_WIDTH, tn), lambda i, j: (TOK_WIDTH // XA_WIDTH, j)),
            pl.BlockSpec((tm, tn), lambda i, j: (i, j)),
        ],
        out_specs=pl.BlockSpec((tm, tn), lambda i, j: (i, j)),
        compiler_params=_params(("parallel", "parallel"), _vmem_limit(blocks, 0, tm * tn * 4)),
        name="out_proj",
    )(mix, xa, w, w, res)


def _compare_exchange(a, i, l, descending):
    hi, lo = jnp.maximum(a[i], a[l]), jnp.minimum(a[i], a[l])
    a[i], a[l] = (hi, lo) if descending else (lo, hi)


def _bitonic_merge_desc(a):
    n = len(a)
    j = n // 2
    while j >= 1:
        for i in range(n):
            if i ^ j > i:
                _compare_exchange(a, i, i ^ j, True)
        j //= 2
    return a


def _top_values(s, count):
    assert s.shape[0] == SUBLANE * count and count & (count - 1) == 0
    a = [s[SUBLANE * v:SUBLANE * (v + 1), :] for v in range(count)]
    k = 2
    while k <= count:
        j = k // 2
        while j >= 1:
            for i in range(count):
                if i ^ j > i:
                    _compare_exchange(a, i, i ^ j, (i & k) == 0)
            j //= 2
        k *= 2
    shift = SUBLANE // 2
    while shift >= 1:
        other = [pltpu.roll(x, shift, 0) for x in a]
        a = _bitonic_merge_desc([jnp.maximum(a[i], other[count - 1 - i]) for i in range(count)])
        shift //= 2
    return [x[0:1, :] for x in a]


def _prefix_count(test, tops):
    n = len(tops)
    assert n & (n - 1) == 0
    bits = []
    count = None
    step = n // 2
    while step >= 1:
        cands = [tops[base + step - 1] for base in range(0, n, 2 * step)]
        for c in reversed(bits):
            cands = [jnp.where(c, cands[2 * i + 1], cands[2 * i]) for i in range(len(cands) // 2)]
        hit = test(cands[0])
        inc = jnp.where(hit, float(step), 0.0)
        count = inc if count is None else count + inc
        bits.append(hit)
        step //= 2
    return count + jnp.where(test(tops[n - 1]), 1.0, 0.0)


def _peer_route_kernel(q_ref, sk_ref, e0_ref, cnt_ref, e1_ref, rnk_ref):
    kk = PEER_TOPK
    half = PEER_QDIM // 2
    e0_heads, cnt_heads = [], []
    for h in range(PEER_HEADS):
        s0 = _nt(sk_ref[2 * h], q_ref[:, (2 * h) * half:(2 * h + 1) * half].astype(BF16))
        s1 = _nt(sk_ref[2 * h + 1], q_ref[:, (2 * h + 1) * half:(2 * h + 2) * half].astype(BF16))
        top0 = _top_values(s0, kk)
        top1 = _top_values(s1, kk)
        cands = [top0[a] + top1[b] for a in range(kk) for b in range(kk) if (a + 1) * (b + 1) <= kk]
        n_pad = -len(cands) % SUBLANE
        cmat = jnp.concatenate(cands + [jnp.full_like(cands[0], -jnp.inf)] * n_pad, axis=0)
        rest = cmat
        for _ in range(kk):
            tau = jnp.max(rest, axis=0, keepdims=True)
            rest = jnp.where(rest == tau, -jnp.inf, rest)
        m_tot = top0[0] + top1[0]
        z = jnp.sum(jnp.where(cmat >= tau, jnp.exp(cmat - m_tot), 0.0), axis=0, keepdims=True)
        cnt = _prefix_count(lambda t: s0 + t >= tau, top1)
        rnk = _prefix_count(lambda t: t > s1, top1)
        e0_heads.append(jnp.exp(s0 - top0[0]) / z)
        cnt_heads.append(cnt)
        e1_ref[h] = jnp.exp(s1 - top1[0]).astype(BF16)
        rnk_ref[h] = rnk.astype(BF16)
    e0_ref[...] = pltpu.einshape("hit->iht", jnp.stack(e0_heads, axis=0))
    cnt_ref[...] = pltpu.einshape("hit->iht", jnp.stack(cnt_heads, axis=0))


def peer_route(q, subkeys, *, tb):
    m, _ = q.shape
    assert m % tb == 0
    sk = subkeys.reshape(PEER_HEADS * 2, PEER_NKEYS, PEER_QDIM // 2).astype(BF16)
    shp = jax.ShapeDtypeStruct((PEER_NKEYS, PEER_HEADS, m), F32)
    shp16 = jax.ShapeDtypeStruct((PEER_HEADS, PEER_NKEYS, m), BF16)
    ospec = pl.BlockSpec((PEER_NKEYS, PEER_HEADS, tb), lambda i: (0, 0, i))
    ospec16 = pl.BlockSpec((PEER_HEADS, PEER_NKEYS, tb), lambda i: (0, 0, i))
    blocks = tb * PEER_HEADS * PEER_QDIM * 4 + sk.size * 2 + 4 * PEER_HEADS * PEER_NKEYS * tb * 4
    return pl.pallas_call(
        _peer_route_kernel,
        out_shape=(shp, shp, shp16, shp16),
        grid=(m // tb,),
        in_specs=[pl.BlockSpec((tb, PEER_HEADS * PEER_QDIM), lambda i: (i, 0)),
                  pl.BlockSpec(sk.shape, lambda i: (0, 0, 0))],
        out_specs=(ospec, ospec, ospec16, ospec16),
        compiler_params=_params(("parallel",), _vmem_limit(blocks, 0, 16 * PEER_NKEYS * tb * 4)),
        name="peer_route",
    )(q, sk)


def _peer_expert_kernel(xn_ref, u_ref, v_ref, e0_ref, cnt_ref, e1_ref, rnk_ref, res_ref, gf_ref, o_ref, *cast_refs,
                        n_i, final_norm):
    et = pl.program_id(1)

    @pl.when(et == 0)
    def _():
        o_ref[...] = res_ref[...]

    u = u_ref[...]
    v = v_ref[...]
    if cast_refs:
        u = u.astype(BF16)
        v = v.astype(BF16)
        cast_refs[0][...] = u
        cast_refs[1][...] = v
    h_t = _nt(u, xn_ref[...])
    act = (0.5 * h_t * (1.0 + lax.erf(h_t * (1.0 / math.sqrt(2.0))))).astype(BF16)
    w_rows = []
    for ii in range(n_i):
        g = jnp.zeros((PEER_NKEYS, h_t.shape[1]), BF16)
        for h in range(PEER_HEADS):
            cnt_i = jnp.broadcast_to(cnt_ref[ii, h:h + 1, :], g.shape).astype(BF16)
            e0_i = jnp.broadcast_to(e0_ref[ii, h:h + 1, :], g.shape).astype(BF16)
            g = g + jnp.where(rnk_ref[h] < cnt_i, e1_ref[h], jnp.zeros((), BF16)) * e0_i
        w_rows.append(g * act[ii * PEER_NKEYS:(ii + 1) * PEER_NKEYS])
    w_t = jnp.concatenate(w_rows, axis=0)
    o_ref[...] += _tn(w_t, v)

    if final_norm:
        @pl.when(et == pl.num_programs(1) - 1)
        def _():
            x = o_ref[...]
            o_ref[...] = x * lax.rsqrt(jnp.mean(x * x, axis=-1, keepdims=True) + EPS) * gf_ref[...]


def peer_experts(xn, u, v, layer, route, res, final_gain, *, tb, te, emit_cast, final_norm):
    m, d = xn.shape
    n_exp = u.shape[-2]
    assert m % tb == 0 and n_exp % te == 0 and te % PEER_NKEYS == 0 and n_exp == PEER_NKEYS * PEER_NKEYS
    assert emit_cast == (u.ndim == 3) and (not emit_cast or m == tb)
    if u.ndim == 3:
        tspec = pl.BlockSpec((None, te, d), lambda i, e: (layer, e, 0))
    else:
        tspec = pl.BlockSpec((te, d), lambda i, e: (e, 0))
    n_i = te // PEER_NKEYS
    ispec = pl.BlockSpec((n_i, PEER_HEADS, tb), lambda i, e: (e, 0, i))
    once = pl.Buffered(1)
    rspec = pl.BlockSpec((PEER_HEADS, PEER_NKEYS, tb), lambda i, e: (0, 0, i), pipeline_mode=once)
    ospec = pl.BlockSpec((tb, d), lambda i, e: (i, 0))
    o_shape = jax.ShapeDtypeStruct((m, d), F32)
    tbytes = u.dtype.itemsize
    blocks = (tb * d * 2 + 2 * te * d * tbytes + 2 * n_i * PEER_HEADS * tb * 4 + 2 * PEER_HEADS * PEER_NKEYS * tb * 2
              + 2 * tb * d * 4 + (2 * te * d * 2 if emit_cast else 0))
    if emit_cast:
        cshape = jax.ShapeDtypeStruct((n_exp, d), BF16)
        cspec = pl.BlockSpec((te, d), lambda i, e: (e, 0))
        out_shape, out_specs = (o_shape, cshape, cshape), (ospec, cspec, cspec)
    else:
        out_shape, out_specs = o_shape, ospec
    return pl.pallas_call(
        functools.partial(_peer_expert_kernel, n_i=n_i, final_norm=final_norm),
        out_shape=out_shape,
        grid=(m // tb, n_exp // te),
        in_specs=[pl.BlockSpec((tb, d), lambda i, e: (i, 0), pipeline_mode=once), tspec, tspec, ispec, ispec, rspec,
                  rspec, pl.BlockSpec((tb, d), lambda i, e: (i, 0), pipeline_mode=once),
                  pl.BlockSpec((1, d), lambda i, e: (0, 0))],
        out_specs=out_specs,
        compiler_params=_params(("parallel", "arbitrary"), _vmem_limit(blocks, 0, 6 * te * tb * 4)),
        name="peer_experts",
    )(xn, u, v, *route, res, final_gain.reshape(1, d))


def peer_layer(x, g, w_q, subkeys, u, v, layer, final_gain, *, tm, tb_route, tb_exp, te, emit_cast, final_norm):
    q, xn = norm_matmul(x, g, w_q, tm=tm, tn=w_q.shape[1], emit_xn=True)
    route = peer_route(q, subkeys, tb=tb_route)
    return peer_experts(xn, u, v, layer, route, x, final_gain, tb=tb_exp, te=te, emit_cast=emit_cast,
                        final_norm=final_norm)


def _pad_gla_weight(w):
    d = w.shape[0]
    n_tok = GLA_A_COL + GLA_RANK
    return jnp.concatenate([w[:, :n_tok], jnp.zeros((d, GLA_QX_COL - n_tok), w.dtype), w[:, n_tok:]],
                           axis=1).astype(BF16)


def kernel(x_prompt, x_sample, state_gla, cache_moba_k, cache_moba_v, cache_mem_k, cache_mem_v, page_table, mem_prompt, ln_mix, ln_mem, ln_ffn, ln_final, w_in_gla, w_alpha_gla, b_alpha_gla, g_head_gla, w_in_moba, rel_bias, w_mem_kv, w_out, w_peer_q, peer_subkeys, peer_u, peer_v):
    bsz, seq, d = x_prompt.shape
    dec_b, dec_t, _ = x_sample.shape
    assert dec_t <= SAMPLE_ROWS
    xp = x_prompt.reshape(bsz * seq, d)
    xs = jnp.pad(x_sample, ((0, 0), (0, SAMPLE_ROWS - dec_t), (0, 0))).reshape(dec_b * SAMPLE_ROWS, d)
    mem2d = mem_prompt.reshape(bsz * N_MEM, d)
    n_dec = dec_b * SAMPLE_ROWS
    cmk = cache_mem_k.reshape(dec_b * DEPTH, N_MEM * XA_HEADS, HEAD_DIM)
    cmv = cache_mem_v.reshape(dec_b * DEPTH, N_MEM * XA_HEADS, HEAD_DIM)

    gla_p, gla_s, kp_l, vp_l, ks_l, vs_l, mkp_l, mvp_l = [], [], [], [], [], [], [], []
    for i in range(DEPTH):
        j = i // N_MIXERS
        kv = norm_matmul(mem2d, ln_mem[i], w_mem_kv[i].astype(BF16), tm=512, tn=512).reshape(bsz, N_MEM, 2 * XA_WIDTH)
        mkp_l.append(kv[..., :XA_WIDTH].reshape(bsz, N_MEM, XA_HEADS, HEAD_DIM))
        mvp_l.append(kv[..., XA_WIDTH:].reshape(bsz, N_MEM, XA_HEADS, HEAD_DIM))
        if i % N_MIXERS == 0:
            w_in = _pad_gla_weight(w_in_gla[j])
            wa = jnp.zeros((LANE, GLA_HEADS * GLA_DK), F32).at[:GLA_RANK].set(w_alpha_gla[j]).astype(BF16)
            pp = norm_matmul(xp, ln_mix[i], w_in, tm=1024, tn=GLA_COLS // 4).reshape(bsz, seq, GLA_COLS)
            ps = norm_matmul(xs, ln_mix[i], w_in, tm=n_dec, tn=512).reshape(dec_b, SAMPLE_ROWS, GLA_COLS)
            zeros0 = jnp.zeros((bsz, 1, GLA_HEADS, GLA_DK, GLA_DV), F32)
            mix_p, s_p = gla_mix(pp, wa, b_alpha_gla[j], g_head_gla[j], zeros0, 0, chunk=GLA_PROMPT_CHUNK, t_valid=seq)
            ps_pad = jnp.pad(ps, ((0, 0), (0, GLA_SAMPLE_CHUNK - SAMPLE_ROWS), (0, 0)))
            mix_s, s_s = gla_mix(ps_pad, wa, b_alpha_gla[j], g_head_gla[j], state_gla, j,
                                 chunk=GLA_SAMPLE_CHUNK, t_valid=dec_t)
            mix_s = mix_s[:, :SAMPLE_ROWS]
            gla_p.append(s_p[:, 0])
            gla_s.append(s_s[:, 0])
            qx_col = GLA_QX_COL
        else:
            w_in = w_in_moba[j].astype(BF16)
            pp = norm_matmul(xp, ln_mix[i], w_in, tm=1024, tn=w_in.shape[1] // 4).reshape(bsz, seq, -1)
            ps = norm_matmul(xs, ln_mix[i], w_in, tm=n_dec, tn=512).reshape(dec_b, SAMPLE_ROWS, -1)
            mix_p, k_hm, v_hm = moba_prompt_mix(pp, rel_bias)
            mix_s = moba_sample_mix(ps, cache_moba_k, cache_moba_v, page_table, j, rel_bias)
            sh = lambda t, n: t.reshape(t.shape[0], n, MOBA_HEADS, HEAD_DIM)
            kp_l.append(jnp.transpose(k_hm, (0, 2, 1, 3)))
            vp_l.append(jnp.transpose(v_hm, (0, 2, 1, 3)))
            ks_l.append(sh(ps[:, :dec_t, TOK_WIDTH:2 * TOK_WIDTH], dec_t))
            vs_l.append(sh(ps[:, :dec_t, 2 * TOK_WIDTH:3 * TOK_WIDTH], dec_t))
            qx_col = MOBA_QX_COL
        kv_rows = kv.reshape(bsz, N_MEM * 2 * XA_HEADS, HEAD_DIM)
        xa_p = cross_attend(pp, qx_col, kv_rows, kv_rows, lambda b: b, stride=2 * XA_HEADS, k_off=0, v_off=XA_HEADS,
                            tq=512)
        xa_s = cross_attend(ps, qx_col, cmk, cmv, lambda b, i=i: b * DEPTH + i, stride=XA_HEADS, k_off=0, v_off=0,
                            tq=SAMPLE_ROWS)
        w_o = w_out[i].astype(BF16)
        xp = out_proj(mix_p.reshape(bsz * seq, TOK_WIDTH), xa_p.reshape(bsz * seq, XA_WIDTH), w_o, xp, tm=512, tn=1024)
        xs = out_proj(mix_s.reshape(n_dec, TOK_WIDTH), xa_s.reshape(n_dec, XA_WIDTH), w_o, xs, tm=n_dec, tn=1024)
        w_q = w_peer_q[i].astype(BF16)
        last = i == DEPTH - 1
        xs, u16, v16 = peer_layer(xs, ln_ffn[i], w_q, peer_subkeys[i], peer_u, peer_v, i, ln_final, tm=n_dec,
                                  tb_route=n_dec, tb_exp=n_dec, te=256, emit_cast=True, final_norm=last)
        xp = peer_layer(xp, ln_ffn[i], w_q, peer_subkeys[i], u16, v16, None, ln_final, tm=512, tb_route=256,
                        tb_exp=1024, te=1024, emit_cast=False, final_norm=last)
    y_prompt = xp.reshape(bsz, seq, d)
    y_sample = xs.reshape(dec_b, SAMPLE_ROWS, d)[:, :dec_t]
    return (y_prompt, y_sample,
            jnp.stack(gla_p, axis=1), jnp.stack(gla_s, axis=1),
            jnp.stack(kp_l, axis=1), jnp.stack(vp_l, axis=1),
            jnp.stack(ks_l, axis=1), jnp.stack(vs_l, axis=1),
            jnp.stack(mkp_l, axis=1), jnp.stack(mvp_l, axis=1))
```
